```python
import math
import jax, jax.numpy as jnp
from jax import lax
import numpy as np

D_MODEL = 1024
BATCH = 16
SEQ = 2048
DEPTH = 1

GLA_HEADS = 4
GLA_DK = D_MODEL // 2
GLA_DV = D_MODEL
GLA_HK = GLA_DK // GLA_HEADS
GLA_HV = GLA_DV // GLA_HEADS
GLA_RANK = 16
GLA_TAU = 16.0
GLA_CHUNK = 64
LOG_DECAY_FLOOR = -1.25

GMLP_WIDTH = D_MODEL
GMLP_GROUPS = 8
GMLP_GC = GMLP_WIDTH // GMLP_GROUPS
GMLP_CHUNK = 128

EPS = 1e-6

IN_SPLITS = (GLA_DK, GLA_DK, GLA_DV, GLA_DV, GLA_RANK, GLA_RANK,
             GMLP_WIDTH, GMLP_WIDTH, GMLP_WIDTH, D_MODEL, D_MODEL)
IN_WIDTH = sum(IN_SPLITS)

kernel_name = 'hybrid_gla_gmlp_adaln_block'


def _split_cols(p, sizes):
    outs = []
    off = 0
    for s in sizes:
        outs.append(p[..., off:off + s])
        off += s
    return outs


def _rmsnorm(x, g):
    xf = x.astype(jnp.float32)
    r = lax.rsqrt(jnp.mean(xf * xf, axis=-1, keepdims=True) + EPS)
    return (xf * r).astype(x.dtype) * g


def _layernorm(x, g, b):
    xf = x.astype(jnp.float32)
    mu = jnp.mean(xf, axis=-1, keepdims=True)
    var = jnp.mean(jnp.square(xf - mu), axis=-1, keepdims=True)
    return ((xf - mu) * lax.rsqrt(var + EPS)).astype(x.dtype) * g + b


def _gla_chunked(q, k, v, log_a, strict):
    B, H, S, dk = q.shape
    dv = v.shape[-1]
    C = GLA_CHUNK
    N = S // C
    q = q.reshape(B, H, N, C, dk)
    k = k.reshape(B, H, N, C, dk)
    v = v.reshape(B, H, N, C, dv)
    b = jnp.cumsum(log_a.reshape(B, H, N, C, dk), axis=3)
    b_last = b[:, :, :, -1:, :]
    q_t = q * jnp.exp(b)
    k_t = k * jnp.exp(-b)
    k_end = k * jnp.exp(b_last - b)
    mask = jnp.tril(jnp.ones((C, C), dtype=bool), k=-1 if strict else 0)
    attn = jnp.where(mask, jnp.einsum('bhntd,bhnsd->bhnts', q_t, k_t), 0.0)
    o_intra = jnp.einsum('bhnts,bhnsv->bhntv', attn, v)
    chunk_kv = jnp.einsum('bhnsd,bhnsv->bhndv', k_end, v)
    chunk_decay = jnp.exp(b_last[:, :, :, 0, :])

    def step(state, inp):
        d, kv = inp
        return d[..., None] * state + kv, state

    init = jnp.zeros((B, H, dk, dv), dtype=q.dtype)
    _, states = lax.scan(step, init, (jnp.moveaxis(chunk_decay, 2, 0), jnp.moveaxis(chunk_kv, 2, 0)))
    states = jnp.moveaxis(states, 0, 2)
    o_inter = jnp.einsum('bhntd,bhndv->bhntv', q_t, states)
    return (o_intra + o_inter).reshape(B, H, S, dv)


def _layer(x, c, norm_g, w_ada, b_ada, w_in, alpha_fw_w, alpha_fw_b, alpha_bw_w, alpha_bw_b,
           gla_norm_g, gmlp_ln_g, gmlp_ln_b, gmlp_ws, gmlp_bs, w_br_gla, w_br_gmlp, w_out):
    B, S, _ = x.shape
    mod = jax.nn.silu(c) @ w_ada + b_ada
    shift, scale, gate = jnp.split(mod, 3, axis=-1)
    h = _rmsnorm(x, norm_g) * (1.0 + scale[:, None, :]) + shift[:, None, :]

    p = h @ w_in
    (q, k, v, z_gla, ra_f, ra_b, u, vs, z_gmlp, m_gla, m_gmlp) = _split_cols(p, IN_SPLITS)

    def heads(t, hd):
        return t.reshape(B, S, GLA_HEADS, hd).transpose(0, 2, 1, 3).astype(jnp.float32)

    log_a_f = jnp.maximum(jax.nn.log_sigmoid(ra_f @ alpha_fw_w + alpha_fw_b) / GLA_TAU, LOG_DECAY_FLOOR)
    log_a_b = jnp.maximum(jax.nn.log_sigmoid(ra_b @ alpha_bw_w + alpha_bw_b) / GLA_TAU, LOG_DECAY_FLOOR)
    qh = heads(q, GLA_HK) * (GLA_HK ** -0.5)
    kh = heads(k, GLA_HK)
    vh = heads(v, GLA_HV)
    laf = heads(log_a_f, GLA_HK)
    lab = heads(log_a_b, GLA_HK)
    flip = lambda t: jnp.flip(t, axis=2)
    o_fw = _gla_chunked(qh, kh, vh, laf, strict=False)
    o_bw = flip(_gla_chunked(flip(qh), flip(kh), flip(vh), flip(lab), strict=True))
    o = o_fw + o_bw
    o = o * lax.rsqrt(jnp.mean(o * o, axis=-1, keepdims=True) + EPS) * gla_norm_g[None, :, None, :]
    o = o.transpose(0, 2, 1, 3).reshape(B, S, GLA_DV).astype(x.dtype)
    y_gla = (o * jax.nn.silu(z_gla)) @ w_br_gla

    u = jax.nn.gelu(u, approximate=False)
    vs = _layernorm(jax.nn.gelu(vs, approximate=False), gmlp_ln_g, gmlp_ln_b)
    vs = vs.reshape(B, S // GMLP_CHUNK, GMLP_CHUNK, GMLP_GROUPS, GMLP_GC)
    sg = jnp.einsum('gts,bnsgc->bntgc', gmlp_ws, vs) + gmlp_bs.T[None, None, :, :, None]
    sg = sg.reshape(B, S, GMLP_WIDTH)
    y_gmlp = (u * sg * jax.nn.silu(z_gmlp)) @ w_br_gmlp

    merged = jax.nn.sigmoid(m_gla) * y_gla + jax.nn.sigmoid(m_gmlp) * y_gmlp
    return x + gate[:, None, :] * (merged @ w_out)


def setup_inputs(seed: int = 0) -> dict:
    key = jax.random.key(seed)
    ks = jax.random.split(key, 20)
    D = D_MODEL
    nrm = lambda k, shape, s: jax.random.normal(k, shape, dtype=jnp.float32) * s
    return {
        'x': nrm(ks[0], (BATCH, SEQ, D), 1.0),
        'c': nrm(ks[1], (BATCH, D), 1.0),
        'norm_g': 1.0 + nrm(ks[2], (DEPTH, D), 0.02),
        'w_ada': nrm(ks[3], (DEPTH, D, 3 * D), 0.5 * D ** -0.5),
        'b_ada': nrm(ks[4], (DEPTH, 3 * D), 0.02),
        'w_in': nrm(ks[5], (DEPTH, D, IN_WIDTH), D ** -0.5),
        'alpha_fw_w': nrm(ks[6], (DEPTH, GLA_RANK, GLA_DK), GLA_RANK ** -0.5),
        'alpha_fw_b': nrm(ks[7], (DEPTH, GLA_DK), 0.1),
        'alpha_bw_w': nrm(ks[8], (DEPTH, GLA_RANK, GLA_DK), GLA_RANK ** -0.5),
        'alpha_bw_b': nrm(ks[9], (DEPTH, GLA_DK), 0.1),
        'gla_norm_g': 1.0 + nrm(ks[10], (DEPTH, GLA_HEADS, GLA_HV), 0.02),
        'gmlp_ln_g': 1.0 + nrm(ks[11], (DEPTH, GMLP_WIDTH), 0.02),
        'gmlp_ln_b': nrm(ks[12], (DEPTH, GMLP_WIDTH), 0.02),
        'gmlp_ws': nrm(ks[13], (DEPTH, GMLP_GROUPS, GMLP_CHUNK, GMLP_CHUNK), GMLP_CHUNK ** -0.5),
        'gmlp_bs': 1.0 + nrm(ks[14], (DEPTH, GMLP_GROUPS, GMLP_CHUNK), 0.02),
        'w_br_gla': nrm(ks[15], (DEPTH, GLA_DV, D), GLA_DV ** -0.5),
        'w_br_gmlp': nrm(ks[16], (DEPTH, GMLP_WIDTH, D), GMLP_WIDTH ** -0.5),
        'w_out': nrm(ks[17], (DEPTH, D, D), D ** -0.5),
        'final_g': 1.0 + nrm(ks[18], (D,), 0.02),
    }


def reference(x, c, norm_g, w_ada, b_ada, w_in, alpha_fw_w, alpha_fw_b, alpha_bw_w, alpha_bw_b,
              gla_norm_g, gmlp_ln_g, gmlp_ln_b, gmlp_ws, gmlp_bs, w_br_gla, w_br_gmlp, w_out, final_g):
    h = x
    for l in range(DEPTH):
        h = _layer(h, c, norm_g[l], w_ada[l], b_ada[l], w_in[l],
                   alpha_fw_w[l], alpha_fw_b[l], alpha_bw_w[l], alpha_bw_b[l],
                   gla_norm_g[l], gmlp_ln_g[l], gmlp_ln_b[l], gmlp_ws[l], gmlp_bs[l],
                   w_br_gla[l], w_br_gmlp[l], w_out[l])
    return _rmsnorm(h, final_g)
```

```python
import functools

import jax
import jax.numpy as jnp
from jax import lax
from jax.experimental import pallas as pl
from jax.experimental.pallas import tpu as pltpu

F32 = jnp.float32
BF16 = jnp.bfloat16

D_MODEL = 1024
GLA_HEADS = 4
GLA_HK = 128
GLA_HV = 256
GLA_DK = GLA_HEADS * GLA_HK
GLA_RANK = 16
GLA_TAU = 16.0
GLA_CHUNK = 64
LOG_DECAY_FLOOR = -1.25
GMLP_GROUPS = 8
GMLP_GC = 128
GMLP_CHUNK = 128
EPS = 1e-6
SQRT_HALF = 0.7071067811865476

RA_PAD = 128
VMEM_LIMIT_BYTES = 56 * 1024 * 1024

PROJ_TM = 512
OUT_TM = 512
ROW_BLK = 256
SCAN_UNROLL = 4


def _silu(z):
    return z * jax.nn.sigmoid(z)


def _gelu_exact(z):
    return 0.5 * z * (1.0 + lax.erf(z * SQRT_HALF))


def _log_sigmoid(z):
    return jnp.minimum(z, 0.0) - jnp.log1p(jnp.exp(-jnp.abs(z)))


def _dot(a, b):
    return jnp.dot(a, b, preferred_element_type=F32)


def _dot_nt(a, b):
    return lax.dot_general(a, b, (((1,), (1,)), ((), ())), preferred_element_type=F32)


def _dot_tn(a, b):
    return lax.dot_general(a, b, (((0,), (0,)), ((), ())), preferred_element_type=F32)


def _mod_kernel(c_ref, w_ref, b_ref, o_ref):
    a = _silu(c_ref[...]).astype(BF16)
    o_ref[...] = _dot(a, w_ref[...].astype(BF16)) + b_ref[...]


def _adaln_mod(c, w_ada, b_ada):
    B, D = c.shape
    n_out = w_ada.shape[1]
    return pl.pallas_call(
        _mod_kernel,
        grid=(n_out // D,),
        in_specs=[
            pl.BlockSpec((B, D), lambda j: (0, 0)),
            pl.BlockSpec((D, D), lambda j: (0, j)),
            pl.BlockSpec((1, D), lambda j: (0, j)),
        ],
        out_specs=pl.BlockSpec((B, D), lambda j: (0, j)),
        out_shape=jax.ShapeDtypeStruct((B, n_out), F32),
        compiler_params=pltpu.CompilerParams(dimension_semantics=("arbitrary",)),
        name="adaln_mod",
    )(c, w_ada, b_ada.reshape(1, n_out))


def _proj_kernel(x_ref, mod_ref, ng_ref, wm_ref, wra_ref, lng_ref, lnb_ref, ws_ref, bsm_ref, wbg_ref,
                 qk_ref, v_ref, gz_ref, ra_ref, sgl_ref, mg_ref, sg_scr):
    D = D_MODEL
    tm = x_ref.shape[0]
    x = x_ref[...]
    r = lax.rsqrt(jnp.mean(x * x, axis=-1, keepdims=True) + EPS)
    shift = mod_ref[0, 0:1, :]
    scale = mod_ref[0, 1:2, :]
    h = ((x * r) * ng_ref[...] * (1.0 + scale) + shift).astype(BF16)

    def proj(j):
        return _dot(h, wm_ref[:, j * D:(j + 1) * D])

    qk_ref[...] = proj(0).astype(BF16)
    v_ref[...] = proj(1).astype(BF16)
    gz_ref[...] = _silu(proj(2)).astype(BF16)
    ra_ref[...] = _dot(h, wra_ref[...])

    vs = _gelu_exact(proj(4))
    mu = jnp.mean(vs, axis=-1, keepdims=True)
    vc = vs - mu
    var = jnp.mean(vc * vc, axis=-1, keepdims=True)
    vsn = ((vc * lax.rsqrt(var + EPS)) * lng_ref[...] + lnb_ref[...]).astype(BF16)
    for n in range(tm // GMLP_CHUNK):
        rs = slice(n * GMLP_CHUNK, (n + 1) * GMLP_CHUNK)
        for g in range(GMLP_GROUPS):
            cs = slice(g * GMLP_GC, (g + 1) * GMLP_GC)
            sg_scr[rs, cs] = _dot(ws_ref[g], vsn[rs, cs]) + bsm_ref[:, cs]
    u = _gelu_exact(proj(3))
    g2 = ((u * sg_scr[...]) * _silu(proj(5))).astype(BF16)
    y_gmlp = _dot(g2, wbg_ref[...])

    sgl_ref[...] = jax.nn.sigmoid(proj(6)).astype(BF16)
    mg_ref[...] = (jax.nn.sigmoid(proj(7)) * y_gmlp).astype(BF16)


def _projection(x2, mod3, norm_g, w_main, w_ra, ln_g, ln_b, ws, bsm, w_br_gmlp, seq):
    T, D = x2.shape
    tm = PROJ_TM
    assert T % tm == 0 and seq % tm == 0 and tm % GMLP_CHUNK == 0
    resident = functools.partial(pl.BlockSpec, pipeline_mode=pl.Buffered(1))
    tok = lambda w: pl.BlockSpec((tm, w), lambda i: (i, 0))
    tiles_per_seq = seq // tm
    out_bf = jax.ShapeDtypeStruct((T, D), BF16)
    return pl.pallas_call(
        _proj_kernel,
        grid=(T // tm,),
        in_specs=[
            tok(D),
            pl.BlockSpec((1, 3, D), lambda i: (i // tiles_per_seq, 0, 0)),
            resident((1, D), lambda i: (0, 0)),
            resident(w_main.shape, lambda i: (0, 0)),
            resident(w_ra.shape, lambda i: (0, 0)),
            resident((1, D), lambda i: (0, 0)),
            resident((1, D), lambda i: (0, 0)),
            resident(ws.shape, lambda i: (0, 0, 0)),
            resident(bsm.shape, lambda i: (0, 0)),
            resident(w_br_gmlp.shape, lambda i: (0, 0)),
        ],
        out_specs=[tok(D), tok(D), tok(D), tok(RA_PAD), tok(D), tok(D)],
        out_shape=[out_bf, out_bf, out_bf, jax.ShapeDtypeStruct((T, RA_PAD), F32), out_bf, out_bf],
        scratch_shapes=[pltpu.VMEM((tm, D), F32)],
        compiler_params=pltpu.CompilerParams(dimension_semantics=("arbitrary",),
                                             vmem_limit_bytes=VMEM_LIMIT_BYTES),
        name="projection_gmlp",
    )(x2, mod3, norm_g, w_main, w_ra, ln_g, ln_b, ws, bsm, w_br_gmlp)


def _gla_kernel(q_ref, k_ref, v_ref, ra_ref, wa_ref, ba_ref, gz_ref, gng_ref, o_ref,
                la_scr, oacc_scr, qt_scr, kv_scr, d_scr):
    S = q_ref.shape[1]
    C = GLA_CHUNK
    n_chunks = S // C
    hk = GLA_HK
    q_scale = GLA_HK ** -0.5

    def decay_rows(i, carry):
        rows = pl.ds(pl.multiple_of(i * ROW_BLK, ROW_BLK), ROW_BLK)
        pre = _dot(ra_ref[0, rows, :].astype(BF16), wa_ref[0]) + ba_ref[0]
        la_scr[rows, :] = jnp.maximum(_log_sigmoid(pre) * (1.0 / GLA_TAU), LOG_DECAY_FLOOR)
        return carry

    lax.fori_loop(0, S // ROW_BLK, decay_rows, 0)

    ti = lax.broadcasted_iota(jnp.int32, (C, C), 0)
    si = lax.broadcasted_iota(jnp.int32, (C, C), 1)

    def direction(forward):
        lane0 = 0 if forward else hk
        keep = (si <= ti) if forward else (si >= ti)
        attend = (si <= ti) if forward else (si > ti)
        tri = jnp.where(keep, 1.0, 0.0).astype(BF16)

        def local(n):
            rows = pl.ds(pl.multiple_of(n * C, C), C)
            la = la_scr[rows, lane0:lane0 + hk]
            hi = la.astype(BF16)
            r1 = la - hi.astype(F32)
            mid = r1.astype(BF16)
            lo = (r1 - mid.astype(F32)).astype(BF16)
            b = _dot(tri, hi) + _dot(tri, mid) + _dot(tri, lo)
            b_last = b[C - 1:C, :] if forward else b[0:1, :]
            q = q_ref[0, rows, :].astype(F32)
            k = k_ref[0, rows, :].astype(F32)
            qt = ((q * q_scale) * jnp.exp(b)).astype(BF16)
            kt = (k * jnp.exp(-b)).astype(BF16)
            ke = (k * jnp.exp(b_last - b)).astype(BF16)
            attn = jnp.where(attend, _dot_nt(qt, kt), 0.0).astype(BF16)
            vc = v_ref[0, rows, :]
            o_intra = _dot(attn, vc)
            if forward:
                oacc_scr[rows, :] = o_intra
            else:
                oacc_scr[rows, :] += o_intra
            kv_scr[n] = _dot_tn(vc, ke)
            qt_scr[rows, :] = qt
            d_scr[pl.ds(n, 1), :] = jnp.exp(b_last)

        def local_group(i, carry):
            for u in range(SCAN_UNROLL):
                local(i * SCAN_UNROLL + u)
            return carry

        lax.fori_loop(0, n_chunks // SCAN_UNROLL, local_group, 0)

        def carry_step(i, state):
            n = i if forward else n_chunks - 1 - i
            rows = pl.ds(pl.multiple_of(n * C, C), C)
            oacc_scr[rows, :] += _dot_nt(qt_scr[rows, :], state.astype(BF16))
            return state * d_scr[pl.ds(n, 1), :] + kv_scr[n]

        lax.fori_loop(0, n_chunks, carry_step, jnp.zeros((GLA_HV, hk), F32), unroll=SCAN_UNROLL)

    direction(True)
    direction(False)

    def finish_rows(i, carry):
        rows = pl.ds(pl.multiple_of(i * ROW_BLK, ROW_BLK), ROW_BLK)
        o = oacc_scr[rows, :]
        on = o * lax.rsqrt(jnp.mean(o * o, axis=-1, keepdims=True) + EPS) * gng_ref[0]
        o_ref[0, rows, :] = (on * gz_ref[0, rows, :].astype(F32)).astype(BF16)
        return carry

    lax.fori_loop(0, S // ROW_BLK, finish_rows, 0)


def _gla(qk3, v3, ra3, wa, ba, gz3, gng):
    B, S, D = v3.shape
    H, hk, hv = GLA_HEADS, GLA_HK, GLA_HV
    assert S % ROW_BLK == 0 and (S // GLA_CHUNK) % SCAN_UNROLL == 0
    return pl.pallas_call(
        _gla_kernel,
        grid=(B, H),
        in_specs=[
            pl.BlockSpec((1, S, hk), lambda b, h: (b, 0, h)),
            pl.BlockSpec((1, S, hk), lambda b, h: (b, 0, H + h)),
            pl.BlockSpec((1, S, hv), lambda b, h: (b, 0, h)),
            pl.BlockSpec((1, S, RA_PAD), lambda b, h: (b, 0, 0)),
            pl.BlockSpec((1, RA_PAD, 2 * hk), lambda b, h: (h, 0, 0)),
            pl.BlockSpec((1, 1, 2 * hk), lambda b, h: (h, 0, 0)),
            pl.BlockSpec((1, S, hv), lambda b, h: (b, 0, h)),
            pl.BlockSpec((1, 1, hv), lambda b, h: (h, 0, 0)),
        ],
        out_specs=pl.BlockSpec((1, S, hv), lambda b, h: (b, 0, h)),
        out_shape=jax.ShapeDtypeStruct((B, S, D), BF16),
        scratch_shapes=[
            pltpu.VMEM((S, 2 * hk), F32),
            pltpu.VMEM((S, hv), F32),
            pltpu.VMEM((S, hk), BF16),
            pltpu.VMEM((S // GLA_CHUNK, hv, hk), F32),
            pltpu.VMEM((S // GLA_CHUNK, hk), F32),
        ],
        compiler_params=pltpu.CompilerParams(dimension_semantics=("arbitrary", "arbitrary"),
                                             vmem_limit_bytes=VMEM_LIMIT_BYTES),
        name="gla_bidir",
    )(qk3, qk3, v3, ra3, wa, ba, gz3, gng)


def _out_kernel(x_ref, g1_ref, sgl_ref, mg_ref, mod_ref, wbr_ref, wout_ref, fg_ref, o_ref):
    y_gla = _dot(g1_ref[...], wbr_ref[...])
    merged = (sgl_ref[...].astype(F32) * y_gla + mg_ref[...].astype(F32)).astype(BF16)
    gate = mod_ref[0, 2:3, :]
    r = x_ref[...] + gate * _dot(merged, wout_ref[...])
    o_ref[...] = (r * lax.rsqrt(jnp.mean(r * r, axis=-1, keepdims=True) + EPS)) * fg_ref[...]


def _merge_out(x2, g1, sgl, mg, mod3, w_br_gla, w_out, final_g, seq):
    T, D = x2.shape
    tm = OUT_TM
    assert T % tm == 0 and seq % tm == 0
    resident = functools.partial(pl.BlockSpec, pipeline_mode=pl.Buffered(1))
    tok = pl.BlockSpec((tm, D), lambda i: (i, 0))
    tiles_per_seq = seq // tm
    return pl.pallas_call(
        _out_kernel,
        grid=(T // tm,),
        in_specs=[
            tok, tok, tok, tok,
            pl.BlockSpec((1, 3, D), lambda i: (i // tiles_per_seq, 0, 0)),
            resident((D, D), lambda i: (0, 0)),
            resident((D, D), lambda i: (0, 0)),
            resident((1, D), lambda i: (0, 0)),
        ],
        out_specs=tok,
        out_shape=jax.ShapeDtypeStruct((T, D), F32),
        compiler_params=pltpu.CompilerParams(dimension_semantics=("arbitrary",),
                                             vmem_limit_bytes=VMEM_LIMIT_BYTES),
        name="merge_out",
    )(x2, g1, sgl, mg, mod3, w_br_gla, w_out, final_g)


def _layer(x, c, norm_g, w_ada, b_ada, w_in, alpha_fw_w, alpha_fw_b, alpha_bw_w, alpha_bw_b,
           gla_norm_g, gmlp_ln_g, gmlp_ln_b, gmlp_ws, gmlp_bs, w_br_gla, w_br_gmlp, w_out, out_g):
    B, S, D = x.shape
    H, hk, hv, R = GLA_HEADS, GLA_HK, GLA_HV, GLA_RANK
    T = B * S

    o_ra = 2 * GLA_DK + 2 * D
    w_main = jnp.concatenate([w_in[:, :o_ra], w_in[:, o_ra + 2 * R:]], axis=1).astype(BF16)
    w_ra = jnp.pad(w_in[:, o_ra:o_ra + 2 * R], ((0, 0), (0, RA_PAD - 2 * R))).astype(BF16)
    afw = alpha_fw_w.reshape(R, H, hk).transpose(1, 0, 2)
    abw = alpha_bw_w.reshape(R, H, hk).transpose(1, 0, 2)
    zeros = jnp.zeros_like(afw)
    wa = jnp.concatenate([jnp.concatenate([afw, zeros], axis=2), jnp.concatenate([zeros, abw], axis=2)], axis=1)
    wa = jnp.pad(wa, ((0, 0), (0, RA_PAD - 2 * R), (0, 0))).astype(BF16)
    ba = jnp.concatenate([alpha_fw_b.reshape(H, 1, hk), alpha_bw_b.reshape(H, 1, hk)], axis=2)
    bsm = jnp.repeat(gmlp_bs.T, GMLP_GC, axis=1)

    mod3 = _adaln_mod(c, w_ada, b_ada).reshape(B, 3, D)
    x2 = x.reshape(T, D)
    qk, v, gz, ra, sgl, mg = _projection(
        x2, mod3, norm_g.reshape(1, D), w_main, w_ra, gmlp_ln_g.reshape(1, D), gmlp_ln_b.reshape(1, D),
        gmlp_ws.astype(BF16), bsm, w_br_gmlp.astype(BF16), S)
    g1 = _gla(qk.reshape(B, S, D), v.reshape(B, S, D), ra.reshape(B, S, RA_PAD), wa, ba,
              gz.reshape(B, S, D), gla_norm_g.reshape(H, 1, hv))
    out = _merge_out(x2, g1.reshape(T, D), sgl, mg, mod3, w_br_gla.astype(BF16), w_out.astype(BF16),
                     out_g.reshape(1, D), S)
    return out.reshape(B, S, D)


def kernel(x, c, norm_g, w_ada, b_ada, w_in, alpha_fw_w, alpha_fw_b, alpha_bw_w, alpha_bw_b, gla_norm_g,
           gmlp_ln_g, gmlp_ln_b, gmlp_ws, gmlp_bs, w_br_gla, w_br_gmlp, w_out, final_g):
    depth = norm_g.shape[0]
    assert depth == 1, "the final rmsnorm is fused into the layer's output kernel"
    return _layer(x, c, norm_g[0], w_ada[0], b_ada[0], w_in[0], alpha_fw_w[0], alpha_fw_b[0], alpha_bw_w[0],
                  alpha_bw_b[0], gla_norm_g[0], gmlp_ln_g[0], gmlp_ln_b[0], gmlp_ws[0], gmlp_bs[0],
                  w_br_gla[0], w_br_gmlp[0], w_out[0], final_g)
```

```python
import functools

import jax
import jax.numpy as jnp
from jax import lax
from jax.experimental import pallas as pl
from jax.experimental.pallas import tpu as pltpu

F32 = jnp.float32
BF16 = jnp.bfloat16

D_MODEL = 1024
GLA_HEADS = 4
GLA_HK = 128
GLA_HV = 256
GLA_DK = GLA_HEADS * GLA_HK
GLA_RANK = 16
GLA_TAU = 16.0
GLA_CHUNK = 64
LOG_DECAY_FLOOR = -1.25
GMLP_GROUPS = 8
GMLP_GC = 128
GMLP_CHUNK = 128
EPS = 1e-6
SQRT_HALF = 0.7071067811865476

RA_PAD = 128
VMEM_LIMIT_BYTES = 56 * 1024 * 1024

PROJ_TM = 512
OUT_TM = 512
GLA_BLK = 256
CHUNKS_PER_BLK = GLA_BLK // GLA_CHUNK
CHUNK_SHIFT = GLA_CHUNK.bit_length() - 1
N_GLA_OPERANDS = 6


def _silu(z):
    return z * jax.nn.sigmoid(z)


def _gelu_exact(z):
    return 0.5 * z * (1.0 + lax.erf(z * SQRT_HALF))


def _log_sigmoid(z):
    return jnp.minimum(z, 0.0) - jnp.log1p(jnp.exp(-jnp.abs(z)))


def _dot(a, b):
    return jnp.dot(a, b, preferred_element_type=F32)


def _dot_nt(a, b):
    return lax.dot_general(a, b, (((1,), (1,)), ((), ())), preferred_element_type=F32)


def _dot_tn(a, b):
    return lax.dot_general(a, b, (((0,), (0,)), ((), ())), preferred_element_type=F32)


def _block_iotas():
    ti = lax.broadcasted_iota(jnp.int32, (GLA_BLK, GLA_BLK), 0)
    si = lax.broadcasted_iota(jnp.int32, (GLA_BLK, GLA_BLK), 1)
    return ti, si


def _mod_kernel(c_ref, w_ref, b_ref, o_ref):
    a = _silu(c_ref[...]).astype(BF16)
    o_ref[...] = _dot(a, w_ref[...].astype(BF16)) + b_ref[...]


def _adaln_mod(c, w_ada, b_ada):
    B, D = c.shape
    n_out = w_ada.shape[1]
    return pl.pallas_call(
        _mod_kernel,
        grid=(n_out // D,),
        in_specs=[
            pl.BlockSpec((B, D), lambda j: (0, 0)),
            pl.BlockSpec((D, D), lambda j: (0, j)),
            pl.BlockSpec((1, D), lambda j: (0, j)),
        ],
        out_specs=pl.BlockSpec((B, D), lambda j: (0, j)),
        out_shape=jax.ShapeDtypeStruct((B, n_out), F32),
        compiler_params=pltpu.CompilerParams(dimension_semantics=("arbitrary",)),
        name="adaln_mod",
    )(c, w_ada, b_ada.reshape(1, n_out))


def _proj_kernel(x_ref, mod_ref, ng_ref, wm_ref, wra_ref, wa_ref, ba_ref, lng_ref, lnb_ref, ws_ref, bsm_ref,
                 wbg_ref, gl_ref, d_ref, v_ref, gz_ref, sgl_ref, mg_ref, sg_scr):
    D = D_MODEL
    DK = GLA_DK
    C = GLA_CHUNK
    tm = x_ref.shape[0]
    x = x_ref[...]
    r = lax.rsqrt(jnp.mean(x * x, axis=-1, keepdims=True) + EPS)
    shift = mod_ref[0, 0:1, :]
    scale = mod_ref[0, 1:2, :]
    h = ((x * r) * ng_ref[...] * (1.0 + scale) + shift).astype(BF16)

    def proj(j):
        return _dot(h, wm_ref[:, j * D:(j + 1) * D])

    v_ref[...] = proj(1).astype(BF16)
    gz_ref[...] = _silu(proj(2)).astype(BF16)

    qk = proj(0)
    ra = _dot(h, wra_ref[...]).astype(BF16)
    pre = _dot(ra, wa_ref[...]) + ba_ref[...]
    la = jnp.maximum(_log_sigmoid(pre) * (1.0 / GLA_TAU), LOG_DECAY_FLOOR)
    hi = la.astype(BF16)
    r1 = la - hi.astype(F32)
    mid = r1.astype(BF16)
    lo = (r1 - mid.astype(F32)).astype(BF16)
    ti, si = _block_iotas()
    same_chunk = (ti >> CHUNK_SHIFT) == (si >> CHUNK_SHIFT)
    tri_f = jnp.where(same_chunk & (si <= ti), 1.0, 0.0).astype(BF16)
    tri_b = jnp.where(same_chunk & (si >= ti), 1.0, 0.0).astype(BF16)
    q_scale = GLA_HK ** -0.5
    for blk in range(tm // GLA_BLK):
        rs = slice(blk * GLA_BLK, (blk + 1) * GLA_BLK)
        b_f = _dot(tri_f, hi[rs, :DK]) + _dot(tri_f, mid[rs, :DK]) + _dot(tri_f, lo[rs, :DK])
        b_b = _dot(tri_b, hi[rs, DK:]) + _dot(tri_b, mid[rs, DK:]) + _dot(tri_b, lo[rs, DK:])
        for c in range(CHUNKS_PER_BLK):
            cs = slice(c * C, (c + 1) * C)
            gr = slice(blk * GLA_BLK + c * C, blk * GLA_BLK + (c + 1) * C)
            n = blk * CHUNKS_PER_BLK + c
            q = qk[gr, :DK] * q_scale
            k = qk[gr, DK:]
            bf = b_f[cs]
            bb = b_b[cs]
            bf_last = bf[C - 1:C]
            bb_last = bb[0:1]
            gl_ref[gr, 0 * DK:1 * DK] = (q * jnp.exp(bf)).astype(BF16)
            gl_ref[gr, 1 * DK:2 * DK] = (k * jnp.exp(-bf)).astype(BF16)
            gl_ref[gr, 2 * DK:3 * DK] = (k * jnp.exp(bf_last - bf)).astype(BF16)
            gl_ref[gr, 3 * DK:4 * DK] = (q * jnp.exp(bb)).astype(BF16)
            gl_ref[gr, 4 * DK:5 * DK] = (k * jnp.exp(-bb)).astype(BF16)
            gl_ref[gr, 5 * DK:6 * DK] = (k * jnp.exp(bb_last - bb)).astype(BF16)
            d_ref[n:n + 1, :DK] = jnp.exp(bf_last)
            d_ref[n:n + 1, DK:] = jnp.exp(bb_last)

    vs = _gelu_exact(proj(4))
    mu = jnp.mean(vs, axis=-1, keepdims=True)
    vc = vs - mu
    var = jnp.mean(vc * vc, axis=-1, keepdims=True)
    vsn = ((vc * lax.rsqrt(var + EPS)) * lng_ref[...] + lnb_ref[...]).astype(BF16)
    for n in range(tm // GMLP_CHUNK):
        rs = slice(n * GMLP_CHUNK, (n + 1) * GMLP_CHUNK)
        for g in range(GMLP_GROUPS):
            cs = slice(g * GMLP_GC, (g + 1) * GMLP_GC)
            sg_scr[rs, cs] = _dot(ws_ref[g], vsn[rs, cs]) + bsm_ref[:, cs]
    u = _gelu_exact(proj(3))
    g2 = ((u * sg_scr[...]) * _silu(proj(5))).astype(BF16)
    y_gmlp = _dot(g2, wbg_ref[...])

    sgl_ref[...] = jax.nn.sigmoid(proj(6)).astype(BF16)
    mg_ref[...] = (jax.nn.sigmoid(proj(7)) * y_gmlp).astype(BF16)


def _projection(x2, mod3, norm_g, w_main, w_ra, wa, ba, ln_g, ln_b, ws, bsm, w_br_gmlp, seq):
    T, D = x2.shape
    tm = PROJ_TM
    assert T % tm == 0 and seq % tm == 0 and tm % GMLP_CHUNK == 0 and tm % GLA_BLK == 0
    resident = functools.partial(pl.BlockSpec, pipeline_mode=pl.Buffered(1))
    tok = lambda w: pl.BlockSpec((tm, w), lambda i: (i, 0))
    tiles_per_seq = seq // tm
    out_bf = jax.ShapeDtypeStruct((T, D), BF16)
    return pl.pallas_call(
        _proj_kernel,
        grid=(T // tm,),
        in_specs=[
            tok(D),
            pl.BlockSpec((1, 3, D), lambda i: (i // tiles_per_seq, 0, 0)),
            resident((1, D), lambda i: (0, 0)),
            resident(w_main.shape, lambda i: (0, 0)),
            resident(w_ra.shape, lambda i: (0, 0)),
            resident(wa.shape, lambda i: (0, 0)),
            resident(ba.shape, lambda i: (0, 0)),
            resident((1, D), lambda i: (0, 0)),
            resident((1, D), lambda i: (0, 0)),
            resident(ws.shape, lambda i: (0, 0, 0)),
            resident(bsm.shape, lambda i: (0, 0)),
            resident(w_br_gmlp.shape, lambda i: (0, 0)),
        ],
        out_specs=[tok(N_GLA_OPERANDS * GLA_DK), pl.BlockSpec((tm // GLA_CHUNK, 2 * GLA_DK), lambda i: (i, 0)),
                   tok(D), tok(D), tok(D), tok(D)],
        out_shape=[jax.ShapeDtypeStruct((T, N_GLA_OPERANDS * GLA_DK), BF16),
                   jax.ShapeDtypeStruct((T // GLA_CHUNK, 2 * GLA_DK), F32),
                   out_bf, out_bf, out_bf, out_bf],
        scratch_shapes=[pltpu.VMEM((tm, D), F32)],
        compiler_params=pltpu.CompilerParams(dimension_semantics=("arbitrary",),
                                             vmem_limit_bytes=VMEM_LIMIT_BYTES),
        name="projection_gmlp",
    )(x2, mod3, norm_g, w_main, w_ra, wa, ba, ln_g, ln_b, ws, bsm, w_br_gmlp)


def _gla_kernel(qtf_ref, ktf_ref, kef_ref, qtb_ref, ktb_ref, keb_ref, v_ref, df_ref, db_ref, gz_ref, gng_ref,
                o_ref, oacc_scr, kvf_scr, kvb_scr):
    S = v_ref.shape[1]
    C = GLA_CHUNK
    hk = GLA_HK
    n_blk = S // GLA_BLK
    ti, si = _block_iotas()
    chunk_lo = (ti >> CHUNK_SHIFT) << CHUNK_SHIFT
    chunk_hi = chunk_lo + (C - 1)
    row_chunk = lax.broadcasted_iota(jnp.int32, (GLA_BLK, hk), 0) >> CHUNK_SHIFT
    chunk_sel = [jnp.where(row_chunk == c, 1.0, 0.0).astype(BF16) for c in range(CHUNKS_PER_BLK)]

    def local(j, carry):
        rows = pl.ds(pl.multiple_of(j * GLA_BLK, GLA_BLK), GLA_BLK)
        vb = v_ref[0, rows, :]

        def one_direction(qt_ref, kt_ref, ke_ref, forward):
            a = _dot_nt(qt_ref[0, rows, :], kt_ref[0, rows, :])
            if forward:
                a = jnp.where(si <= ti, jnp.where(si >= chunk_lo, a, 0.0), 0.0)
            else:
                a = jnp.where(si > ti, jnp.where(si <= chunk_hi, a, 0.0), 0.0)
            ke = ke_ref[0, rows, :]
            ke_bd = jnp.concatenate([ke * chunk_sel[c] for c in range(CHUNKS_PER_BLK)], axis=1)
            return _dot(a.astype(BF16), vb), ke_bd

        o_f, ke_f = one_direction(qtf_ref, ktf_ref, kef_ref, True)
        o_b, ke_b = one_direction(qtb_ref, ktb_ref, keb_ref, False)
        oacc_scr[rows, :] = o_f + o_b
        kv = _dot_tn(vb, jnp.concatenate([ke_f, ke_b], axis=1))
        kvf_scr[j] = kv[:, :CHUNKS_PER_BLK * hk]
        kvb_scr[j] = kv[:, CHUNKS_PER_BLK * hk:]
        return carry

    lax.fori_loop(0, n_blk, local, 0)

    def scan(i, states):
        sf, sb = states
        jf = i
        jb = n_blk - 1 - i
        for c in range(CHUNKS_PER_BLK):
            cb = CHUNKS_PER_BLK - 1 - c
            rf = pl.ds(pl.multiple_of(jf * GLA_BLK + c * C, C), C)
            rb = pl.ds(pl.multiple_of(jb * GLA_BLK + cb * C, C), C)
            oacc_scr[rf, :] += _dot_nt(qtf_ref[0, rf, :], sf.astype(BF16))
            oacc_scr[rb, :] += _dot_nt(qtb_ref[0, rb, :], sb.astype(BF16))
            sf = sf * df_ref[0, pl.ds(jf * CHUNKS_PER_BLK + c, 1), :] + kvf_scr[jf, :, c * hk:(c + 1) * hk]
            sb = sb * db_ref[0, pl.ds(jb * CHUNKS_PER_BLK + cb, 1), :] + kvb_scr[jb, :, cb * hk:(cb + 1) * hk]
        return sf, sb

    zero = jnp.zeros((GLA_HV, hk), F32)
    lax.fori_loop(0, n_blk, scan, (zero, zero))

    def finish_rows(i, carry):
        rows = pl.ds(pl.multiple_of(i * GLA_BLK, GLA_BLK), GLA_BLK)
        o = oacc_scr[rows, :]
        on = o * lax.rsqrt(jnp.mean(o * o, axis=-1, keepdims=True) + EPS) * gng_ref[0]
        o_ref[0, rows, :] = (on * gz_ref[0, rows, :].astype(F32)).astype(BF16)
        return carry

    lax.fori_loop(0, n_blk, finish_rows, 0)


def _gla(gl3, v3, d3, gz3, gng):
    B, S, D = v3.shape
    H, hk, hv = GLA_HEADS, GLA_HK, GLA_HV
    n_chunks = S // GLA_CHUNK
    assert S % GLA_BLK == 0
    operand = lambda g: pl.BlockSpec((1, S, hk), lambda b, h: (b, 0, g * H + h))
    decay = lambda g: pl.BlockSpec((1, n_chunks, hk), lambda b, h: (b, 0, g * H + h))
    head_v = pl.BlockSpec((1, S, hv), lambda b, h: (b, 0, h))
    kv_scratch = pltpu.VMEM((S // GLA_BLK, hv, CHUNKS_PER_BLK * hk), F32)
    return pl.pallas_call(
        _gla_kernel,
        grid=(B, H),
        in_specs=[operand(g) for g in range(N_GLA_OPERANDS)] + [
            head_v, decay(0), decay(1), head_v,
            pl.BlockSpec((1, 1, hv), lambda b, h: (h, 0, 0)),
        ],
        out_specs=head_v,
        out_shape=jax.ShapeDtypeStruct((B, S, D), BF16),
        scratch_shapes=[pltpu.VMEM((S, hv), F32), kv_scratch, kv_scratch],
        compiler_params=pltpu.CompilerParams(dimension_semantics=("arbitrary", "arbitrary"),
                                             vmem_limit_bytes=VMEM_LIMIT_BYTES),
        name="gla_bidir",
    )(*([gl3] * N_GLA_OPERANDS), v3, d3, d3, gz3, gng)


def _out_kernel(x_ref, g1_ref, sgl_ref, mg_ref, mod_ref, wbr_ref, wout_ref, fg_ref, o_ref):
    y_gla = _dot(g1_ref[...], wbr_ref[...])
    merged = (sgl_ref[...].astype(F32) * y_gla + mg_ref[...].astype(F32)).astype(BF16)
    gate = mod_ref[0, 2:3, :]
    r = x_ref[...] + gate * _dot(merged, wout_ref[...])
    o_ref[...] = (r * lax.rsqrt(jnp.mean(r * r, axis=-1, keepdims=True) + EPS)) * fg_ref[...]


def _merge_out(x2, g1, sgl, mg, mod3, w_br_gla, w_out, final_g, seq):
    T, D = x2.shape
    tm = OUT_TM
    assert T % tm == 0 and seq % tm == 0
    resident = functools.partial(pl.BlockSpec, pipeline_mode=pl.Buffered(1))
    tok = pl.BlockSpec((tm, D), lambda i: (i, 0))
    tiles_per_seq = seq // tm
    return pl.pallas_call(
        _out_kernel,
        grid=(T // tm,),
        in_specs=[
            tok, tok, tok, tok,
            pl.BlockSpec((1, 3, D), lambda i: (i // tiles_per_seq, 0, 0)),
            resident((D, D), lambda i: (0, 0)),
            resident((D, D), lambda i: (0, 0)),
            resident((1, D), lambda i: (0, 0)),
        ],
        out_specs=tok,
        out_shape=jax.ShapeDtypeStruct((T, D), F32),
        compiler_params=pltpu.CompilerParams(dimension_semantics=("arbitrary",),
                                             vmem_limit_bytes=VMEM_LIMIT_BYTES),
        name="merge_out",
    )(x2, g1, sgl, mg, mod3, w_br_gla, w_out, final_g)


def _layer(x, c, norm_g, w_ada, b_ada, w_in, alpha_fw_w, alpha_fw_b, alpha_bw_w, alpha_bw_b,
           gla_norm_g, gmlp_ln_g, gmlp_ln_b, gmlp_ws, gmlp_bs, w_br_gla, w_br_gmlp, w_out, out_g):
    B, S, D = x.shape
    H, hv, R, DK = GLA_HEADS, GLA_HV, GLA_RANK, GLA_DK
    T = B * S

    o_ra = 2 * DK + 2 * D
    w_main = jnp.concatenate([w_in[:, :o_ra], w_in[:, o_ra + 2 * R:]], axis=1).astype(BF16)
    w_ra = jnp.pad(w_in[:, o_ra:o_ra + 2 * R], ((0, 0), (0, RA_PAD - 2 * R))).astype(BF16)
    zeros = jnp.zeros_like(alpha_fw_w)
    wa = jnp.concatenate([jnp.concatenate([alpha_fw_w, zeros], axis=1),
                          jnp.concatenate([zeros, alpha_bw_w], axis=1)], axis=0)
    wa = jnp.pad(wa, ((0, RA_PAD - 2 * R), (0, 0))).astype(BF16)
    ba = jnp.concatenate([alpha_fw_b, alpha_bw_b]).reshape(1, 2 * DK)
    bsm = jnp.repeat(gmlp_bs.T, GMLP_GC, axis=1)

    mod3 = _adaln_mod(c, w_ada, b_ada).reshape(B, 3, D)
    x2 = x.reshape(T, D)
    gl, d, v, gz, sgl, mg = _projection(
        x2, mod3, norm_g.reshape(1, D), w_main, w_ra, wa, ba, gmlp_ln_g.reshape(1, D), gmlp_ln_b.reshape(1, D),
        gmlp_ws.astype(BF16), bsm, w_br_gmlp.astype(BF16), S)
    g1 = _gla(gl.reshape(B, S, N_GLA_OPERANDS * DK), v.reshape(B, S, D), d.reshape(B, S // GLA_CHUNK, 2 * DK),
              gz.reshape(B, S, D), gla_norm_g.reshape(H, 1, hv))
    out = _merge_out(x2, g1.reshape(T, D), sgl, mg, mod3, w_br_gla.astype(BF16), w_out.astype(BF16),
                     out_g.reshape(1, D), S)
    return out.reshape(B, S, D)


def kernel(x, c, norm_g, w_ada, b_ada, w_in, alpha_fw_w, alpha_fw_b, alpha_bw_w, alpha_bw_b, gla_norm_g,
           gmlp_ln_g, gmlp_ln_b, gmlp_ws, gmlp_bs, w_br_gla, w_br_gmlp, w_out, final_g):
    depth = norm_g.shape[0]
    assert depth == 1, "the final rmsnorm is fused into the layer's output kernel"
    return _layer(x, c, norm_g[0], w_ada[0], b_ada[0], w_in[0], alpha_fw_w[0], alpha_fw_b[0], alpha_bw_w[0],
                  alpha_bw_b[0], gla_norm_g[0], gmlp_ln_g[0], gmlp_ln_b[0], gmlp_ws[0], gmlp_bs[0],
                  w_br_gla[0], w_br_gmlp[0], w_out[0], final_g)
```

```python
import functools

import jax
import jax.numpy as jnp
from jax import lax
from jax.experimental import pallas as pl
from jax.experimental.pallas import tpu as pltpu

F32 = jnp.float32
BF16 = jnp.bfloat16

D_MODEL = 1024
GLA_HEADS = 4
GLA_HK = 128
GLA_HV = 256
GLA_DK = GLA_HEADS * GLA_HK
GLA_RANK = 16
GLA_TAU = 16.0
GLA_CHUNK = 64
LOG_DECAY_FLOOR = -1.25
GMLP_GROUPS = 8
GMLP_GC = 128
GMLP_CHUNK = 128
EPS = 1e-6
SQRT_HALF = 0.7071067811865476
LOG2_E = 1.4426950408889634

RA_PAD = 128
VMEM_LIMIT_BYTES = 56 * 1024 * 1024

PROJ_TM = 512
OUT_TM = 512
GLA_BLK = 256
CHUNKS_PER_BLK = GLA_BLK // GLA_CHUNK
CHUNK_SHIFT = GLA_CHUNK.bit_length() - 1
N_GLA_OPERANDS = 6
GLA_UNROLL = 4


def _silu(z):
    return z * jax.nn.sigmoid(z)


def _gelu_exact(z):
    return 0.5 * z * (1.0 + lax.erf(z * SQRT_HALF))


def _log_sigmoid(z):
    return jnp.minimum(z, 0.0) - jnp.log1p(jnp.exp(-jnp.abs(z)))


def _dot(a, b):
    return jnp.dot(a, b, preferred_element_type=F32)


def _dot_nt(a, b):
    return lax.dot_general(a, b, (((1,), (1,)), ((), ())), preferred_element_type=F32)


def _dot_tn(a, b):
    return lax.dot_general(a, b, (((0,), (0,)), ((), ())), preferred_element_type=F32)


def _block_iotas():
    ti = lax.broadcasted_iota(jnp.int32, (GLA_BLK, GLA_BLK), 0)
    si = lax.broadcasted_iota(jnp.int32, (GLA_BLK, GLA_BLK), 1)
    return ti, si


def _mod_kernel(c_ref, w_ref, b_ref, o_ref):
    a = _silu(c_ref[...]).astype(BF16)
    o_ref[...] = _dot(a, w_ref[...].astype(BF16)) + b_ref[...]


def _adaln_mod(c, w_ada, b_ada):
    B, D = c.shape
    n_out = w_ada.shape[1]
    return pl.pallas_call(
        _mod_kernel,
        grid=(n_out // D,),
        in_specs=[
            pl.BlockSpec((B, D), lambda j: (0, 0)),
            pl.BlockSpec((D, D), lambda j: (0, j)),
            pl.BlockSpec((1, D), lambda j: (0, j)),
        ],
        out_specs=pl.BlockSpec((B, D), lambda j: (0, j)),
        out_shape=jax.ShapeDtypeStruct((B, n_out), F32),
        compiler_params=pltpu.CompilerParams(dimension_semantics=("arbitrary",)),
        name="adaln_mod",
    )(c, w_ada, b_ada.reshape(1, n_out))


def _proj_kernel(x_ref, mod_ref, ng_ref, wm_ref, wa_ref, ba_ref, lng_ref, lnb_ref, ws_ref, bsm_ref,
                 wbg_ref, gl_ref, d_ref, v_ref, gz_ref, sgl_ref, mg_ref, sg_scr):
    D = D_MODEL
    DK = GLA_DK
    C = GLA_CHUNK
    tm = x_ref.shape[0]
    x = x_ref[...]
    r = lax.rsqrt(jnp.mean(x * x, axis=-1, keepdims=True) + EPS)
    shift = mod_ref[0, 0:1, :]
    scale = mod_ref[0, 1:2, :]
    h = ((x * r) * (ng_ref[...] * (1.0 + scale)) + shift).astype(BF16)

    def proj(j):
        start = j * D + RA_PAD
        return _dot(h, wm_ref[:, start:start + D])


    qk = _dot(h, wm_ref[:, :D])
    half = tm // 2
    ra = jnp.concatenate([_dot(h[:half], wm_ref[:, D:D + RA_PAD]), _dot(h[half:], wm_ref[:, D:D + RA_PAD])],
                         axis=0).astype(BF16)
    pre = _dot(ra, wa_ref[...]) + ba_ref[...]
    la = jnp.maximum(_log_sigmoid(pre) * (1.0 / GLA_TAU), LOG_DECAY_FLOOR) * LOG2_E

    sgl_ref[...] = jax.nn.sigmoid(proj(6)).astype(BF16)

    hi = la.astype(BF16)
    r1 = la - hi.astype(F32)
    mid = r1.astype(BF16)
    lo = (r1 - mid.astype(F32)).astype(BF16)

    gz_ref[...] = _silu(proj(2)).astype(BF16)

    ti, si = _block_iotas()
    same_chunk = (ti >> CHUNK_SHIFT) == (si >> CHUNK_SHIFT)
    tri_f = jnp.where(same_chunk & (si <= ti), 1.0, 0.0).astype(BF16)
    tri_b = jnp.where(same_chunk & (si >= ti), 1.0, 0.0).astype(BF16)
    cum = []
    for blk in range(tm // GLA_BLK):
        rs = slice(blk * GLA_BLK, (blk + 1) * GLA_BLK)
        b_f = _dot(tri_f, hi[rs, :DK]) + _dot(tri_f, mid[rs, :DK]) + _dot(tri_f, lo[rs, :DK])
        b_b = _dot(tri_b, hi[rs, DK:]) + _dot(tri_b, mid[rs, DK:]) + _dot(tri_b, lo[rs, DK:])
        cum.append((b_f, b_b))

    vs = _gelu_exact(proj(4))
    mu = jnp.mean(vs, axis=-1, keepdims=True)
    vc = vs - mu
    var = jnp.mean(vc * vc, axis=-1, keepdims=True)
    vsn = ((vc * lax.rsqrt(var + EPS)) * lng_ref[...] + lnb_ref[...]).astype(BF16)

    u_pre = proj(3)

    q_scale = GLA_HK ** -0.5
    for blk in range(tm // GLA_BLK):
        b_f, b_b = cum[blk]
        for c in range(CHUNKS_PER_BLK):
            cs = slice(c * C, (c + 1) * C)
            gr = slice(blk * GLA_BLK + c * C, blk * GLA_BLK + (c + 1) * C)
            n = blk * CHUNKS_PER_BLK + c
            q = qk[gr, :DK] * q_scale
            k = qk[gr, DK:]
            bf = b_f[cs]
            bb = b_b[cs]
            bf_last = bf[C - 1:C]
            bb_last = bb[0:1]
            gl_ref[gr, 0 * DK:1 * DK] = (q * jnp.exp2(bf)).astype(BF16)
            gl_ref[gr, 1 * DK:2 * DK] = (k * jnp.exp2(-bf)).astype(BF16)
            gl_ref[gr, 2 * DK:3 * DK] = (k * jnp.exp2(bf_last - bf)).astype(BF16)
            gl_ref[gr, 3 * DK:4 * DK] = (q * jnp.exp2(bb)).astype(BF16)
            gl_ref[gr, 4 * DK:5 * DK] = (k * jnp.exp2(-bb)).astype(BF16)
            gl_ref[gr, 5 * DK:6 * DK] = (k * jnp.exp2(bb_last - bb)).astype(BF16)
            d_ref[n:n + 1, :DK] = jnp.exp2(bf_last)
            d_ref[n:n + 1, DK:] = jnp.exp2(bb_last)

    zg = proj(5)

    for n in range(tm // GMLP_CHUNK):
        rs = slice(n * GMLP_CHUNK, (n + 1) * GMLP_CHUNK)
        for g in range(GMLP_GROUPS):
            cs = slice(g * GMLP_GC, (g + 1) * GMLP_GC)
            sg_scr[rs, cs] = _dot(ws_ref[g], vsn[rs, cs]) + bsm_ref[:, cs]
    g2 = ((_gelu_exact(u_pre) * sg_scr[...]) * _silu(zg)).astype(BF16)
    y_gmlp = _dot(g2, wbg_ref[...])

    mg_ref[...] = (jax.nn.sigmoid(proj(7)) * y_gmlp).astype(BF16)
    v_ref[...] = proj(1).astype(BF16)


def _projection(x2, mod3, norm_g, w_main, wa, ba, ln_g, ln_b, ws, bsm, w_br_gmlp, seq):
    T, D = x2.shape
    tm = PROJ_TM
    assert T % tm == 0 and seq % tm == 0 and tm % GMLP_CHUNK == 0 and tm % GLA_BLK == 0
    resident = functools.partial(pl.BlockSpec, pipeline_mode=pl.Buffered(1))
    tok = lambda w: pl.BlockSpec((tm, w), lambda i: (i, 0))
    tiles_per_seq = seq // tm
    out_bf = jax.ShapeDtypeStruct((T, D), BF16)
    return pl.pallas_call(
        _proj_kernel,
        grid=(T // tm,),
        in_specs=[
            tok(D),
            pl.BlockSpec((1, 3, D), lambda i: (i // tiles_per_seq, 0, 0)),
            resident((1, D), lambda i: (0, 0)),
            resident(w_main.shape, lambda i: (0, 0)),
            resident(wa.shape, lambda i: (0, 0)),
            resident(ba.shape, lambda i: (0, 0)),
            resident((1, D), lambda i: (0, 0)),
            resident((1, D), lambda i: (0, 0)),
            resident(ws.shape, lambda i: (0, 0, 0)),
            resident(bsm.shape, lambda i: (0, 0)),
            resident(w_br_gmlp.shape, lambda i: (0, 0)),
        ],
        out_specs=[tok(N_GLA_OPERANDS * GLA_DK), pl.BlockSpec((tm // GLA_CHUNK, 2 * GLA_DK), lambda i: (i, 0)),
                   tok(D), tok(D), tok(D), tok(D)],
        out_shape=[jax.ShapeDtypeStruct((T, N_GLA_OPERANDS * GLA_DK), BF16),
                   jax.ShapeDtypeStruct((T // GLA_CHUNK, 2 * GLA_DK), F32),
                   out_bf, out_bf, out_bf, out_bf],
        scratch_shapes=[pltpu.VMEM((tm, D), F32)],
        compiler_params=pltpu.CompilerParams(dimension_semantics=("arbitrary",),
                                             vmem_limit_bytes=VMEM_LIMIT_BYTES),
        name="projection_gmlp",
    )(x2, mod3, norm_g, w_main, wa, ba, ln_g, ln_b, ws, bsm, w_br_gmlp)


def _gla_kernel(qtf_ref, ktf_ref, kef_ref, qtb_ref, ktb_ref, keb_ref, v_ref, df_ref, db_ref, gz_ref, gng_ref,
                o_ref, oacc_scr, kvf_scr, kvb_scr):
    S = v_ref.shape[1]
    C = GLA_CHUNK
    hk = GLA_HK
    n_blk = S // GLA_BLK
    ti, si = _block_iotas()
    chunk_lo = (ti >> CHUNK_SHIFT) << CHUNK_SHIFT
    chunk_hi = chunk_lo + (C - 1)
    row_chunk = lax.broadcasted_iota(jnp.int32, (GLA_BLK, hk), 0) >> CHUNK_SHIFT
    chunk_sel = [jnp.where(row_chunk == c, 1.0, 0.0).astype(BF16) for c in range(CHUNKS_PER_BLK)]

    def local(j, carry):
        rows = pl.ds(pl.multiple_of(j * GLA_BLK, GLA_BLK), GLA_BLK)
        vb = v_ref[0, rows, :]

        def one_direction(qt_ref, kt_ref, ke_ref, forward):
            a = _dot_nt(qt_ref[0, rows, :], kt_ref[0, rows, :])
            if forward:
                a = jnp.where(si <= ti, jnp.where(si >= chunk_lo, a, 0.0), 0.0)
            else:
                a = jnp.where(si > ti, jnp.where(si <= chunk_hi, a, 0.0), 0.0)
            ke = ke_ref[0, rows, :]
            ke_bd = jnp.concatenate([ke * chunk_sel[c] for c in range(CHUNKS_PER_BLK)], axis=1)
            return _dot(a.astype(BF16), vb), ke_bd

        o_f, ke_f = one_direction(qtf_ref, ktf_ref, kef_ref, True)
        o_b, ke_b = one_direction(qtb_ref, ktb_ref, keb_ref, False)
        oacc_scr[rows, :] = o_f + o_b
        kv = _dot_tn(vb, jnp.concatenate([ke_f, ke_b], axis=1))
        kvf_scr[j] = kv[:, :CHUNKS_PER_BLK * hk]
        kvb_scr[j] = kv[:, CHUNKS_PER_BLK * hk:]
        return carry

    lax.fori_loop(0, n_blk, local, 0, unroll=GLA_UNROLL)

    def scan(i, states):
        sf, sb = states
        jf = i
        jb = n_blk - 1 - i
        for c in range(CHUNKS_PER_BLK):
            cb = CHUNKS_PER_BLK - 1 - c
            rf = pl.ds(pl.multiple_of(jf * GLA_BLK + c * C, C), C)
            rb = pl.ds(pl.multiple_of(jb * GLA_BLK + cb * C, C), C)
            oacc_scr[rf, :] += _dot_nt(qtf_ref[0, rf, :], sf.astype(BF16))
            oacc_scr[rb, :] += _dot_nt(qtb_ref[0, rb, :], sb.astype(BF16))
            sf = sf * df_ref[0, pl.ds(jf * CHUNKS_PER_BLK + c, 1), :] + kvf_scr[jf, :, c * hk:(c + 1) * hk]
            sb = sb * db_ref[0, pl.ds(jb * CHUNKS_PER_BLK + cb, 1), :] + kvb_scr[jb, :, cb * hk:(cb + 1) * hk]
        return sf, sb

    zero = jnp.zeros((GLA_HV, hk), F32)
    lax.fori_loop(0, n_blk, scan, (zero, zero), unroll=GLA_UNROLL)

    def finish_rows(i, carry):
        rows = pl.ds(pl.multiple_of(i * GLA_BLK, GLA_BLK), GLA_BLK)
        o = oacc_scr[rows, :]
        on = o * lax.rsqrt(jnp.mean(o * o, axis=-1, keepdims=True) + EPS) * gng_ref[0]
        o_ref[0, rows, :] = (on * gz_ref[0, rows, :].astype(F32)).astype(BF16)
        return carry

    lax.fori_loop(0, n_blk, finish_rows, 0, unroll=GLA_UNROLL)


def _gla(gl3, v3, d3, gz3, gng):
    B, S, D = v3.shape
    H, hk, hv = GLA_HEADS, GLA_HK, GLA_HV
    n_chunks = S // GLA_CHUNK
    assert S % GLA_BLK == 0
    operand = lambda g: pl.BlockSpec((1, S, hk), lambda b, h: (b, 0, g * H + h))
    decay = lambda g: pl.BlockSpec((1, n_chunks, hk), lambda b, h: (b, 0, g * H + h))
    head_v = pl.BlockSpec((1, S, hv), lambda b, h: (b, 0, h))
    kv_scratch = pltpu.VMEM((S // GLA_BLK, hv, CHUNKS_PER_BLK * hk), F32)
    return pl.pallas_call(
        _gla_kernel,
        grid=(B, H),
        in_specs=[operand(g) for g in range(N_GLA_OPERANDS)] + [
            head_v, decay(0), decay(1), head_v,
            pl.BlockSpec((1, 1, hv), lambda b, h: (h, 0, 0)),
        ],
        out_specs=head_v,
        out_shape=jax.ShapeDtypeStruct((B, S, D), BF16),
        scratch_shapes=[pltpu.VMEM((S, hv), F32), kv_scratch, kv_scratch],
        compiler_params=pltpu.CompilerParams(dimension_semantics=("arbitrary", "arbitrary"),
                                             vmem_limit_bytes=VMEM_LIMIT_BYTES),
        name="gla_bidir",
    )(*([gl3] * N_GLA_OPERANDS), v3, d3, d3, gz3, gng)


def _out_kernel(x_ref, g1_ref, sgl_ref, mg_ref, mod_ref, wbr_ref, wout_ref, fg_ref, o_ref):
    y_gla = _dot(g1_ref[...], wbr_ref[...])
    merged = (sgl_ref[...].astype(F32) * y_gla + mg_ref[...].astype(F32)).astype(BF16)
    gate = mod_ref[0, 2:3, :]
    r = x_ref[...] + gate * _dot(merged, wout_ref[...])
    o_ref[...] = (r * lax.rsqrt(jnp.mean(r * r, axis=-1, keepdims=True) + EPS)) * fg_ref[...]


def _merge_out(x2, g1, sgl, mg, mod3, w_br_gla, w_out, final_g, seq):
    T, D = x2.shape
    tm = OUT_TM
    assert T % tm == 0 and seq % tm == 0
    resident = functools.partial(pl.BlockSpec, pipeline_mode=pl.Buffered(1))
    tok = pl.BlockSpec((tm, D), lambda i: (i, 0))
    tiles_per_seq = seq // tm
    return pl.pallas_call(
        _out_kernel,
        grid=(T // tm,),
        in_specs=[
            tok, tok, tok, tok,
            pl.BlockSpec((1, 3, D), lambda i: (i // tiles_per_seq, 0, 0)),
            resident((D, D), lambda i: (0, 0)),
            resident((D, D), lambda i: (0, 0)),
            resident((1, D), lambda i: (0, 0)),
        ],
        out_specs=tok,
        out_shape=jax.ShapeDtypeStruct((T, D), F32),
        compiler_params=pltpu.CompilerParams(dimension_semantics=("arbitrary",),
                                             vmem_limit_bytes=VMEM_LIMIT_BYTES),
        name="merge_out",
    )(x2, g1, sgl, mg, mod3, w_br_gla, w_out, final_g)


def _layer(x, c, norm_g, w_ada, b_ada, w_in, alpha_fw_w, alpha_fw_b, alpha_bw_w, alpha_bw_b,
           gla_norm_g, gmlp_ln_g, gmlp_ln_b, gmlp_ws, gmlp_bs, w_br_gla, w_br_gmlp, w_out, out_g):
    B, S, D = x.shape
    H, hv, R, DK = GLA_HEADS, GLA_HV, GLA_RANK, GLA_DK
    T = B * S

    o_ra = 2 * DK + 2 * D
    w_ra = jnp.pad(w_in[:, o_ra:o_ra + 2 * R], ((0, 0), (0, RA_PAD - 2 * R)))
    w_main = jnp.concatenate([w_in[:, :2 * DK], w_ra, w_in[:, 2 * DK:o_ra], w_in[:, o_ra + 2 * R:]],
                             axis=1).astype(BF16)
    zeros = jnp.zeros_like(alpha_fw_w)
    wa = jnp.concatenate([jnp.concatenate([alpha_fw_w, zeros], axis=1),
                          jnp.concatenate([zeros, alpha_bw_w], axis=1)], axis=0)
    wa = jnp.pad(wa, ((0, RA_PAD - 2 * R), (0, 0))).astype(BF16)
    ba = jnp.concatenate([alpha_fw_b, alpha_bw_b]).reshape(1, 2 * DK)
    bsm = jnp.repeat(gmlp_bs.T, GMLP_GC, axis=1)

    mod3 = _adaln_mod(c, w_ada, b_ada).reshape(B, 3, D)
    x2 = x.reshape(T, D)
    gl, d, v, gz, sgl, mg = _projection(
        x2, mod3, norm_g.reshape(1, D), w_main, wa, ba, gmlp_ln_g.reshape(1, D), gmlp_ln_b.reshape(1, D),
        gmlp_ws.astype(BF16), bsm, w_br_gmlp.astype(BF16), S)
    g1 = _gla(gl.reshape(B, S, N_GLA_OPERANDS * DK), v.reshape(B, S, D), d.reshape(B, S // GLA_CHUNK, 2 * DK),
              gz.reshape(B, S, D), gla_norm_g.reshape(H, 1, hv))
    out = _merge_out(x2, g1.reshape(T, D), sgl, mg, mod3, w_br_gla.astype(BF16), w_out.astype(BF16),
                     out_g.reshape(1, D), S)
    return out.reshape(B, S, D)


def kernel(x, c, norm_g, w_ada, b_ada, w_in, alpha_fw_w, alpha_fw_b, alpha_bw_w, alpha_bw_b, gla_norm_g,
           gmlp_ln_g, gmlp_ln_b, gmlp_ws, gmlp_bs, w_br_gla, w_br_gmlp, w_out, final_g):
    depth = norm_g.shape[0]
    assert depth == 1, "the final rmsnorm is fused into the layer's output kernel"
    return _layer(x, c, norm_g[0], w_ada[0], b_ada[0], w_in[0], alpha_fw_w[0], alpha_fw_b[0], alpha_bw_w[0],
                  alpha_bw_b[0], gla_norm_g[0], gmlp_ln_g[0], gmlp_ln_b[0], gmlp_ws[0], gmlp_bs[0],
                  w_br_gla[0], w_br_gmlp[0], w_out[0], final_g)
```

```python
import functools

import jax
import jax.numpy as jnp
from jax import lax
from jax.experimental import pallas as pl
from jax.experimental.pallas import tpu as pltpu

F32 = jnp.float32
BF16 = jnp.bfloat16

D_MODEL = 1024
GLA_HEADS = 4
GLA_HK = 128
GLA_HV = 256
GLA_DK = GLA_HEADS * GLA_HK
GLA_RANK = 16
GLA_TAU = 16.0
GLA_CHUNK = 64
LOG_DECAY_FLOOR = -1.25
GMLP_GROUPS = 8
GMLP_GC = 128
GMLP_CHUNK = 128
EPS = 1e-6
SQRT_HALF = 0.7071067811865476
LOG2_E = 1.4426950408889634

RA_PAD = 128
MXU_K = 256
N_LO_GROUPS = 3
VMEM_LIMIT_BYTES = 56 * 1024 * 1024

PROJ_TM = 512
OUT_TM = 1024
GLA_BLK = 256
CHUNKS_PER_BLK = GLA_BLK // GLA_CHUNK
CHUNK_SHIFT = GLA_CHUNK.bit_length() - 1
N_GLA_OPERANDS = 6
GLA_UNROLL = 4


def _silu(z):
    return z * jax.nn.sigmoid(z)


def _gelu_exact(z):
    return 0.5 * z * (1.0 + lax.erf(z * SQRT_HALF))


def _log_sigmoid(z):
    return jnp.minimum(z, 0.0) - jnp.log1p(jnp.exp(-jnp.abs(z)))


def _dot(a, b):
    return jnp.dot(a, b, preferred_element_type=F32)


def _dot_nt(a, b):
    return lax.dot_general(a, b, (((1,), (1,)), ((), ())), preferred_element_type=F32)


def _dot_tn(a, b):
    return lax.dot_general(a, b, (((0,), (0,)), ((), ())), preferred_element_type=F32)


def _block_iotas():
    ti = lax.broadcasted_iota(jnp.int32, (GLA_BLK, GLA_BLK), 0)
    si = lax.broadcasted_iota(jnp.int32, (GLA_BLK, GLA_BLK), 1)
    return ti, si


def _mod_kernel(c_ref, w_ref, b_ref, o_ref):
    a = _silu(c_ref[...]).astype(BF16)
    o_ref[...] = _dot(a, w_ref[...].astype(BF16)) + b_ref[...]


def _adaln_mod(c, w_ada, b_ada):
    B, D = c.shape
    n_out = w_ada.shape[1]
    return pl.pallas_call(
        _mod_kernel,
        grid=(n_out // D,),
        in_specs=[
            pl.BlockSpec((B, D), lambda j: (0, 0)),
            pl.BlockSpec((D, D), lambda j: (0, j)),
            pl.BlockSpec((1, D), lambda j: (0, j)),
        ],
        out_specs=pl.BlockSpec((B, D), lambda j: (0, j)),
        out_shape=jax.ShapeDtypeStruct((B, n_out), F32),
        compiler_params=pltpu.CompilerParams(dimension_semantics=("arbitrary",)),
        name="adaln_mod",
    )(c, w_ada, b_ada.reshape(1, n_out))


def _proj_kernel(x_ref, mod_ref, ng_ref, wlo_ref, whi_ref, wra_ref, wa_ref, ba_ref, lng_ref, lnb_ref, ws_ref, bsm_ref,
                 wbg_ref, gl_ref, d_ref, v_ref, gz_ref, sgl_ref, mg_ref, sg_scr):
    D = D_MODEL
    DK = GLA_DK
    C = GLA_CHUNK
    tm = x_ref.shape[0]
    half = tm // 2
    shift = mod_ref[0, 0:1, :]
    gain = ng_ref[...] * (1.0 + mod_ref[0, 1:2, :])

    def normed(rows):
        x = x_ref[rows, :]
        r = lax.rsqrt(jnp.mean(x * x, axis=-1, keepdims=True) + EPS)
        return ((x * r) * gain + shift).astype(BF16)

    h_top = normed(slice(0, half))
    h_bot = normed(slice(half, tm))
    h = jnp.concatenate([h_top, h_bot], axis=0)

    def proj(j):
        if j < N_LO_GROUPS:
            return _dot(h, wlo_ref[:, j * D:(j + 1) * D])
        return _dot(h, whi_ref[:, (j - N_LO_GROUPS) * D:(j - N_LO_GROUPS + 1) * D])


    qk = jnp.concatenate([_dot(h_top, wlo_ref[:, :D]), _dot(h_bot, wlo_ref[:, :D])], axis=0)
    ra = jnp.concatenate([_dot(h_top, wra_ref[...]), _dot(h_bot, wra_ref[...])], axis=0).astype(BF16)
    pre = _dot(ra, wa_ref[...]) + ba_ref[...]
    la = jnp.maximum(_log_sigmoid(pre) * (1.0 / GLA_TAU), LOG_DECAY_FLOOR) * LOG2_E

    v_ref[...] = proj(1).astype(BF16)

    hi = la.astype(BF16)
    r1 = la - hi.astype(F32)
    mid = r1.astype(BF16)
    lo = (r1 - mid.astype(F32)).astype(BF16)

    u_pre = proj(3)

    ti, si = _block_iotas()
    same_chunk = (ti >> CHUNK_SHIFT) == (si >> CHUNK_SHIFT)
    tri_f = jnp.where(same_chunk & (si <= ti), 1.0, 0.0).astype(BF16)
    tri_b = jnp.where(same_chunk & (si >= ti), 1.0, 0.0).astype(BF16)
    cum = []
    for blk in range(tm // GLA_BLK):
        rs = slice(blk * GLA_BLK, (blk + 1) * GLA_BLK)
        b_f = _dot(tri_f, hi[rs, :DK]) + _dot(tri_f, mid[rs, :DK]) + _dot(tri_f, lo[rs, :DK])
        b_b = _dot(tri_b, hi[rs, DK:]) + _dot(tri_b, mid[rs, DK:]) + _dot(tri_b, lo[rs, DK:])
        cum.append((b_f, b_b))

    sgl_ref[...] = jax.nn.sigmoid(proj(6)).astype(BF16)

    q_scale = GLA_HK ** -0.5
    for blk in range(tm // GLA_BLK):
        b_f, b_b = cum[blk]
        for c in range(CHUNKS_PER_BLK):
            cs = slice(c * C, (c + 1) * C)
            gr = slice(blk * GLA_BLK + c * C, blk * GLA_BLK + (c + 1) * C)
            n = blk * CHUNKS_PER_BLK + c
            q = qk[gr, :DK] * q_scale
            k = qk[gr, DK:]
            bf = b_f[cs]
            bb = b_b[cs]
            bf_last = bf[C - 1:C]
            bb_last = bb[0:1]
            gl_ref[gr, 0 * DK:1 * DK] = (q * jnp.exp2(bf)).astype(BF16)
            gl_ref[gr, 1 * DK:2 * DK] = (k * jnp.exp2(-bf)).astype(BF16)
            gl_ref[gr, 2 * DK:3 * DK] = (k * jnp.exp2(bf_last - bf)).astype(BF16)
            gl_ref[gr, 3 * DK:4 * DK] = (q * jnp.exp2(bb)).astype(BF16)
            gl_ref[gr, 4 * DK:5 * DK] = (k * jnp.exp2(-bb)).astype(BF16)
            gl_ref[gr, 5 * DK:6 * DK] = (k * jnp.exp2(bb_last - bb)).astype(BF16)
            d_ref[n:n + 1, :DK] = jnp.exp2(bf_last)
            d_ref[n:n + 1, DK:] = jnp.exp2(bb_last)

    zg = proj(5)

    vs = _gelu_exact(proj(4))
    mu = jnp.mean(vs, axis=-1, keepdims=True)
    vc = vs - mu
    var = jnp.mean(vc * vc, axis=-1, keepdims=True)
    vsn = ((vc * lax.rsqrt(var + EPS)) * lng_ref[...] + lnb_ref[...]).astype(BF16)

    gz_ref[...] = _silu(proj(2)).astype(BF16)

    n_pos = tm // GMLP_CHUNK
    for g in range(GMLP_GROUPS):
        cs = slice(g * GMLP_GC, (g + 1) * GMLP_GC)
        rhs = jnp.concatenate([vsn[n * GMLP_CHUNK:(n + 1) * GMLP_CHUNK, cs] for n in range(n_pos)], axis=1)
        mixed = _dot(ws_ref[g], rhs)
        for n in range(n_pos):
            sg_scr[n * GMLP_CHUNK:(n + 1) * GMLP_CHUNK, cs] = (
                mixed[:, n * GMLP_GC:(n + 1) * GMLP_GC] + bsm_ref[:, cs])
    merge_gate = jax.nn.sigmoid(proj(7))

    y_gmlp = None
    for kt in range(D // MXU_K):
        ks = slice(kt * MXU_K, (kt + 1) * MXU_K)
        g2 = ((_gelu_exact(u_pre[:, ks]) * sg_scr[:, ks]) * _silu(zg[:, ks])).astype(BF16)
        part = _dot(g2, wbg_ref[ks, :])
        y_gmlp = part if y_gmlp is None else y_gmlp + part
    mg_ref[...] = (merge_gate * y_gmlp).astype(BF16)


def _projection(x2, mod3, norm_g, w_lo, w_hi, w_ra, wa, ba, ln_g, ln_b, ws, bsm, w_br_gmlp, seq):
    T, D = x2.shape
    tm = PROJ_TM
    assert T % tm == 0 and seq % tm == 0 and tm % GMLP_CHUNK == 0 and tm % GLA_BLK == 0
    resident = functools.partial(pl.BlockSpec, pipeline_mode=pl.Buffered(1))
    tok = lambda w: pl.BlockSpec((tm, w), lambda i: (i, 0))
    tiles_per_seq = seq // tm
    out_bf = jax.ShapeDtypeStruct((T, D), BF16)
    return pl.pallas_call(
        _proj_kernel,
        grid=(T // tm,),
        in_specs=[
            tok(D),
            pl.BlockSpec((1, 3, D), lambda i: (i // tiles_per_seq, 0, 0)),
            resident((1, D), lambda i: (0, 0)),
            resident(w_lo.shape, lambda i: (0, 0)),
            resident(w_hi.shape, lambda i: (0, 0)),
            resident(w_ra.shape, lambda i: (0, 0)),
            resident(wa.shape, lambda i: (0, 0)),
            resident(ba.shape, lambda i: (0, 0)),
            resident((1, D), lambda i: (0, 0)),
            resident((1, D), lambda i: (0, 0)),
            resident(ws.shape, lambda i: (0, 0, 0)),
            resident(bsm.shape, lambda i: (0, 0)),
            resident(w_br_gmlp.shape, lambda i: (0, 0)),
        ],
        out_specs=[tok(N_GLA_OPERANDS * GLA_DK), pl.BlockSpec((tm // GLA_CHUNK, 2 * GLA_DK), lambda i: (i, 0)),
                   tok(D), tok(D), tok(D), tok(D)],
        out_shape=[jax.ShapeDtypeStruct((T, N_GLA_OPERANDS * GLA_DK), BF16),
                   jax.ShapeDtypeStruct((T // GLA_CHUNK, 2 * GLA_DK), F32),
                   out_bf, out_bf, out_bf, out_bf],
        scratch_shapes=[pltpu.VMEM((tm, D), F32)],
        compiler_params=pltpu.CompilerParams(dimension_semantics=("arbitrary",),
                                             vmem_limit_bytes=VMEM_LIMIT_BYTES),
        name="projection_gmlp",
    )(x2, mod3, norm_g, w_lo, w_hi, w_ra, wa, ba, ln_g, ln_b, ws, bsm, w_br_gmlp)


def _gla_kernel(qtf_ref, ktf_ref, kef_ref, qtb_ref, ktb_ref, keb_ref, v_ref, df_ref, db_ref, gz_ref, gng_ref,
                o_ref, oacc_scr, kvf_scr, kvb_scr):
    S = v_ref.shape[1]
    C = GLA_CHUNK
    hk = GLA_HK
    n_blk = S // GLA_BLK
    ti, si = _block_iotas()
    chunk_lo = (ti >> CHUNK_SHIFT) << CHUNK_SHIFT
    chunk_hi = chunk_lo + (C - 1)
    row_chunk = lax.broadcasted_iota(jnp.int32, (GLA_BLK, hk), 0) >> CHUNK_SHIFT
    chunk_sel = [jnp.where(row_chunk == c, 1.0, 0.0).astype(BF16) for c in range(CHUNKS_PER_BLK)]

    def local(j, carry):
        rows = pl.ds(pl.multiple_of(j * GLA_BLK, GLA_BLK), GLA_BLK)
        vb = v_ref[0, rows, :]

        def one_direction(qt_ref, kt_ref, ke_ref, forward):
            a = _dot_nt(qt_ref[0, rows, :], kt_ref[0, rows, :])
            if forward:
                a = jnp.where(si <= ti, jnp.where(si >= chunk_lo, a, 0.0), 0.0)
            else:
                a = jnp.where(si > ti, jnp.where(si <= chunk_hi, a, 0.0), 0.0)
            ke = ke_ref[0, rows, :]
            ke_bd = jnp.concatenate([ke * chunk_sel[c] for c in range(CHUNKS_PER_BLK)], axis=1)
            return _dot(a.astype(BF16), vb), ke_bd

        o_f, ke_f = one_direction(qtf_ref, ktf_ref, kef_ref, True)
        o_b, ke_b = one_direction(qtb_ref, ktb_ref, keb_ref, False)
        oacc_scr[rows, :] = o_f + o_b
        kv = _dot_tn(vb, jnp.concatenate([ke_f, ke_b], axis=1))
        kvf_scr[j] = kv[:, :CHUNKS_PER_BLK * hk]
        kvb_scr[j] = kv[:, CHUNKS_PER_BLK * hk:]
        return carry

    lax.fori_loop(0, n_blk, local, 0, unroll=GLA_UNROLL)

    def scan(i, states):
        sf, sb = states
        jf = i
        jb = n_blk - 1 - i
        for c in range(CHUNKS_PER_BLK):
            cb = CHUNKS_PER_BLK - 1 - c
            rf = pl.ds(pl.multiple_of(jf * GLA_BLK + c * C, C), C)
            rb = pl.ds(pl.multiple_of(jb * GLA_BLK + cb * C, C), C)
            oacc_scr[rf, :] += _dot_nt(qtf_ref[0, rf, :], sf.astype(BF16))
            oacc_scr[rb, :] += _dot_nt(qtb_ref[0, rb, :], sb.astype(BF16))
            sf = sf * df_ref[0, pl.ds(jf * CHUNKS_PER_BLK + c, 1), :] + kvf_scr[jf, :, c * hk:(c + 1) * hk]
            sb = sb * db_ref[0, pl.ds(jb * CHUNKS_PER_BLK + cb, 1), :] + kvb_scr[jb, :, cb * hk:(cb + 1) * hk]
        return sf, sb

    zero = jnp.zeros((GLA_HV, hk), F32)
    lax.fori_loop(0, n_blk, scan, (zero, zero), unroll=GLA_UNROLL)

    def finish_rows(i, carry):
        rows = pl.ds(pl.multiple_of(i * GLA_BLK, GLA_BLK), GLA_BLK)
        o = oacc_scr[rows, :]
        on = o * lax.rsqrt(jnp.mean(o * o, axis=-1, keepdims=True) + EPS) * gng_ref[0]
        o_ref[0, rows, :] = (on * gz_ref[0, rows, :].astype(F32)).astype(BF16)
        return carry

    lax.fori_loop(0, n_blk, finish_rows, 0, unroll=GLA_UNROLL)


def _gla(gl3, v3, d3, gz3, gng):
    B, S, D = v3.shape
    H, hk, hv = GLA_HEADS, GLA_HK, GLA_HV
    n_chunks = S // GLA_CHUNK
    assert S % GLA_BLK == 0
    operand = lambda g: pl.BlockSpec((1, S, hk), lambda b, h: (b, 0, g * H + h))
    decay = lambda g: pl.BlockSpec((1, n_chunks, hk), lambda b, h: (b, 0, g * H + h))
    head_v = pl.BlockSpec((1, S, hv), lambda b, h: (b, 0, h))
    kv_scratch = pltpu.VMEM((S // GLA_BLK, hv, CHUNKS_PER_BLK * hk), F32)
    return pl.pallas_call(
        _gla_kernel,
        grid=(B, H),
        in_specs=[operand(g) for g in range(N_GLA_OPERANDS)] + [
            head_v, decay(0), decay(1), head_v,
            pl.BlockSpec((1, 1, hv), lambda b, h: (h, 0, 0)),
        ],
        out_specs=head_v,
        out_shape=jax.ShapeDtypeStruct((B, S, D), BF16),
        scratch_shapes=[pltpu.VMEM((S, hv), F32), kv_scratch, kv_scratch],
        compiler_params=pltpu.CompilerParams(dimension_semantics=("arbitrary", "arbitrary"),
                                             vmem_limit_bytes=VMEM_LIMIT_BYTES),
        name="gla_bidir",
    )(*([gl3] * N_GLA_OPERANDS), v3, d3, d3, gz3, gng)


def _out_kernel(x_ref, g1_ref, sgl_ref, mg_ref, mod_ref, wbr_ref, wout_ref, fg_ref, o_ref):
    y_gla = _dot(g1_ref[...], wbr_ref[...])
    merged = (sgl_ref[...].astype(F32) * y_gla + mg_ref[...].astype(F32)).astype(BF16)
    gate = mod_ref[0, 2:3, :]
    r = x_ref[...] + gate * _dot(merged, wout_ref[...])
    o_ref[...] = (r * lax.rsqrt(jnp.mean(r * r, axis=-1, keepdims=True) + EPS)) * fg_ref[...]


def _merge_out(x2, g1, sgl, mg, mod3, w_br_gla, w_out, final_g, seq):
    T, D = x2.shape
    tm = OUT_TM
    assert T % tm == 0 and seq % tm == 0
    resident = functools.partial(pl.BlockSpec, pipeline_mode=pl.Buffered(1))
    tok = pl.BlockSpec((tm, D), lambda i: (i, 0))
    tiles_per_seq = seq // tm
    return pl.pallas_call(
        _out_kernel,
        grid=(T // tm,),
        in_specs=[
            tok, tok, tok, tok,
            pl.BlockSpec((1, 3, D), lambda i: (i // tiles_per_seq, 0, 0)),
            resident((D, D), lambda i: (0, 0)),
            resident((D, D), lambda i: (0, 0)),
            resident((1, D), lambda i: (0, 0)),
        ],
        out_specs=tok,
        out_shape=jax.ShapeDtypeStruct((T, D), F32),
        compiler_params=pltpu.CompilerParams(dimension_semantics=("arbitrary",),
                                             vmem_limit_bytes=VMEM_LIMIT_BYTES),
        name="merge_out",
    )(x2, g1, sgl, mg, mod3, w_br_gla, w_out, final_g)


def _layer(x, c, norm_g, w_ada, b_ada, w_in, alpha_fw_w, alpha_fw_b, alpha_bw_w, alpha_bw_b,
           gla_norm_g, gmlp_ln_g, gmlp_ln_b, gmlp_ws, gmlp_bs, w_br_gla, w_br_gmlp, w_out, out_g):
    B, S, D = x.shape
    H, hv, R, DK = GLA_HEADS, GLA_HV, GLA_RANK, GLA_DK
    T = B * S

    o_ra = 2 * DK + 2 * D
    assert o_ra == N_LO_GROUPS * D
    w_lo = w_in[:, :o_ra].astype(BF16)
    w_hi = w_in[:, o_ra + 2 * R:].astype(BF16)
    w_ra = jnp.pad(w_in[:, o_ra:o_ra + 2 * R], ((0, 0), (0, RA_PAD - 2 * R))).astype(BF16)
    zeros = jnp.zeros_like(alpha_fw_w)
    wa = jnp.concatenate([jnp.concatenate([alpha_fw_w, zeros], axis=1),
                          jnp.concatenate([zeros, alpha_bw_w], axis=1)], axis=0)
    wa = jnp.pad(wa, ((0, RA_PAD - 2 * R), (0, 0))).astype(BF16)
    ba = jnp.concatenate([alpha_fw_b, alpha_bw_b]).reshape(1, 2 * DK)
    bsm = jnp.repeat(gmlp_bs.T, GMLP_GC, axis=1)

    mod3 = _adaln_mod(c, w_ada, b_ada).reshape(B, 3, D)
    x2 = x.reshape(T, D)
    gl, d, v, gz, sgl, mg = _projection(
        x2, mod3, norm_g.reshape(1, D), w_lo, w_hi, w_ra, wa, ba, gmlp_ln_g.reshape(1, D), gmlp_ln_b.reshape(1, D),
        gmlp_ws.astype(BF16), bsm, w_br_gmlp.astype(BF16), S)
    g1 = _gla(gl.reshape(B, S, N_GLA_OPERANDS * DK), v.reshape(B, S, D), d.reshape(B, S // GLA_CHUNK, 2 * DK),
              gz.reshape(B, S, D), gla_norm_g.reshape(H, 1, hv))
    out = _merge_out(x2, g1.reshape(T, D), sgl, mg, mod3, w_br_gla.astype(BF16), w_out.astype(BF16),
                     out_g.reshape(1, D), S)
    return out.reshape(B, S, D)


def kernel(x, c, norm_g, w_ada, b_ada, w_in, alpha_fw_w, alpha_fw_b, alpha_bw_w, alpha_bw_b, gla_norm_g,
           gmlp_ln_g, gmlp_ln_b, gmlp_ws, gmlp_bs, w_br_gla, w_br_gmlp, w_out, final_g):
    depth = norm_g.shape[0]
    assert depth == 1, "the final rmsnorm is fused into the layer's output kernel"
    return _layer(x, c, norm_g[0], w_ada[0], b_ada[0], w_in[0], alpha_fw_w[0], alpha_fw_b[0], alpha_bw_w[0],
                  alpha_bw_b[0], gla_norm_g[0], gmlp_ln_g[0], gmlp_ln_b[0], gmlp_ws[0], gmlp_bs[0],
                  w_br_gla[0], w_br_gmlp[0], w_out[0], final_g)
```

```python
import functools

import jax
import jax.numpy as jnp
from jax import lax
from jax.experimental import pallas as pl
from jax.experimental.pallas import tpu as pltpu

F32 = jnp.float32
BF16 = jnp.bfloat16

D_MODEL = 1024
GLA_HEADS = 4
GLA_HK = 128
GLA_HV = 256
GLA_DK = GLA_HEADS * GLA_HK
GLA_RANK = 16
GLA_TAU = 16.0
GLA_CHUNK = 64
LOG_DECAY_FLOOR = -1.25
GMLP_GROUPS = 8
GMLP_GC = 128
GMLP_CHUNK = 128
EPS = 1e-6
SQRT_HALF = 0.7071067811865476
LOG2_E = 1.4426950408889634

RA_PAD = 128
MXU_K = 256
N_LO_GROUPS = 3
VMEM_LIMIT_BYTES = 56 * 1024 * 1024

PROJ_TM = 512
OUT_TM = 1024
GLA_BLK = 256
CHUNKS_PER_BLK = GLA_BLK // GLA_CHUNK
CHUNK_SHIFT = GLA_CHUNK.bit_length() - 1
N_GLA_OPERANDS = 6


def _silu(z):
    return z * jax.nn.sigmoid(z)


def _gelu_exact(z):
    return 0.5 * z * (1.0 + lax.erf(z * SQRT_HALF))


def _log_sigmoid(z):
    return jnp.minimum(z, 0.0) - jnp.log1p(jnp.exp(-jnp.abs(z)))


def _dot(a, b):
    return jnp.dot(a, b, preferred_element_type=F32)


def _dot_nt(a, b):
    return lax.dot_general(a, b, (((1,), (1,)), ((), ())), preferred_element_type=F32)


def _dot_tn(a, b):
    return lax.dot_general(a, b, (((0,), (0,)), ((), ())), preferred_element_type=F32)


def _aligned_rows(start, size):
    if isinstance(start, int):
        return pl.ds(start, size)
    return pl.ds(pl.multiple_of(start, size), size)


def _block_iotas():
    ti = lax.broadcasted_iota(jnp.int32, (GLA_BLK, GLA_BLK), 0)
    si = lax.broadcasted_iota(jnp.int32, (GLA_BLK, GLA_BLK), 1)
    return ti, si


def _mod_kernel(c_ref, w_ref, b_ref, o_ref):
    a = _silu(c_ref[...]).astype(BF16)
    o_ref[...] = _dot(a, w_ref[...].astype(BF16)) + b_ref[...]


def _adaln_mod(c, w_ada, b_ada):
    B, D = c.shape
    n_out = w_ada.shape[1]
    return pl.pallas_call(
        _mod_kernel,
        grid=(n_out // D,),
        in_specs=[
            pl.BlockSpec((B, D), lambda j: (0, 0)),
            pl.BlockSpec((D, D), lambda j: (0, j)),
            pl.BlockSpec((1, D), lambda j: (0, j)),
        ],
        out_specs=pl.BlockSpec((B, D), lambda j: (0, j)),
        out_shape=jax.ShapeDtypeStruct((B, n_out), F32),
        compiler_params=pltpu.CompilerParams(dimension_semantics=("arbitrary",)),
        name="adaln_mod",
    )(c, w_ada, b_ada.reshape(1, n_out))


def _proj_kernel(x_ref, mod_ref, ng_ref, wlo_ref, whi_ref, wra_ref, wa_ref, ba_ref, lng_ref, lnb_ref, ws_ref, bsm_ref,
                 wbg_ref, gl_ref, d_ref, v_ref, gz_ref, sgl_ref, mg_ref, sg_scr):
    D = D_MODEL
    DK = GLA_DK
    C = GLA_CHUNK
    tm = x_ref.shape[0]
    half = tm // 2
    shift = mod_ref[0, 0:1, :]
    gain = ng_ref[...] * (1.0 + mod_ref[0, 1:2, :])

    def normed(rows):
        x = x_ref[rows, :]
        r = lax.rsqrt(jnp.mean(x * x, axis=-1, keepdims=True) + EPS)
        return ((x * r) * gain + shift).astype(BF16)

    h_top = normed(slice(0, half))
    h_bot = normed(slice(half, tm))
    h = jnp.concatenate([h_top, h_bot], axis=0)

    def proj(j):
        if j < N_LO_GROUPS:
            return _dot(h, wlo_ref[:, j * D:(j + 1) * D])
        return _dot(h, whi_ref[:, (j - N_LO_GROUPS) * D:(j - N_LO_GROUPS + 1) * D])


    qk = jnp.concatenate([_dot(h_top, wlo_ref[:, :D]), _dot(h_bot, wlo_ref[:, :D])], axis=0)
    ra = jnp.concatenate([_dot(h_top, wra_ref[...]), _dot(h_bot, wra_ref[...])], axis=0).astype(BF16)
    pre = _dot(ra, wa_ref[...]) + ba_ref[...]
    la = jnp.maximum(_log_sigmoid(pre) * (1.0 / GLA_TAU), LOG_DECAY_FLOOR) * LOG2_E

    v_ref[...] = proj(1).astype(BF16)

    hi = la.astype(BF16)
    r1 = la - hi.astype(F32)
    mid = r1.astype(BF16)
    lo = (r1 - mid.astype(F32)).astype(BF16)

    u_pre = proj(3)

    ti, si = _block_iotas()
    same_chunk = (ti >> CHUNK_SHIFT) == (si >> CHUNK_SHIFT)
    tri_f = jnp.where(same_chunk & (si <= ti), 1.0, 0.0).astype(BF16)
    tri_b = jnp.where(same_chunk & (si >= ti), 1.0, 0.0).astype(BF16)
    cum = []
    for blk in range(tm // GLA_BLK):
        rs = slice(blk * GLA_BLK, (blk + 1) * GLA_BLK)
        b_f = _dot(tri_f, hi[rs, :DK]) + _dot(tri_f, mid[rs, :DK]) + _dot(tri_f, lo[rs, :DK])
        b_b = _dot(tri_b, hi[rs, DK:]) + _dot(tri_b, mid[rs, DK:]) + _dot(tri_b, lo[rs, DK:])
        cum.append((b_f, b_b))

    sgl_ref[...] = jax.nn.sigmoid(proj(6)).astype(BF16)

    q_scale = GLA_HK ** -0.5
    for blk in range(tm // GLA_BLK):
        b_f, b_b = cum[blk]
        for c in range(CHUNKS_PER_BLK):
            cs = slice(c * C, (c + 1) * C)
            gr = slice(blk * GLA_BLK + c * C, blk * GLA_BLK + (c + 1) * C)
            n = blk * CHUNKS_PER_BLK + c
            q = qk[gr, :DK] * q_scale
            k = qk[gr, DK:]
            bf = b_f[cs]
            bb = b_b[cs]
            bf_last = bf[C - 1:C]
            bb_last = bb[0:1]
            gl_ref[gr, 0 * DK:1 * DK] = (q * jnp.exp2(bf)).astype(BF16)
            gl_ref[gr, 1 * DK:2 * DK] = (k * jnp.exp2(-bf)).astype(BF16)
            gl_ref[gr, 2 * DK:3 * DK] = (k * jnp.exp2(bf_last - bf)).astype(BF16)
            gl_ref[gr, 3 * DK:4 * DK] = (q * jnp.exp2(bb)).astype(BF16)
            gl_ref[gr, 4 * DK:5 * DK] = (k * jnp.exp2(-bb)).astype(BF16)
            gl_ref[gr, 5 * DK:6 * DK] = (k * jnp.exp2(bb_last - bb)).astype(BF16)
            d_ref[n:n + 1, :DK] = jnp.exp2(bf_last)
            d_ref[n:n + 1, DK:] = jnp.exp2(bb_last)

    zg = proj(5)

    vs = _gelu_exact(proj(4))
    mu = jnp.mean(vs, axis=-1, keepdims=True)
    vc = vs - mu
    var = jnp.mean(vc * vc, axis=-1, keepdims=True)
    vsn = ((vc * lax.rsqrt(var + EPS)) * lng_ref[...] + lnb_ref[...]).astype(BF16)

    gz_ref[...] = _silu(proj(2)).astype(BF16)

    n_pos = tm // GMLP_CHUNK
    for g in range(GMLP_GROUPS):
        cs = slice(g * GMLP_GC, (g + 1) * GMLP_GC)
        rhs = jnp.concatenate([vsn[n * GMLP_CHUNK:(n + 1) * GMLP_CHUNK, cs] for n in range(n_pos)], axis=1)
        mixed = _dot(ws_ref[g], rhs)
        for n in range(n_pos):
            sg_scr[n * GMLP_CHUNK:(n + 1) * GMLP_CHUNK, cs] = (
                mixed[:, n * GMLP_GC:(n + 1) * GMLP_GC] + bsm_ref[:, cs])
    merge_gate = jax.nn.sigmoid(proj(7))

    y_gmlp = None
    for kt in range(D // MXU_K):
        ks = slice(kt * MXU_K, (kt + 1) * MXU_K)
        g2 = ((_gelu_exact(u_pre[:, ks]) * sg_scr[:, ks]) * _silu(zg[:, ks])).astype(BF16)
        part = _dot(g2, wbg_ref[ks, :])
        y_gmlp = part if y_gmlp is None else y_gmlp + part
    mg_ref[...] = (merge_gate * y_gmlp).astype(BF16)


def _projection(x2, mod3, norm_g, w_lo, w_hi, w_ra, wa, ba, ln_g, ln_b, ws, bsm, w_br_gmlp, seq):
    T, D = x2.shape
    tm = PROJ_TM
    assert T % tm == 0 and seq % tm == 0 and tm % GMLP_CHUNK == 0 and tm % GLA_BLK == 0
    resident = functools.partial(pl.BlockSpec, pipeline_mode=pl.Buffered(1))
    tok = lambda w: pl.BlockSpec((tm, w), lambda i: (i, 0))
    tiles_per_seq = seq // tm
    out_bf = jax.ShapeDtypeStruct((T, D), BF16)
    return pl.pallas_call(
        _proj_kernel,
        grid=(T // tm,),
        in_specs=[
            tok(D),
            pl.BlockSpec((1, 3, D), lambda i: (i // tiles_per_seq, 0, 0)),
            resident((1, D), lambda i: (0, 0)),
            resident(w_lo.shape, lambda i: (0, 0)),
            resident(w_hi.shape, lambda i: (0, 0)),
            resident(w_ra.shape, lambda i: (0, 0)),
            resident(wa.shape, lambda i: (0, 0)),
            resident(ba.shape, lambda i: (0, 0)),
            resident((1, D), lambda i: (0, 0)),
            resident((1, D), lambda i: (0, 0)),
            resident(ws.shape, lambda i: (0, 0, 0)),
            resident(bsm.shape, lambda i: (0, 0)),
            resident(w_br_gmlp.shape, lambda i: (0, 0)),
        ],
        out_specs=[tok(N_GLA_OPERANDS * GLA_DK), pl.BlockSpec((tm // GLA_CHUNK, 2 * GLA_DK), lambda i: (i, 0)),
                   tok(D), tok(D), tok(D), tok(D)],
        out_shape=[jax.ShapeDtypeStruct((T, N_GLA_OPERANDS * GLA_DK), BF16),
                   jax.ShapeDtypeStruct((T // GLA_CHUNK, 2 * GLA_DK), F32),
                   out_bf, out_bf, out_bf, out_bf],
        scratch_shapes=[pltpu.VMEM((tm, D), F32)],
        compiler_params=pltpu.CompilerParams(dimension_semantics=("arbitrary",),
                                             vmem_limit_bytes=VMEM_LIMIT_BYTES),
        name="projection_gmlp",
    )(x2, mod3, norm_g, w_lo, w_hi, w_ra, wa, ba, ln_g, ln_b, ws, bsm, w_br_gmlp)


def _gla_kernel(qtf_ref, ktf_ref, kef_ref, qtb_ref, ktb_ref, keb_ref, v_ref, df_ref, db_ref, gz_ref, gng_ref,
                o_ref, of_scr, ob_scr, kvf_scr, kvb_scr):
    S = v_ref.shape[1]
    C = GLA_CHUNK
    hk = GLA_HK
    n_blk = S // GLA_BLK
    ti, si = _block_iotas()
    chunk_lo = (ti >> CHUNK_SHIFT) << CHUNK_SHIFT
    chunk_hi = chunk_lo + (C - 1)
    row_chunk = lax.broadcasted_iota(jnp.int32, (GLA_BLK, hk), 0) >> CHUNK_SHIFT
    chunk_sel = [jnp.where(row_chunk == c, 1.0, 0.0).astype(BF16) for c in range(CHUNKS_PER_BLK)]
    fwd = (qtf_ref, ktf_ref, kef_ref, df_ref, of_scr, kvf_scr)
    bwd = (qtb_ref, ktb_ref, keb_ref, db_ref, ob_scr, kvb_scr)

    def scores(j, operands, forward):
        qt_ref, kt_ref, _, _, _, _ = operands
        rows = _aligned_rows(j * GLA_BLK, GLA_BLK)
        a = _dot_nt(qt_ref[0, rows, :], kt_ref[0, rows, :])
        if forward:
            a = jnp.where(si <= ti, jnp.where(si >= chunk_lo, a, 0.0), 0.0)
        else:
            a = jnp.where(si > ti, jnp.where(si <= chunk_hi, a, 0.0), 0.0)
        return a.astype(BF16)

    def chunk_states(j, operands):
        _, _, ke_ref, _, _, kv_scr = operands
        rows = _aligned_rows(j * GLA_BLK, GLA_BLK)
        ke = ke_ref[0, rows, :]
        ke_bd = jnp.concatenate([ke * chunk_sel[c] for c in range(CHUNKS_PER_BLK)], axis=1)
        kv_scr[j] = _dot_tn(v_ref[0, rows, :], ke_bd)

    def weighted_values(j, a, operands):
        rows = _aligned_rows(j * GLA_BLK, GLA_BLK)
        operands[4][rows, :] = _dot(a, v_ref[0, rows, :])

    def scan(j, state, operands, forward):
        qt_ref, _, _, d_ref, o_scr, kv_scr = operands
        o_inter = [None] * CHUNKS_PER_BLK
        for c in (range(CHUNKS_PER_BLK) if forward else reversed(range(CHUNKS_PER_BLK))):
            rows = _aligned_rows(j * GLA_BLK + c * C, C)
            o_inter[c] = _dot_nt(qt_ref[0, rows, :], state.astype(BF16))
            state = state * d_ref[0, pl.ds(j * CHUNKS_PER_BLK + c, 1), :] + kv_scr[j, :, c * hk:(c + 1) * hk]
        o_scr[_aligned_rows(j * GLA_BLK, GLA_BLK), :] += jnp.concatenate(o_inter, axis=0)
        return state

    def finish(j):
        rows = _aligned_rows(j * GLA_BLK, GLA_BLK)
        o = of_scr[rows, :] + ob_scr[rows, :]
        on = o * lax.rsqrt(jnp.mean(o * o, axis=-1, keepdims=True) + EPS) * gng_ref[0]
        o_ref[0, rows, :] = (on * gz_ref[0, rows, :].astype(F32)).astype(BF16)

    sf = sb = jnp.zeros((GLA_HV, hk), F32)
    for i in range(-1, n_blk):
        jf, jb = i, n_blk - 1 - i
        prepare = i + 1 < n_blk
        if prepare:
            a_f = scores(jf + 1, fwd, True)
            a_b = scores(jb - 1, bwd, False)
            chunk_states(jf + 1, fwd)
            chunk_states(jb - 1, bwd)
        if i >= 0:
            sf = scan(jf, sf, fwd, True)
            sb = scan(jb, sb, bwd, False)
        if prepare:
            weighted_values(jf + 1, a_f, fwd)
            weighted_values(jb - 1, a_b, bwd)
        for j in sorted({jf, jb}):
            if i >= 0 and max(j, n_blk - 1 - j) == i:
                finish(j)


def _gla(gl3, v3, d3, gz3, gng):
    B, S, D = v3.shape
    H, hk, hv = GLA_HEADS, GLA_HK, GLA_HV
    n_chunks = S // GLA_CHUNK
    assert S % GLA_BLK == 0
    operand = lambda g: pl.BlockSpec((1, S, hk), lambda b, h: (b, 0, g * H + h))
    decay = lambda g: pl.BlockSpec((1, n_chunks, hk), lambda b, h: (b, 0, g * H + h))
    head_v = pl.BlockSpec((1, S, hv), lambda b, h: (b, 0, h))
    kv_scratch = pltpu.VMEM((S // GLA_BLK, hv, CHUNKS_PER_BLK * hk), F32)
    return pl.pallas_call(
        _gla_kernel,
        grid=(B, H),
        in_specs=[operand(g) for g in range(N_GLA_OPERANDS)] + [
            head_v, decay(0), decay(1), head_v,
            pl.BlockSpec((1, 1, hv), lambda b, h: (h, 0, 0)),
        ],
        out_specs=head_v,
        out_shape=jax.ShapeDtypeStruct((B, S, D), BF16),
        scratch_shapes=[pltpu.VMEM((S, hv), F32), pltpu.VMEM((S, hv), F32), kv_scratch, kv_scratch],
        compiler_params=pltpu.CompilerParams(dimension_semantics=("arbitrary", "arbitrary"),
                                             vmem_limit_bytes=VMEM_LIMIT_BYTES),
        name="gla_bidir",
    )(*([gl3] * N_GLA_OPERANDS), v3, d3, d3, gz3, gng)


def _out_kernel(x_ref, g1_ref, sgl_ref, mg_ref, mod_ref, wbr_ref, wout_ref, fg_ref, o_ref):
    y_gla = _dot(g1_ref[...], wbr_ref[...])
    merged = (sgl_ref[...].astype(F32) * y_gla + mg_ref[...].astype(F32)).astype(BF16)
    gate = mod_ref[0, 2:3, :]
    r = x_ref[...] + gate * _dot(merged, wout_ref[...])
    o_ref[...] = (r * lax.rsqrt(jnp.mean(r * r, axis=-1, keepdims=True) + EPS)) * fg_ref[...]


def _merge_out(x2, g1, sgl, mg, mod3, w_br_gla, w_out, final_g, seq):
    T, D = x2.shape
    tm = OUT_TM
    assert T % tm == 0 and seq % tm == 0
    resident = functools.partial(pl.BlockSpec, pipeline_mode=pl.Buffered(1))
    tok = pl.BlockSpec((tm, D), lambda i: (i, 0))
    tiles_per_seq = seq // tm
    return pl.pallas_call(
        _out_kernel,
        grid=(T // tm,),
        in_specs=[
            tok, tok, tok, tok,
            pl.BlockSpec((1, 3, D), lambda i: (i // tiles_per_seq, 0, 0)),
            resident((D, D), lambda i: (0, 0)),
            resident((D, D), lambda i: (0, 0)),
            resident((1, D), lambda i: (0, 0)),
        ],
        out_specs=tok,
        out_shape=jax.ShapeDtypeStruct((T, D), F32),
        compiler_params=pltpu.CompilerParams(dimension_semantics=("arbitrary",),
                                             vmem_limit_bytes=VMEM_LIMIT_BYTES),
        name="merge_out",
    )(x2, g1, sgl, mg, mod3, w_br_gla, w_out, final_g)


def _layer(x, c, norm_g, w_ada, b_ada, w_in, alpha_fw_w, alpha_fw_b, alpha_bw_w, alpha_bw_b,
           gla_norm_g, gmlp_ln_g, gmlp_ln_b, gmlp_ws, gmlp_bs, w_br_gla, w_br_gmlp, w_out, out_g):
    B, S, D = x.shape
    H, hv, R, DK = GLA_HEADS, GLA_HV, GLA_RANK, GLA_DK
    T = B * S

    o_ra = 2 * DK + 2 * D
    assert o_ra == N_LO_GROUPS * D
    w_lo = w_in[:, :o_ra].astype(BF16)
    w_hi = w_in[:, o_ra + 2 * R:].astype(BF16)
    w_ra = jnp.pad(w_in[:, o_ra:o_ra + 2 * R], ((0, 0), (0, RA_PAD - 2 * R))).astype(BF16)
    zeros = jnp.zeros_like(alpha_fw_w)
    wa = jnp.concatenate([jnp.concatenate([alpha_fw_w, zeros], axis=1),
                          jnp.concatenate([zeros, alpha_bw_w], axis=1)], axis=0)
    wa = jnp.pad(wa, ((0, RA_PAD - 2 * R), (0, 0))).astype(BF16)
    ba = jnp.concatenate([alpha_fw_b, alpha_bw_b]).reshape(1, 2 * DK)
    bsm = jnp.repeat(gmlp_bs.T, GMLP_GC, axis=1)

    mod3 = _adaln_mod(c, w_ada, b_ada).reshape(B, 3, D)
    x2 = x.reshape(T, D)
    gl, d, v, gz, sgl, mg = _projection(
        x2, mod3, norm_g.reshape(1, D), w_lo, w_hi, w_ra, wa, ba, gmlp_ln_g.reshape(1, D), gmlp_ln_b.reshape(1, D),
        gmlp_ws.astype(BF16), bsm, w_br_gmlp.astype(BF16), S)
    g1 = _gla(gl.reshape(B, S, N_GLA_OPERANDS * DK), v.reshape(B, S, D), d.reshape(B, S // GLA_CHUNK, 2 * DK),
              gz.reshape(B, S, D), gla_norm_g.reshape(H, 1, hv))
    out = _merge_out(x2, g1.reshape(T, D), sgl, mg, mod3, w_br_gla.astype(BF16), w_out.astype(BF16),
                     out_g.reshape(1, D), S)
    return out.reshape(B, S, D)


def kernel(x, c, norm_g, w_ada, b_ada, w_in, alpha_fw_w, alpha_fw_b, alpha_bw_w, alpha_bw_b, gla_norm_g,
           gmlp_ln_g, gmlp_ln_b, gmlp_ws, gmlp_bs, w_br_gla, w_br_gmlp, w_out, final_g):
    depth = norm_g.shape[0]
    assert depth == 1, "the final rmsnorm is fused into the layer's output kernel"
    return _layer(x, c, norm_g[0], w_ada[0], b_ada[0], w_in[0], alpha_fw_w[0], alpha_fw_b[0], alpha_bw_w[0],
                  alpha_bw_b[0], gla_norm_g[0], gmlp_ln_g[0], gmlp_ln_b[0], gmlp_ws[0], gmlp_bs[0],
                  w_br_gla[0], w_br_gmlp[0], w_out[0], final_g)
```

```python
import functools

import jax
import jax.numpy as jnp
from jax import lax
from jax.experimental import pallas as pl
from jax.experimental.pallas import tpu as pltpu

F32 = jnp.float32
BF16 = jnp.bfloat16

D_MODEL = 1024
GLA_HEADS = 4
GLA_HK = 128
GLA_HV = 256
GLA_DK = GLA_HEADS * GLA_HK
GLA_RANK = 16
GLA_TAU = 16.0
GLA_CHUNK = 64
LOG_DECAY_FLOOR = -1.25
GMLP_GROUPS = 8
GMLP_GC = 128
GMLP_CHUNK = 128
EPS = 1e-6
SQRT_HALF = 0.7071067811865476
LOG2_E = 1.4426950408889634

RA_PAD = 128
MXU_K = 256
N_LO_GROUPS = 3
VMEM_LIMIT_BYTES = 56 * 1024 * 1024

PROJ_TM = 512
OUT_TM = 1024
OUT_SUB = 256
GLA_BLK = 256
CHUNKS_PER_BLK = GLA_BLK // GLA_CHUNK
CHUNK_SHIFT = GLA_CHUNK.bit_length() - 1
N_GLA_OPERANDS = 6


def _silu(z):
    return z * jax.nn.sigmoid(z)


def _gelu_exact(z):
    return 0.5 * z * (1.0 + lax.erf(z * SQRT_HALF))


def _log_sigmoid(z):
    return jnp.minimum(z, 0.0) - jnp.log1p(jnp.exp(-jnp.abs(z)))


def _dot(a, b):
    return jnp.dot(a, b, preferred_element_type=F32)


def _dot_nt(a, b):
    return lax.dot_general(a, b, (((1,), (1,)), ((), ())), preferred_element_type=F32)


def _dot_tn(a, b):
    return lax.dot_general(a, b, (((0,), (0,)), ((), ())), preferred_element_type=F32)


def _aligned_rows(start, size):
    if isinstance(start, int):
        return pl.ds(start, size)
    return pl.ds(pl.multiple_of(start, size), size)


def _block_iotas():
    ti = lax.broadcasted_iota(jnp.int32, (GLA_BLK, GLA_BLK), 0)
    si = lax.broadcasted_iota(jnp.int32, (GLA_BLK, GLA_BLK), 1)
    return ti, si


def _mod_kernel(c_ref, w_ref, b_ref, o_ref):
    a = _silu(c_ref[...]).astype(BF16)
    o_ref[...] = _dot(a, w_ref[...].astype(BF16)) + b_ref[...]


def _adaln_mod(c, w_ada, b_ada):
    B, D = c.shape
    n_out = w_ada.shape[1]
    return pl.pallas_call(
        _mod_kernel,
        grid=(n_out // D,),
        in_specs=[
            pl.BlockSpec((B, D), lambda j: (0, 0)),
            pl.BlockSpec((D, D), lambda j: (0, j)),
            pl.BlockSpec((1, D), lambda j: (0, j)),
        ],
        out_specs=pl.BlockSpec((B, D), lambda j: (0, j)),
        out_shape=jax.ShapeDtypeStruct((B, n_out), F32),
        compiler_params=pltpu.CompilerParams(dimension_semantics=("arbitrary",)),
        name="adaln_mod",
    )(c, w_ada, b_ada.reshape(1, n_out))


def _proj_kernel(x_ref, mod_ref, ng_ref, wlo_ref, whi_ref, wra_ref, wa_ref, ba_ref, lng_ref, lnb_ref, ws_ref, bsm_ref,
                 wbg_ref, gl_ref, d_ref, v_ref, gz_ref, sgl_ref, mg_ref, sg_scr):
    D = D_MODEL
    DK = GLA_DK
    C = GLA_CHUNK
    tm = x_ref.shape[0]
    half = tm // 2
    shift = mod_ref[0, 0:1, :]
    gain = ng_ref[...] * (1.0 + mod_ref[0, 1:2, :])

    def normed(rows):
        x = x_ref[rows, :]
        r = lax.rsqrt(jnp.mean(x * x, axis=-1, keepdims=True) + EPS)
        return ((x * r) * gain + shift).astype(BF16)

    h_top = normed(slice(0, half))
    h_bot = normed(slice(half, tm))
    h = jnp.concatenate([h_top, h_bot], axis=0)

    def proj(j):
        if j < N_LO_GROUPS:
            return _dot(h, wlo_ref[:, j * D:(j + 1) * D])
        return _dot(h, whi_ref[:, (j - N_LO_GROUPS) * D:(j - N_LO_GROUPS + 1) * D])


    qk = jnp.concatenate([_dot(h_top, wlo_ref[:, :D]), _dot(h_bot, wlo_ref[:, :D])], axis=0)
    ra = jnp.concatenate([_dot(h_top, wra_ref[...]), _dot(h_bot, wra_ref[...])], axis=0).astype(BF16)
    pre = _dot(ra, wa_ref[...]) + ba_ref[...]
    la = jnp.maximum(_log_sigmoid(pre) * (1.0 / GLA_TAU), LOG_DECAY_FLOOR) * LOG2_E

    v_ref[...] = proj(1).astype(BF16)

    hi = la.astype(BF16)
    r1 = la - hi.astype(F32)
    mid = r1.astype(BF16)
    lo = (r1 - mid.astype(F32)).astype(BF16)

    u_pre = proj(3)

    ti, si = _block_iotas()
    same_chunk = (ti >> CHUNK_SHIFT) == (si >> CHUNK_SHIFT)
    tri_f = jnp.where(same_chunk & (si <= ti), 1.0, 0.0).astype(BF16)
    tri_b = jnp.where(same_chunk & (si >= ti), 1.0, 0.0).astype(BF16)
    cum = []
    for blk in range(tm // GLA_BLK):
        rs = slice(blk * GLA_BLK, (blk + 1) * GLA_BLK)
        b_f = _dot(tri_f, hi[rs, :DK]) + _dot(tri_f, mid[rs, :DK]) + _dot(tri_f, lo[rs, :DK])
        b_b = _dot(tri_b, hi[rs, DK:]) + _dot(tri_b, mid[rs, DK:]) + _dot(tri_b, lo[rs, DK:])
        cum.append((b_f, b_b))

    sgl_ref[...] = jax.nn.sigmoid(proj(6)).astype(BF16)

    q_scale = GLA_HK ** -0.5
    for blk in range(tm // GLA_BLK):
        b_f, b_b = cum[blk]
        for c in range(CHUNKS_PER_BLK):
            cs = slice(c * C, (c + 1) * C)
            gr = slice(blk * GLA_BLK + c * C, blk * GLA_BLK + (c + 1) * C)
            n = blk * CHUNKS_PER_BLK + c
            q = qk[gr, :DK] * q_scale
            k = qk[gr, DK:]
            bf = b_f[cs]
            bb = b_b[cs]
            bf_last = bf[C - 1:C]
            bb_last = bb[0:1]
            gl_ref[gr, 0 * DK:1 * DK] = (q * jnp.exp2(bf)).astype(BF16)
            gl_ref[gr, 1 * DK:2 * DK] = (k * jnp.exp2(-bf)).astype(BF16)
            gl_ref[gr, 2 * DK:3 * DK] = (k * jnp.exp2(bf_last - bf)).astype(BF16)
            gl_ref[gr, 3 * DK:4 * DK] = (q * jnp.exp2(bb)).astype(BF16)
            gl_ref[gr, 4 * DK:5 * DK] = (k * jnp.exp2(-bb)).astype(BF16)
            gl_ref[gr, 5 * DK:6 * DK] = (k * jnp.exp2(bb_last - bb)).astype(BF16)
            d_ref[n:n + 1, :DK] = jnp.exp2(bf_last)
            d_ref[n:n + 1, DK:] = jnp.exp2(bb_last)

    zg = proj(5)

    vs = _gelu_exact(proj(4))
    mu = jnp.mean(vs, axis=-1, keepdims=True)
    vc = vs - mu
    var = jnp.mean(vc * vc, axis=-1, keepdims=True)
    vsn = ((vc * lax.rsqrt(var + EPS)) * lng_ref[...] + lnb_ref[...]).astype(BF16)

    gz_ref[...] = _silu(proj(2)).astype(BF16)

    n_pos = tm // GMLP_CHUNK
    for g in range(GMLP_GROUPS):
        cs = slice(g * GMLP_GC, (g + 1) * GMLP_GC)
        rhs = jnp.concatenate([vsn[n * GMLP_CHUNK:(n + 1) * GMLP_CHUNK, cs] for n in range(n_pos)], axis=1)
        mixed = _dot(ws_ref[g], rhs)
        for n in range(n_pos):
            sg_scr[n * GMLP_CHUNK:(n + 1) * GMLP_CHUNK, cs] = (
                mixed[:, n * GMLP_GC:(n + 1) * GMLP_GC] + bsm_ref[:, cs])
    merge_gate = jax.nn.sigmoid(proj(7))

    y_gmlp = None
    for kt in range(D // MXU_K):
        ks = slice(kt * MXU_K, (kt + 1) * MXU_K)
        g2 = ((_gelu_exact(u_pre[:, ks]) * sg_scr[:, ks]) * _silu(zg[:, ks])).astype(BF16)
        part = _dot(g2, wbg_ref[ks, :])
        y_gmlp = part if y_gmlp is None else y_gmlp + part
    mg_ref[...] = (merge_gate * y_gmlp).astype(BF16)


def _projection(x2, mod3, norm_g, w_lo, w_hi, w_ra, wa, ba, ln_g, ln_b, ws, bsm, w_br_gmlp, seq):
    T, D = x2.shape
    tm = PROJ_TM
    assert T % tm == 0 and seq % tm == 0 and tm % GMLP_CHUNK == 0 and tm % GLA_BLK == 0
    resident = functools.partial(pl.BlockSpec, pipeline_mode=pl.Buffered(1))
    tok = lambda w: pl.BlockSpec((tm, w), lambda i: (i, 0))
    tiles_per_seq = seq // tm
    out_bf = jax.ShapeDtypeStruct((T, D), BF16)
    return pl.pallas_call(
        _proj_kernel,
        grid=(T // tm,),
        in_specs=[
            tok(D),
            pl.BlockSpec((1, 3, D), lambda i: (i // tiles_per_seq, 0, 0)),
            resident((1, D), lambda i: (0, 0)),
            resident(w_lo.shape, lambda i: (0, 0)),
            resident(w_hi.shape, lambda i: (0, 0)),
            resident(w_ra.shape, lambda i: (0, 0)),
            resident(wa.shape, lambda i: (0, 0)),
            resident(ba.shape, lambda i: (0, 0)),
            resident((1, D), lambda i: (0, 0)),
            resident((1, D), lambda i: (0, 0)),
            resident(ws.shape, lambda i: (0, 0, 0)),
            resident(bsm.shape, lambda i: (0, 0)),
            resident(w_br_gmlp.shape, lambda i: (0, 0)),
        ],
        out_specs=[tok(N_GLA_OPERANDS * GLA_DK), pl.BlockSpec((tm // GLA_CHUNK, 2 * GLA_DK), lambda i: (i, 0)),
                   tok(D), tok(D), tok(D), tok(D)],
        out_shape=[jax.ShapeDtypeStruct((T, N_GLA_OPERANDS * GLA_DK), BF16),
                   jax.ShapeDtypeStruct((T // GLA_CHUNK, 2 * GLA_DK), F32),
                   out_bf, out_bf, out_bf, out_bf],
        scratch_shapes=[pltpu.VMEM((tm, D), F32)],
        compiler_params=pltpu.CompilerParams(dimension_semantics=("arbitrary",),
                                             vmem_limit_bytes=VMEM_LIMIT_BYTES),
        name="projection_gmlp",
    )(x2, mod3, norm_g, w_lo, w_hi, w_ra, wa, ba, ln_g, ln_b, ws, bsm, w_br_gmlp)


def _gla_kernel(qtf_ref, ktf_ref, kef_ref, qtb_ref, ktb_ref, keb_ref, v_ref, df_ref, db_ref, gz_ref, gng_ref,
                o_ref, of_scr, ob_scr, kvf_scr, kvb_scr):
    S = v_ref.shape[1]
    C = GLA_CHUNK
    hk = GLA_HK
    n_blk = S // GLA_BLK
    ti, si = _block_iotas()
    chunk_lo = (ti >> CHUNK_SHIFT) << CHUNK_SHIFT
    chunk_hi = chunk_lo + (C - 1)
    pos_chunk = lax.broadcasted_iota(jnp.int32, (hk, GLA_BLK), 1) >> CHUNK_SHIFT
    chunk_sel = [jnp.where(pos_chunk == c, 1.0, 0.0).astype(BF16) for c in range(CHUNKS_PER_BLK)]
    fwd = (qtf_ref, ktf_ref, kef_ref, df_ref, of_scr, kvf_scr)
    bwd = (qtb_ref, ktb_ref, keb_ref, db_ref, ob_scr, kvb_scr)

    def scores(j, operands, forward):
        qt_ref, kt_ref, _, _, _, _ = operands
        rows = _aligned_rows(j * GLA_BLK, GLA_BLK)
        a = _dot_nt(qt_ref[0, rows, :], kt_ref[0, rows, :])
        if forward:
            a = jnp.where(si <= ti, jnp.where(si >= chunk_lo, a, 0.0), 0.0)
        else:
            a = jnp.where(si > ti, jnp.where(si <= chunk_hi, a, 0.0), 0.0)
        return a.astype(BF16)

    def chunk_states(j, operands):
        _, _, ke_ref, _, _, kv_scr = operands
        rows = _aligned_rows(j * GLA_BLK, GLA_BLK)
        ke = ke_ref[0, rows, :]
        ke_t = ke.T
        ke_bd = jnp.concatenate([ke_t * chunk_sel[c] for c in range(CHUNKS_PER_BLK)], axis=0)
        kv_scr[j] = _dot(ke_bd, v_ref[0, rows, :])

    def weighted_values(j, a, operands):
        rows = _aligned_rows(j * GLA_BLK, GLA_BLK)
        operands[4][rows, :] = _dot(a, v_ref[0, rows, :])

    def scan(j, state, operands, forward):
        qt_ref, _, _, d_ref, o_scr, kv_scr = operands
        o_inter = [None] * CHUNKS_PER_BLK
        for c in (range(CHUNKS_PER_BLK) if forward else reversed(range(CHUNKS_PER_BLK))):
            rows = _aligned_rows(j * GLA_BLK + c * C, C)
            n = j * CHUNKS_PER_BLK + c
            o_inter[c] = _dot(qt_ref[0, rows, :], state.astype(BF16))
            state = state * d_ref[0, :, n:n + 1] + kv_scr[j, c * hk:(c + 1) * hk, :]
        o_scr[_aligned_rows(j * GLA_BLK, GLA_BLK), :] += jnp.concatenate(o_inter, axis=0)
        return state

    def finish(j):
        rows = _aligned_rows(j * GLA_BLK, GLA_BLK)
        o = of_scr[rows, :] + ob_scr[rows, :]
        on = o * lax.rsqrt(jnp.mean(o * o, axis=-1, keepdims=True) + EPS) * gng_ref[0]
        o_ref[0, rows, :] = (on * gz_ref[0, rows, :].astype(F32)).astype(BF16)

    sf = sb = jnp.zeros((hk, GLA_HV), F32)
    for i in range(-1, n_blk):
        jf, jb = i, n_blk - 1 - i
        prepare = i + 1 < n_blk
        if prepare:
            a_f = scores(jf + 1, fwd, True)
            a_b = scores(jb - 1, bwd, False)
        if i >= 0:
            sf = scan(jf, sf, fwd, True)
            sb = scan(jb, sb, bwd, False)
        if prepare:
            chunk_states(jf + 1, fwd)
            chunk_states(jb - 1, bwd)
            weighted_values(jf + 1, a_f, fwd)
            weighted_values(jb - 1, a_b, bwd)
        for j in sorted({jf, jb}):
            if i >= 0 and max(j, n_blk - 1 - j) == i:
                finish(j)


def _gla(gl3, v3, d3, gz3, gng):
    B, S, D = v3.shape
    H, hk, hv = GLA_HEADS, GLA_HK, GLA_HV
    n_chunks = S // GLA_CHUNK
    assert S % GLA_BLK == 0
    operand = lambda g: pl.BlockSpec((1, S, hk), lambda b, h: (b, 0, g * H + h))
    decay = lambda g: pl.BlockSpec((1, hk, n_chunks), lambda b, h: (b, g * H + h, 0))
    head_v = pl.BlockSpec((1, S, hv), lambda b, h: (b, 0, h))
    kv_scratch = pltpu.VMEM((S // GLA_BLK, CHUNKS_PER_BLK * hk, hv), F32)
    return pl.pallas_call(
        _gla_kernel,
        grid=(B, H),
        in_specs=[operand(g) for g in range(N_GLA_OPERANDS)] + [
            head_v, decay(0), decay(1), head_v,
            pl.BlockSpec((1, 1, hv), lambda b, h: (h, 0, 0)),
        ],
        out_specs=head_v,
        out_shape=jax.ShapeDtypeStruct((B, S, D), BF16),
        scratch_shapes=[pltpu.VMEM((S, hv), F32), pltpu.VMEM((S, hv), F32), kv_scratch, kv_scratch],
        compiler_params=pltpu.CompilerParams(dimension_semantics=("arbitrary", "arbitrary"),
                                             vmem_limit_bytes=VMEM_LIMIT_BYTES),
        name="gla_bidir",
    )(*([gl3] * N_GLA_OPERANDS), v3, d3, d3, gz3, gng)


def _out_kernel(x_ref, g1_ref, sgl_ref, mg_ref, mod_ref, wbr_ref, wout_ref, fg_ref, o_ref):
    gate = mod_ref[0, 2:3, :]
    n_sub = x_ref.shape[0] // OUT_SUB
    subs = [slice(s * OUT_SUB, (s + 1) * OUT_SUB) for s in range(n_sub)]
    y_gla = [None] * n_sub
    y_gla[0] = _dot(g1_ref[subs[0], :], wbr_ref[...])
    for s, rows in enumerate(subs):
        if s + 1 < n_sub:
            y_gla[s + 1] = _dot(g1_ref[subs[s + 1], :], wbr_ref[...])
        merged = (sgl_ref[rows, :].astype(F32) * y_gla[s] + mg_ref[rows, :].astype(F32)).astype(BF16)
        r = x_ref[rows, :] + gate * _dot(merged, wout_ref[...])
        o_ref[rows, :] = (r * lax.rsqrt(jnp.mean(r * r, axis=-1, keepdims=True) + EPS)) * fg_ref[...]


def _merge_out(x2, g1, sgl, mg, mod3, w_br_gla, w_out, final_g, seq):
    T, D = x2.shape
    tm = OUT_TM
    assert T % tm == 0 and seq % tm == 0
    resident = functools.partial(pl.BlockSpec, pipeline_mode=pl.Buffered(1))
    tok = pl.BlockSpec((tm, D), lambda i: (i, 0))
    tiles_per_seq = seq // tm
    return pl.pallas_call(
        _out_kernel,
        grid=(T // tm,),
        in_specs=[
            tok, tok, tok, tok,
            pl.BlockSpec((1, 3, D), lambda i: (i // tiles_per_seq, 0, 0)),
            resident((D, D), lambda i: (0, 0)),
            resident((D, D), lambda i: (0, 0)),
            resident((1, D), lambda i: (0, 0)),
        ],
        out_specs=tok,
        out_shape=jax.ShapeDtypeStruct((T, D), F32),
        compiler_params=pltpu.CompilerParams(dimension_semantics=("arbitrary",),
                                             vmem_limit_bytes=VMEM_LIMIT_BYTES),
        name="merge_out",
    )(x2, g1, sgl, mg, mod3, w_br_gla, w_out, final_g)


def _layer(x, c, norm_g, w_ada, b_ada, w_in, alpha_fw_w, alpha_fw_b, alpha_bw_w, alpha_bw_b,
           gla_norm_g, gmlp_ln_g, gmlp_ln_b, gmlp_ws, gmlp_bs, w_br_gla, w_br_gmlp, w_out, out_g):
    B, S, D = x.shape
    H, hv, R, DK = GLA_HEADS, GLA_HV, GLA_RANK, GLA_DK
    T = B * S

    o_ra = 2 * DK + 2 * D
    assert o_ra == N_LO_GROUPS * D
    w_lo = w_in[:, :o_ra].astype(BF16)
    w_hi = w_in[:, o_ra + 2 * R:].astype(BF16)
    w_ra = jnp.pad(w_in[:, o_ra:o_ra + 2 * R], ((0, 0), (0, RA_PAD - 2 * R))).astype(BF16)
    zeros = jnp.zeros_like(alpha_fw_w)
    wa = jnp.concatenate([jnp.concatenate([alpha_fw_w, zeros], axis=1),
                          jnp.concatenate([zeros, alpha_bw_w], axis=1)], axis=0)
    wa = jnp.pad(wa, ((0, RA_PAD - 2 * R), (0, 0))).astype(BF16)
    ba = jnp.concatenate([alpha_fw_b, alpha_bw_b]).reshape(1, 2 * DK)
    bsm = jnp.repeat(gmlp_bs.T, GMLP_GC, axis=1)

    mod3 = _adaln_mod(c, w_ada, b_ada).reshape(B, 3, D)
    x2 = x.reshape(T, D)
    gl, d, v, gz, sgl, mg = _projection(
        x2, mod3, norm_g.reshape(1, D), w_lo, w_hi, w_ra, wa, ba, gmlp_ln_g.reshape(1, D), gmlp_ln_b.reshape(1, D),
        gmlp_ws.astype(BF16), bsm, w_br_gmlp.astype(BF16), S)
    d_cols = d.reshape(B, S // GLA_CHUNK, 2 * DK).transpose(0, 2, 1)
    g1 = _gla(gl.reshape(B, S, N_GLA_OPERANDS * DK), v.reshape(B, S, D), d_cols,
              gz.reshape(B, S, D), gla_norm_g.reshape(H, 1, hv))
    out = _merge_out(x2, g1.reshape(T, D), sgl, mg, mod3, w_br_gla.astype(BF16), w_out.astype(BF16),
                     out_g.reshape(1, D), S)
    return out.reshape(B, S, D)


def kernel(x, c, norm_g, w_ada, b_ada, w_in, alpha_fw_w, alpha_fw_b, alpha_bw_w, alpha_bw_b, gla_norm_g,
           gmlp_ln_g, gmlp_ln_b, gmlp_ws, gmlp_bs, w_br_gla, w_br_gmlp, w_out, final_g):
    depth = norm_g.shape[0]
    assert depth == 1, "the final rmsnorm is fused into the layer's output kernel"
    return _layer(x, c, norm_g[0], w_ada[0], b_ada[0], w_in[0], alpha_fw_w[0], alpha_fw_b[0], alpha_bw_w[0],
                  alpha_bw_b[0], gla_norm_g[0], gmlp_ln_g[0], gmlp_ln_b[0], gmlp_ws[0], gmlp_bs[0],
                  w_br_gla[0], w_br_gmlp[0], w_out[0], final_g)
```

```python
import functools

import jax
import jax.numpy as jnp
from jax import lax
from jax.experimental import pallas as pl
from jax.experimental.pallas import tpu as pltpu

F32 = jnp.float32
BF16 = jnp.bfloat16

D_MODEL = 1024
GLA_HEADS = 4
GLA_HK = 128
GLA_HV = 256
GLA_DK = GLA_HEADS * GLA_HK
GLA_RANK = 16
GLA_TAU = 16.0
GLA_CHUNK = 64
LOG_DECAY_FLOOR = -1.25
GMLP_GROUPS = 8
GMLP_GC = 128
GMLP_CHUNK = 128
EPS = 1e-6
SQRT_HALF = 0.7071067811865476
LOG2_E = 1.4426950408889634

RA_PAD = 128
MXU_K = 256
MXU_N = 256
N_LO_GROUPS = 3
VMEM_LIMIT_BYTES = 56 * 1024 * 1024

PROJ_TM = 512
OUT_TM = 1024
OUT_SUB = 256
GLA_BLK = 256
CHUNKS_PER_BLK = GLA_BLK // GLA_CHUNK
CHUNK_SHIFT = GLA_CHUNK.bit_length() - 1
N_GLA_OPERANDS = 6


def _silu(z):
    return z * jax.nn.sigmoid(z)


def _gelu_exact(z):
    return 0.5 * z * (1.0 + lax.erf(z * SQRT_HALF))


def _log_sigmoid(z):
    return jnp.minimum(z, 0.0) - jnp.log1p(jnp.exp(-jnp.abs(z)))


def _dot(a, b):
    return jnp.dot(a, b, preferred_element_type=F32)


def _dot_nt(a, b):
    return lax.dot_general(a, b, (((1,), (1,)), ((), ())), preferred_element_type=F32)


def _dot_tn(a, b):
    return lax.dot_general(a, b, (((0,), (0,)), ((), ())), preferred_element_type=F32)


def _column_tiles(w):
    k, n = w.shape
    return w.reshape(k, n // MXU_N, MXU_N).transpose(1, 0, 2)


def _dot_tiled(a, w_ref, first=0, count=None):
    count = w_ref.shape[0] - first if count is None else count
    return jnp.concatenate([_dot(a, w_ref[first + t]) for t in range(count)], axis=1)


def _aligned_rows(start, size):
    if isinstance(start, int):
        return pl.ds(start, size)
    return pl.ds(pl.multiple_of(start, size), size)


def _block_iotas():
    ti = lax.broadcasted_iota(jnp.int32, (GLA_BLK, GLA_BLK), 0)
    si = lax.broadcasted_iota(jnp.int32, (GLA_BLK, GLA_BLK), 1)
    return ti, si


def _mod_kernel(c_ref, w_ref, b_ref, o_ref):
    a = _silu(c_ref[...]).astype(BF16)
    o_ref[...] = _dot(a, w_ref[...].astype(BF16)) + b_ref[...]


def _adaln_mod(c, w_ada, b_ada):
    B, D = c.shape
    n_out = w_ada.shape[1]
    return pl.pallas_call(
        _mod_kernel,
        grid=(n_out // D,),
        in_specs=[
            pl.BlockSpec((B, D), lambda j: (0, 0)),
            pl.BlockSpec((D, D), lambda j: (0, j)),
            pl.BlockSpec((1, D), lambda j: (0, j)),
        ],
        out_specs=pl.BlockSpec((B, D), lambda j: (0, j)),
        out_shape=jax.ShapeDtypeStruct((B, n_out), F32),
        compiler_params=pltpu.CompilerParams(dimension_semantics=("arbitrary",)),
        name="adaln_mod",
    )(c, w_ada, b_ada.reshape(1, n_out))


def _proj_kernel(x_ref, mod_ref, ng_ref, wlo_ref, whi_ref, wra_ref, wa_ref, ba_ref, lng_ref, lnb_ref, ws_ref, bsm_ref,
                 wbg_ref, gl_ref, d_ref, v_ref, gz_ref, sgl_ref, mg_ref, sg_scr, h_scr):
    D = D_MODEL
    DK = GLA_DK
    C = GLA_CHUNK
    tm = x_ref.shape[0]
    half = tm // 2
    shift = mod_ref[0, 0:1, :]
    gain = ng_ref[...] * (1.0 + mod_ref[0, 1:2, :])

    def normed(rows):
        x = x_ref[rows, :]
        r = lax.rsqrt(jnp.mean(x * x, axis=-1, keepdims=True) + EPS)
        return ((x * r) * gain + shift).astype(BF16)

    h_scr[0:half, :] = normed(slice(0, half))
    h_scr[half:tm, :] = normed(slice(half, tm))
    h_top = h_scr[0:half, :]
    h_bot = h_scr[half:tm, :]

    tiles_per_group = D // MXU_N

    def proj(j, lhs=None):
        lhs = h_scr[...] if lhs is None else lhs
        if j < N_LO_GROUPS:
            return _dot_tiled(lhs, wlo_ref, j * tiles_per_group, tiles_per_group)
        return _dot_tiled(lhs, whi_ref, (j - N_LO_GROUPS) * tiles_per_group, tiles_per_group)


    qk = jnp.concatenate([proj(0, h_top), proj(0, h_bot)], axis=0)
    ra = jnp.concatenate([_dot(h_top, wra_ref[...]), _dot(h_bot, wra_ref[...])], axis=0).astype(BF16)
    pre = _dot_tiled(ra, wa_ref) + ba_ref[...]
    la = jnp.maximum(_log_sigmoid(pre) * (1.0 / GLA_TAU), LOG_DECAY_FLOOR) * LOG2_E

    v_ref[...] = proj(1).astype(BF16)

    hi = la.astype(BF16)
    r1 = la - hi.astype(F32)
    mid = r1.astype(BF16)
    lo = (r1 - mid.astype(F32)).astype(BF16)

    u_pre = proj(3)

    ti, si = _block_iotas()
    same_chunk = (ti >> CHUNK_SHIFT) == (si >> CHUNK_SHIFT)
    tri_f = jnp.where(same_chunk & (si <= ti), 1.0, 0.0).astype(BF16)
    tri_b = jnp.where(same_chunk & (si >= ti), 1.0, 0.0).astype(BF16)
    cum = []
    for blk in range(tm // GLA_BLK):
        rs = slice(blk * GLA_BLK, (blk + 1) * GLA_BLK)
        b_f = _dot(tri_f, hi[rs, :DK]) + _dot(tri_f, mid[rs, :DK]) + _dot(tri_f, lo[rs, :DK])
        b_b = _dot(tri_b, hi[rs, DK:]) + _dot(tri_b, mid[rs, DK:]) + _dot(tri_b, lo[rs, DK:])
        cum.append((b_f, b_b))

    sgl_ref[...] = jax.nn.sigmoid(proj(6)).astype(BF16)

    q_scale = GLA_HK ** -0.5
    for blk in range(tm // GLA_BLK):
        b_f, b_b = cum[blk]
        for c in range(CHUNKS_PER_BLK):
            cs = slice(c * C, (c + 1) * C)
            gr = slice(blk * GLA_BLK + c * C, blk * GLA_BLK + (c + 1) * C)
            n = blk * CHUNKS_PER_BLK + c
            q = qk[gr, :DK] * q_scale
            k = qk[gr, DK:]
            bf = b_f[cs]
            bb = b_b[cs]
            bf_last = bf[C - 1:C]
            bb_last = bb[0:1]
            gl_ref[gr, 0 * DK:1 * DK] = (q * jnp.exp2(bf)).astype(BF16)
            gl_ref[gr, 1 * DK:2 * DK] = (k * jnp.exp2(-bf)).astype(BF16)
            gl_ref[gr, 2 * DK:3 * DK] = (k * jnp.exp2(bf_last - bf)).astype(BF16)
            gl_ref[gr, 3 * DK:4 * DK] = (q * jnp.exp2(bb)).astype(BF16)
            gl_ref[gr, 4 * DK:5 * DK] = (k * jnp.exp2(-bb)).astype(BF16)
            gl_ref[gr, 5 * DK:6 * DK] = (k * jnp.exp2(bb_last - bb)).astype(BF16)
            d_ref[n:n + 1, :DK] = jnp.exp2(bf_last)
            d_ref[n:n + 1, DK:] = jnp.exp2(bb_last)

    zg = proj(5)

    vs = _gelu_exact(proj(4))
    mu = jnp.mean(vs, axis=-1, keepdims=True)
    vc = vs - mu
    var = jnp.mean(vc * vc, axis=-1, keepdims=True)
    vsn = ((vc * lax.rsqrt(var + EPS)) * lng_ref[...] + lnb_ref[...]).astype(BF16)

    gz_ref[...] = _silu(proj(2)).astype(BF16)

    n_pos = tm // GMLP_CHUNK
    for g in range(GMLP_GROUPS):
        cs = slice(g * GMLP_GC, (g + 1) * GMLP_GC)
        rhs = jnp.concatenate([vsn[n * GMLP_CHUNK:(n + 1) * GMLP_CHUNK, cs] for n in range(n_pos)], axis=1)
        mixed = _dot(ws_ref[g], rhs)
        for n in range(n_pos):
            sg_scr[n * GMLP_CHUNK:(n + 1) * GMLP_CHUNK, cs] = (
                mixed[:, n * GMLP_GC:(n + 1) * GMLP_GC] + bsm_ref[:, cs])
    merge_gate = jax.nn.sigmoid(proj(7))

    y_gmlp = None
    for kt in range(D // MXU_K):
        ks = slice(kt * MXU_K, (kt + 1) * MXU_K)
        g2 = ((_gelu_exact(u_pre[:, ks]) * sg_scr[:, ks]) * _silu(zg[:, ks])).astype(BF16)
        part = jnp.concatenate([_dot(g2, wbg_ref[t, ks, :]) for t in range(tiles_per_group)], axis=1)
        y_gmlp = part if y_gmlp is None else y_gmlp + part
    mg_ref[...] = (merge_gate * y_gmlp).astype(BF16)


def _projection(x2, mod3, norm_g, w_lo, w_hi, w_ra, wa, ba, ln_g, ln_b, ws, bsm, w_br_gmlp, seq):
    T, D = x2.shape
    tm = PROJ_TM
    assert T % tm == 0 and seq % tm == 0 and tm % GMLP_CHUNK == 0 and tm % GLA_BLK == 0
    resident = functools.partial(pl.BlockSpec, pipeline_mode=pl.Buffered(1))
    tok = lambda w: pl.BlockSpec((tm, w), lambda i: (i, 0))
    tiles_per_seq = seq // tm
    out_bf = jax.ShapeDtypeStruct((T, D), BF16)
    return pl.pallas_call(
        _proj_kernel,
        grid=(T // tm,),
        in_specs=[
            tok(D),
            pl.BlockSpec((1, 3, D), lambda i: (i // tiles_per_seq, 0, 0)),
            resident((1, D), lambda i: (0, 0)),
            resident(w_lo.shape, lambda i: (0, 0, 0)),
            resident(w_hi.shape, lambda i: (0, 0, 0)),
            resident(w_ra.shape, lambda i: (0, 0)),
            resident(wa.shape, lambda i: (0, 0, 0)),
            resident(ba.shape, lambda i: (0, 0)),
            resident((1, D), lambda i: (0, 0)),
            resident((1, D), lambda i: (0, 0)),
            resident(ws.shape, lambda i: (0, 0, 0)),
            resident(bsm.shape, lambda i: (0, 0)),
            resident(w_br_gmlp.shape, lambda i: (0, 0, 0)),
        ],
        out_specs=[tok(N_GLA_OPERANDS * GLA_DK), pl.BlockSpec((tm // GLA_CHUNK, 2 * GLA_DK), lambda i: (i, 0)),
                   tok(D), tok(D), tok(D), tok(D)],
        out_shape=[jax.ShapeDtypeStruct((T, N_GLA_OPERANDS * GLA_DK), BF16),
                   jax.ShapeDtypeStruct((T // GLA_CHUNK, 2 * GLA_DK), F32),
                   out_bf, out_bf, out_bf, out_bf],
        scratch_shapes=[pltpu.VMEM((tm, D), F32), pltpu.VMEM((tm, D), BF16)],
        compiler_params=pltpu.CompilerParams(dimension_semantics=("arbitrary",),
                                             vmem_limit_bytes=VMEM_LIMIT_BYTES),
        name="projection_gmlp",
    )(x2, mod3, norm_g, w_lo, w_hi, w_ra, wa, ba, ln_g, ln_b, ws, bsm, w_br_gmlp)


def _gla_kernel(qtf_ref, ktf_ref, kef_ref, qtb_ref, ktb_ref, keb_ref, v_ref, df_ref, db_ref, gz_ref, gng_ref,
                o_ref, of_scr, ob_scr, kvf_scr, kvb_scr):
    S = v_ref.shape[1]
    C = GLA_CHUNK
    hk = GLA_HK
    n_blk = S // GLA_BLK
    ti, si = _block_iotas()
    chunk_lo = (ti >> CHUNK_SHIFT) << CHUNK_SHIFT
    chunk_hi = chunk_lo + (C - 1)
    pos_chunk = lax.broadcasted_iota(jnp.int32, (hk, GLA_BLK), 1) >> CHUNK_SHIFT
    chunk_sel = [jnp.where(pos_chunk == c, 1.0, 0.0).astype(BF16) for c in range(CHUNKS_PER_BLK)]
    fwd = (qtf_ref, ktf_ref, kef_ref, df_ref, of_scr, kvf_scr)
    bwd = (qtb_ref, ktb_ref, keb_ref, db_ref, ob_scr, kvb_scr)

    def scores(j, operands, forward):
        qt_ref, kt_ref, _, _, _, _ = operands
        rows = _aligned_rows(j * GLA_BLK, GLA_BLK)
        a = _dot_nt(qt_ref[0, rows, :], kt_ref[0, rows, :])
        if forward:
            a = jnp.where(si <= ti, jnp.where(si >= chunk_lo, a, 0.0), 0.0)
        else:
            a = jnp.where(si > ti, jnp.where(si <= chunk_hi, a, 0.0), 0.0)
        return a.astype(BF16)

    def chunk_states(j, operands):
        _, _, ke_ref, _, _, kv_scr = operands
        rows = _aligned_rows(j * GLA_BLK, GLA_BLK)
        ke = ke_ref[0, rows, :]
        ke_t = ke.T
        ke_bd = jnp.concatenate([ke_t * chunk_sel[c] for c in range(CHUNKS_PER_BLK)], axis=0)
        kv_scr[j] = _dot(ke_bd, v_ref[0, rows, :])

    def weighted_values(j, a, operands):
        rows = _aligned_rows(j * GLA_BLK, GLA_BLK)
        operands[4][rows, :] = _dot(a, v_ref[0, rows, :])

    def scan(j, state, operands, forward):
        qt_ref, _, _, d_ref, o_scr, kv_scr = operands
        o_inter = [None] * CHUNKS_PER_BLK
        for c in (range(CHUNKS_PER_BLK) if forward else reversed(range(CHUNKS_PER_BLK))):
            rows = _aligned_rows(j * GLA_BLK + c * C, C)
            n = j * CHUNKS_PER_BLK + c
            o_inter[c] = _dot(qt_ref[0, rows, :], state.astype(BF16))
            state = state * d_ref[0, :, n:n + 1] + kv_scr[j, c * hk:(c + 1) * hk, :]
        o_scr[_aligned_rows(j * GLA_BLK, GLA_BLK), :] += jnp.concatenate(o_inter, axis=0)
        return state

    def finish(j):
        rows = _aligned_rows(j * GLA_BLK, GLA_BLK)
        o = of_scr[rows, :] + ob_scr[rows, :]
        on = o * lax.rsqrt(jnp.mean(o * o, axis=-1, keepdims=True) + EPS) * gng_ref[0]
        o_ref[0, rows, :] = (on * gz_ref[0, rows, :].astype(F32)).astype(BF16)

    sf = sb = jnp.zeros((hk, GLA_HV), F32)
    for i in range(-1, n_blk):
        jf, jb = i, n_blk - 1 - i
        prepare = i + 1 < n_blk
        if prepare:
            a_f = scores(jf + 1, fwd, True)
            a_b = scores(jb - 1, bwd, False)
        if i >= 0:
            sf = scan(jf, sf, fwd, True)
            sb = scan(jb, sb, bwd, False)
        if prepare:
            chunk_states(jf + 1, fwd)
            chunk_states(jb - 1, bwd)
            weighted_values(jf + 1, a_f, fwd)
            weighted_values(jb - 1, a_b, bwd)
        for j in sorted({jf, jb}):
            if i >= 0 and max(j, n_blk - 1 - j) == i:
                finish(j)


def _gla(gl3, v3, d3, gz3, gng):
    B, S, D = v3.shape
    H, hk, hv = GLA_HEADS, GLA_HK, GLA_HV
    n_chunks = S // GLA_CHUNK
    assert S % GLA_BLK == 0
    operand = lambda g: pl.BlockSpec((1, S, hk), lambda b, h: (b, 0, g * H + h))
    decay = lambda g: pl.BlockSpec((1, hk, n_chunks), lambda b, h: (b, g * H + h, 0))
    head_v = pl.BlockSpec((1, S, hv), lambda b, h: (b, 0, h))
    kv_scratch = pltpu.VMEM((S // GLA_BLK, CHUNKS_PER_BLK * hk, hv), F32)
    return pl.pallas_call(
        _gla_kernel,
        grid=(B, H),
        in_specs=[operand(g) for g in range(N_GLA_OPERANDS)] + [
            head_v, decay(0), decay(1), head_v,
            pl.BlockSpec((1, 1, hv), lambda b, h: (h, 0, 0)),
        ],
        out_specs=head_v,
        out_shape=jax.ShapeDtypeStruct((B, S, D), BF16),
        scratch_shapes=[pltpu.VMEM((S, hv), F32), pltpu.VMEM((S, hv), F32), kv_scratch, kv_scratch],
        compiler_params=pltpu.CompilerParams(dimension_semantics=("arbitrary", "arbitrary"),
                                             vmem_limit_bytes=VMEM_LIMIT_BYTES),
        name="gla_bidir",
    )(*([gl3] * N_GLA_OPERANDS), v3, d3, d3, gz3, gng)


def _out_kernel(x_ref, g1_ref, sgl_ref, mg_ref, mod_ref, wbr_ref, wout_ref, fg_ref, o_ref):
    gate = mod_ref[0, 2:3, :]
    n_sub = x_ref.shape[0] // OUT_SUB
    subs = [slice(s * OUT_SUB, (s + 1) * OUT_SUB) for s in range(n_sub)]
    y_gla = [None] * n_sub
    y_gla[0] = _dot_tiled(g1_ref[subs[0], :], wbr_ref)
    for s, rows in enumerate(subs):
        if s + 1 < n_sub:
            y_gla[s + 1] = _dot_tiled(g1_ref[subs[s + 1], :], wbr_ref)
        merged = (sgl_ref[rows, :].astype(F32) * y_gla[s] + mg_ref[rows, :].astype(F32)).astype(BF16)
        r = x_ref[rows, :] + gate * _dot_tiled(merged, wout_ref)
        o_ref[rows, :] = (r * lax.rsqrt(jnp.mean(r * r, axis=-1, keepdims=True) + EPS)) * fg_ref[...]


def _merge_out(x2, g1, sgl, mg, mod3, w_br_gla, w_out, final_g, seq):
    T, D = x2.shape
    tm = OUT_TM
    assert T % tm == 0 and seq % tm == 0
    resident = functools.partial(pl.BlockSpec, pipeline_mode=pl.Buffered(1))
    tok = pl.BlockSpec((tm, D), lambda i: (i, 0))
    tiles_per_seq = seq // tm
    return pl.pallas_call(
        _out_kernel,
        grid=(T // tm,),
        in_specs=[
            tok, tok, tok, tok,
            pl.BlockSpec((1, 3, D), lambda i: (i // tiles_per_seq, 0, 0)),
            resident(w_br_gla.shape, lambda i: (0, 0, 0)),
            resident(w_out.shape, lambda i: (0, 0, 0)),
            resident((1, D), lambda i: (0, 0)),
        ],
        out_specs=tok,
        out_shape=jax.ShapeDtypeStruct((T, D), F32),
        compiler_params=pltpu.CompilerParams(dimension_semantics=("arbitrary",),
                                             vmem_limit_bytes=VMEM_LIMIT_BYTES),
        name="merge_out",
    )(x2, g1, sgl, mg, mod3, w_br_gla, w_out, final_g)


def _layer(x, c, norm_g, w_ada, b_ada, w_in, alpha_fw_w, alpha_fw_b, alpha_bw_w, alpha_bw_b,
           gla_norm_g, gmlp_ln_g, gmlp_ln_b, gmlp_ws, gmlp_bs, w_br_gla, w_br_gmlp, w_out, out_g):
    B, S, D = x.shape
    H, hv, R, DK = GLA_HEADS, GLA_HV, GLA_RANK, GLA_DK
    T = B * S

    o_ra = 2 * DK + 2 * D
    assert o_ra == N_LO_GROUPS * D
    w_lo = _column_tiles(w_in[:, :o_ra].astype(BF16))
    w_hi = _column_tiles(w_in[:, o_ra + 2 * R:].astype(BF16))
    w_ra = jnp.pad(w_in[:, o_ra:o_ra + 2 * R], ((0, 0), (0, RA_PAD - 2 * R))).astype(BF16)
    zeros = jnp.zeros_like(alpha_fw_w)
    wa = jnp.concatenate([jnp.concatenate([alpha_fw_w, zeros], axis=1),
                          jnp.concatenate([zeros, alpha_bw_w], axis=1)], axis=0)
    wa = _column_tiles(jnp.pad(wa, ((0, RA_PAD - 2 * R), (0, 0))).astype(BF16))
    ba = jnp.concatenate([alpha_fw_b, alpha_bw_b]).reshape(1, 2 * DK)
    bsm = jnp.repeat(gmlp_bs.T, GMLP_GC, axis=1)

    mod3 = _adaln_mod(c, w_ada, b_ada).reshape(B, 3, D)
    x2 = x.reshape(T, D)
    gl, d, v, gz, sgl, mg = _projection(
        x2, mod3, norm_g.reshape(1, D), w_lo, w_hi, w_ra, wa, ba, gmlp_ln_g.reshape(1, D), gmlp_ln_b.reshape(1, D),
        gmlp_ws.astype(BF16), bsm, _column_tiles(w_br_gmlp.astype(BF16)), S)
    d_cols = d.reshape(B, S // GLA_CHUNK, 2 * DK).transpose(0, 2, 1)
    g1 = _gla(gl.reshape(B, S, N_GLA_OPERANDS * DK), v.reshape(B, S, D), d_cols,
              gz.reshape(B, S, D), gla_norm_g.reshape(H, 1, hv))
    out = _merge_out(x2, g1.reshape(T, D), sgl, mg, mod3, _column_tiles(w_br_gla.astype(BF16)),
                     _column_tiles(w_out.astype(BF16)),
                     out_g.reshape(1, D), S)
    return out.reshape(B, S, D)


def kernel(x, c, norm_g, w_ada, b_ada, w_in, alpha_fw_w, alpha_fw_b, alpha_bw_w, alpha_bw_b, gla_norm_g,
           gmlp_ln_g, gmlp_ln_b, gmlp_ws, gmlp_bs, w_br_gla, w_br_gmlp, w_out, final_g):
    depth = norm_g.shape[0]
    assert depth == 1, "the final rmsnorm is fused into the layer's output kernel"
    return _layer(x, c, norm_g[0], w_ada[0], b_ada[0], w_in[0], alpha_fw_w[0], alpha_fw_b[0], alpha_bw_w[0],
                  alpha_bw_b[0], gla_norm_g[0], gmlp_ln_g[0], gmlp_ln_b[0], gmlp_ws[0], gmlp_bs[0],
                  w_br_gla[0], w_br_gmlp[0], w_out[0], final_g)
```

```python
import functools

import jax
import jax.numpy as jnp
from jax import lax
from jax.experimental import pallas as pl
from jax.experimental.pallas import tpu as pltpu

F32 = jnp.float32
BF16 = jnp.bfloat16

D_MODEL = 1024
GLA_HEADS = 4
GLA_HK = 128
GLA_HV = 256
GLA_DK = GLA_HEADS * GLA_HK
GLA_RANK = 16
GLA_TAU = 16.0
GLA_CHUNK = 64
LOG_DECAY_FLOOR = -1.25
GMLP_GROUPS = 8
GMLP_GC = 128
GMLP_CHUNK = 128
EPS = 1e-6
SQRT_HALF = 0.7071067811865476
LOG2_E = 1.4426950408889634

RA_PAD = 128
MXU_K = 256
N_LO_GROUPS = 3
VMEM_LIMIT_BYTES = 56 * 1024 * 1024

PROJ_TM = 512
OUT_TM = 1024
OUT_SUB = 256
GLA_BLK = 256
CHUNKS_PER_BLK = GLA_BLK // GLA_CHUNK
CHUNK_SHIFT = GLA_CHUNK.bit_length() - 1
N_GLA_OPERANDS = 6


def _silu(z):
    return z * jax.nn.sigmoid(z)


def _gelu_exact(z):
    return 0.5 * z * (1.0 + lax.erf(z * SQRT_HALF))


def _log_sigmoid(z):
    return jnp.minimum(z, 0.0) - jnp.log1p(jnp.exp(-jnp.abs(z)))


def _dot(a, b):
    return jnp.dot(a, b, preferred_element_type=F32)


def _dot_nt(a, b):
    return lax.dot_general(a, b, (((1,), (1,)), ((), ())), preferred_element_type=F32)


def _dot_tn(a, b):
    return lax.dot_general(a, b, (((0,), (0,)), ((), ())), preferred_element_type=F32)


def _aligned_rows(start, size):
    if isinstance(start, int):
        return pl.ds(start, size)
    return pl.ds(pl.multiple_of(start, size), size)


def _block_iotas():
    ti = lax.broadcasted_iota(jnp.int32, (GLA_BLK, GLA_BLK), 0)
    si = lax.broadcasted_iota(jnp.int32, (GLA_BLK, GLA_BLK), 1)
    return ti, si


def _mod_kernel(c_ref, w_ref, b_ref, o_ref):
    a = _silu(c_ref[...]).astype(BF16)
    o_ref[...] = _dot(a, w_ref[...].astype(BF16)) + b_ref[...]


def _adaln_mod(c, w_ada, b_ada):
    B, D = c.shape
    n_out = w_ada.shape[1]
    return pl.pallas_call(
        _mod_kernel,
        grid=(n_out // D,),
        in_specs=[
            pl.BlockSpec((B, D), lambda j: (0, 0)),
            pl.BlockSpec((D, D), lambda j: (0, j)),
            pl.BlockSpec((1, D), lambda j: (0, j)),
        ],
        out_specs=pl.BlockSpec((B, D), lambda j: (0, j)),
        out_shape=jax.ShapeDtypeStruct((B, n_out), F32),
        compiler_params=pltpu.CompilerParams(dimension_semantics=("arbitrary",)),
        name="adaln_mod",
    )(c, w_ada, b_ada.reshape(1, n_out))


def _proj_kernel(x_ref, mod_ref, ng_ref, wlo_ref, whi_ref, wra_ref, wa_ref, ba_ref, lng_ref, lnb_ref, ws_ref, bsm_ref,
                 wbg_ref, gl_ref, d_ref, v_ref, gz_ref, sgl_ref, mg_ref, sg_scr):
    D = D_MODEL
    DK = GLA_DK
    hk = GLA_HK
    C = GLA_CHUNK
    tm = x_ref.shape[0]
    half = tm // 2
    shift = mod_ref[0, 0:1, :]
    gain = ng_ref[...] * (1.0 + mod_ref[0, 1:2, :])

    def normed(rows):
        x = x_ref[rows, :]
        r = lax.rsqrt(jnp.mean(x * x, axis=-1, keepdims=True) + EPS)
        return ((x * r) * gain + shift).astype(BF16)

    h_top = normed(slice(0, half))
    h_bot = normed(slice(half, tm))
    h = jnp.concatenate([h_top, h_bot], axis=0)

    def proj(j):
        if j < N_LO_GROUPS:
            return _dot(h, wlo_ref[:, j * D:(j + 1) * D])
        return _dot(h, whi_ref[:, (j - N_LO_GROUPS) * D:(j - N_LO_GROUPS + 1) * D])


    qk = jnp.concatenate([_dot(h_top, wlo_ref[:, :D]), _dot(h_bot, wlo_ref[:, :D])], axis=0)
    ra = jnp.concatenate([_dot(h_top, wra_ref[...]), _dot(h_bot, wra_ref[...])], axis=0).astype(BF16)
    pre = _dot(ra, wa_ref[...]) + ba_ref[...]
    la = jnp.maximum(_log_sigmoid(pre) * (1.0 / GLA_TAU), LOG_DECAY_FLOOR) * LOG2_E

    v_ref[...] = proj(1).astype(BF16)

    hi = la.astype(BF16)
    r1 = la - hi.astype(F32)
    mid = r1.astype(BF16)
    lo = (r1 - mid.astype(F32)).astype(BF16)

    u_pre = proj(3)

    ti, si = _block_iotas()
    same_chunk = (ti >> CHUNK_SHIFT) == (si >> CHUNK_SHIFT)
    tri_f = jnp.where(same_chunk & (si <= ti), 1.0, 0.0).astype(BF16)
    tri_b = jnp.where(same_chunk & (si >= ti), 1.0, 0.0).astype(BF16)
    cum = []
    for blk in range(tm // GLA_BLK):
        rs = slice(blk * GLA_BLK, (blk + 1) * GLA_BLK)
        b_f = _dot(tri_f, hi[rs, :DK]) + _dot(tri_f, mid[rs, :DK]) + _dot(tri_f, lo[rs, :DK])
        b_b = _dot(tri_b, hi[rs, DK:]) + _dot(tri_b, mid[rs, DK:]) + _dot(tri_b, lo[rs, DK:])
        cum.append((b_f, b_b))

    sgl_ref[...] = jax.nn.sigmoid(proj(6)).astype(BF16)

    q_scale = GLA_HK ** -0.5
    for blk in range(tm // GLA_BLK):
        b_f, b_b = cum[blk]
        for c in range(CHUNKS_PER_BLK):
            cs = slice(c * C, (c + 1) * C)
            gr = slice(blk * GLA_BLK + c * C, blk * GLA_BLK + (c + 1) * C)
            n = blk * CHUNKS_PER_BLK + c
            q = qk[gr, :DK] * q_scale
            k = qk[gr, DK:]
            bf = b_f[cs]
            bb = b_b[cs]
            bf_last = bf[C - 1:C]
            bb_last = bb[0:1]
            operands = (q * jnp.exp2(bf), k * jnp.exp2(-bf), k * jnp.exp2(bf_last - bf),
                        q * jnp.exp2(bb), k * jnp.exp2(-bb), k * jnp.exp2(bb_last - bb))
            decays = (jnp.exp2(bf_last), jnp.exp2(bb_last))
            for hd in range(GLA_HEADS):
                hs = slice(hd * hk, (hd + 1) * hk)
                for g, val in enumerate(operands):
                    lane0 = (hd * N_GLA_OPERANDS + g) * hk
                    gl_ref[gr, lane0:lane0 + hk] = val[:, hs].astype(BF16)
                for g, val in enumerate(decays):
                    lane0 = (hd * len(decays) + g) * hk
                    d_ref[n:n + 1, lane0:lane0 + hk] = val[:, hs]

    zg = proj(5)

    vs = _gelu_exact(proj(4))
    mu = jnp.mean(vs, axis=-1, keepdims=True)
    vc = vs - mu
    var = jnp.mean(vc * vc, axis=-1, keepdims=True)
    vsn = ((vc * lax.rsqrt(var + EPS)) * lng_ref[...] + lnb_ref[...]).astype(BF16)

    gz_ref[...] = _silu(proj(2)).astype(BF16)

    n_pos = tm // GMLP_CHUNK
    for g in range(GMLP_GROUPS):
        cs = slice(g * GMLP_GC, (g + 1) * GMLP_GC)
        rhs = jnp.concatenate([vsn[n * GMLP_CHUNK:(n + 1) * GMLP_CHUNK, cs] for n in range(n_pos)], axis=1)
        mixed = _dot(ws_ref[g], rhs)
        for n in range(n_pos):
            sg_scr[n * GMLP_CHUNK:(n + 1) * GMLP_CHUNK, cs] = (
                mixed[:, n * GMLP_GC:(n + 1) * GMLP_GC] + bsm_ref[:, cs])
    merge_gate = jax.nn.sigmoid(proj(7))

    y_gmlp = None
    for kt in range(D // MXU_K):
        ks = slice(kt * MXU_K, (kt + 1) * MXU_K)
        g2 = ((_gelu_exact(u_pre[:, ks]) * sg_scr[:, ks]) * _silu(zg[:, ks])).astype(BF16)
        part = _dot(g2, wbg_ref[ks, :])
        y_gmlp = part if y_gmlp is None else y_gmlp + part
    mg_ref[...] = (merge_gate * y_gmlp).astype(BF16)


def _projection(x2, mod3, norm_g, w_lo, w_hi, w_ra, wa, ba, ln_g, ln_b, ws, bsm, w_br_gmlp, seq):
    T, D = x2.shape
    tm = PROJ_TM
    assert T % tm == 0 and seq % tm == 0 and tm % GMLP_CHUNK == 0 and tm % GLA_BLK == 0
    resident = functools.partial(pl.BlockSpec, pipeline_mode=pl.Buffered(1))
    tok = lambda w: pl.BlockSpec((tm, w), lambda i: (i, 0))
    tiles_per_seq = seq // tm
    out_bf = jax.ShapeDtypeStruct((T, D), BF16)
    return pl.pallas_call(
        _proj_kernel,
        grid=(T // tm,),
        in_specs=[
            tok(D),
            pl.BlockSpec((1, 3, D), lambda i: (i // tiles_per_seq, 0, 0)),
            resident((1, D), lambda i: (0, 0)),
            resident(w_lo.shape, lambda i: (0, 0)),
            resident(w_hi.shape, lambda i: (0, 0)),
            resident(w_ra.shape, lambda i: (0, 0)),
            resident(wa.shape, lambda i: (0, 0)),
            resident(ba.shape, lambda i: (0, 0)),
            resident((1, D), lambda i: (0, 0)),
            resident((1, D), lambda i: (0, 0)),
            resident(ws.shape, lambda i: (0, 0, 0)),
            resident(bsm.shape, lambda i: (0, 0)),
            resident(w_br_gmlp.shape, lambda i: (0, 0)),
        ],
        out_specs=[tok(N_GLA_OPERANDS * GLA_DK), pl.BlockSpec((tm // GLA_CHUNK, 2 * GLA_DK), lambda i: (i, 0)),
                   tok(D), tok(D), tok(D), tok(D)],
        out_shape=[jax.ShapeDtypeStruct((T, N_GLA_OPERANDS * GLA_DK), BF16),
                   jax.ShapeDtypeStruct((T // GLA_CHUNK, 2 * GLA_DK), F32),
                   out_bf, out_bf, out_bf, out_bf],
        scratch_shapes=[pltpu.VMEM((tm, D), F32)],
        compiler_params=pltpu.CompilerParams(dimension_semantics=("arbitrary",),
                                             vmem_limit_bytes=VMEM_LIMIT_BYTES),
        name="projection_gmlp",
    )(x2, mod3, norm_g, w_lo, w_hi, w_ra, wa, ba, ln_g, ln_b, ws, bsm, w_br_gmlp)


def _gla_kernel(gl_ref, v_ref, d_ref, gz_ref, gng_ref, o_ref, of_scr, ob_scr, kvf_scr, kvb_scr):
    S = v_ref.shape[1]
    C = GLA_CHUNK
    hk = GLA_HK
    n_blk = S // GLA_BLK
    ti, si = _block_iotas()
    chunk_lo = (ti >> CHUNK_SHIFT) << CHUNK_SHIFT
    chunk_hi = chunk_lo + (C - 1)
    pos_chunk = lax.broadcasted_iota(jnp.int32, (hk, GLA_BLK), 1) >> CHUNK_SHIFT
    chunk_sel = [jnp.where(pos_chunk == c, 1.0, 0.0).astype(BF16) for c in range(CHUNKS_PER_BLK)]
    d_cols = d_ref[0].T
    fwd = (0 * hk, 1 * hk, 2 * hk, d_cols[:hk], of_scr, kvf_scr)
    bwd = (3 * hk, 4 * hk, 5 * hk, d_cols[hk:], ob_scr, kvb_scr)

    def operand(rows, lane0):
        return gl_ref[0, rows, lane0:lane0 + hk]

    def scores(j, operands, forward):
        rows = _aligned_rows(j * GLA_BLK, GLA_BLK)
        a = _dot_nt(operand(rows, operands[0]), operand(rows, operands[1]))
        if forward:
            a = jnp.where(si <= ti, jnp.where(si >= chunk_lo, a, 0.0), 0.0)
        else:
            a = jnp.where(si > ti, jnp.where(si <= chunk_hi, a, 0.0), 0.0)
        return a.astype(BF16)

    def chunk_states(j, operands):
        kv_scr = operands[5]
        rows = _aligned_rows(j * GLA_BLK, GLA_BLK)
        ke = operand(rows, operands[2])
        ke_t = ke.T
        ke_bd = jnp.concatenate([ke_t * chunk_sel[c] for c in range(CHUNKS_PER_BLK)], axis=0)
        kv_scr[j] = _dot(ke_bd, v_ref[0, rows, :])

    def weighted_values(j, a, operands):
        rows = _aligned_rows(j * GLA_BLK, GLA_BLK)
        operands[4][rows, :] = _dot(a, v_ref[0, rows, :])

    def scan(j, state, operands, forward):
        qt_lane, _, _, decays, o_scr, kv_scr = operands
        o_inter = [None] * CHUNKS_PER_BLK
        for c in (range(CHUNKS_PER_BLK) if forward else reversed(range(CHUNKS_PER_BLK))):
            rows = _aligned_rows(j * GLA_BLK + c * C, C)
            n = j * CHUNKS_PER_BLK + c
            o_inter[c] = _dot(operand(rows, qt_lane), state.astype(BF16))
            state = state * decays[:, n:n + 1] + kv_scr[j, c * hk:(c + 1) * hk, :]
        o_scr[_aligned_rows(j * GLA_BLK, GLA_BLK), :] += jnp.concatenate(o_inter, axis=0)
        return state

    def finish(j):
        rows = _aligned_rows(j * GLA_BLK, GLA_BLK)
        o = of_scr[rows, :] + ob_scr[rows, :]
        on = o * lax.rsqrt(jnp.mean(o * o, axis=-1, keepdims=True) + EPS) * gng_ref[0]
        o_ref[0, rows, :] = (on * gz_ref[0, rows, :].astype(F32)).astype(BF16)

    sf = sb = jnp.zeros((hk, GLA_HV), F32)
    for i in range(-1, n_blk):
        jf, jb = i, n_blk - 1 - i
        prepare = i + 1 < n_blk
        if prepare:
            a_f = scores(jf + 1, fwd, True)
            a_b = scores(jb - 1, bwd, False)
        if i >= 0:
            sf = scan(jf, sf, fwd, True)
            sb = scan(jb, sb, bwd, False)
        if prepare:
            chunk_states(jf + 1, fwd)
            chunk_states(jb - 1, bwd)
            weighted_values(jf + 1, a_f, fwd)
            weighted_values(jb - 1, a_b, bwd)
        for j in sorted({jf, jb}):
            if i >= 0 and max(j, n_blk - 1 - j) == i:
                finish(j)


def _gla(gl3, v3, d3, gz3, gng):
    B, S, D = v3.shape
    H, hk, hv = GLA_HEADS, GLA_HK, GLA_HV
    n_chunks = S // GLA_CHUNK
    assert S % GLA_BLK == 0
    head_v = pl.BlockSpec((1, S, hv), lambda b, h: (b, 0, h))
    kv_scratch = pltpu.VMEM((S // GLA_BLK, CHUNKS_PER_BLK * hk, hv), F32)
    return pl.pallas_call(
        _gla_kernel,
        grid=(B, H),
        in_specs=[
            pl.BlockSpec((1, S, N_GLA_OPERANDS * hk), lambda b, h: (b, 0, h)),
            head_v,
            pl.BlockSpec((1, n_chunks, 2 * hk), lambda b, h: (b, 0, h)),
            head_v,
            pl.BlockSpec((1, 1, hv), lambda b, h: (h, 0, 0)),
        ],
        out_specs=head_v,
        out_shape=jax.ShapeDtypeStruct((B, S, D), BF16),
        scratch_shapes=[pltpu.VMEM((S, hv), F32), pltpu.VMEM((S, hv), F32), kv_scratch, kv_scratch],
        compiler_params=pltpu.CompilerParams(dimension_semantics=("arbitrary", "arbitrary"),
                                             vmem_limit_bytes=VMEM_LIMIT_BYTES),
        name="gla_bidir",
    )(gl3, v3, d3, gz3, gng)


def _out_kernel(x_ref, g1_ref, sgl_ref, mg_ref, mod_ref, wbr_ref, wout_ref, fg_ref, o_ref):
    gate = mod_ref[0, 2:3, :]
    n_sub = x_ref.shape[0] // OUT_SUB
    subs = [slice(s * OUT_SUB, (s + 1) * OUT_SUB) for s in range(n_sub)]
    y_gla = [None] * n_sub
    y_gla[0] = _dot(g1_ref[subs[0], :], wbr_ref[...])
    for s, rows in enumerate(subs):
        if s + 1 < n_sub:
            y_gla[s + 1] = _dot(g1_ref[subs[s + 1], :], wbr_ref[...])
        merged = (sgl_ref[rows, :].astype(F32) * y_gla[s] + mg_ref[rows, :].astype(F32)).astype(BF16)
        r = x_ref[rows, :] + gate * _dot(merged, wout_ref[...])
        o_ref[rows, :] = (r * lax.rsqrt(jnp.mean(r * r, axis=-1, keepdims=True) + EPS)) * fg_ref[...]


def _merge_out(x2, g1, sgl, mg, mod3, w_br_gla, w_out, final_g, seq):
    T, D = x2.shape
    tm = OUT_TM
    assert T % tm == 0 and seq % tm == 0
    resident = functools.partial(pl.BlockSpec, pipeline_mode=pl.Buffered(1))
    tok = pl.BlockSpec((tm, D), lambda i: (i, 0))
    tiles_per_seq = seq // tm
    return pl.pallas_call(
        _out_kernel,
        grid=(T // tm,),
        in_specs=[
            tok, tok, tok, tok,
            pl.BlockSpec((1, 3, D), lambda i: (i // tiles_per_seq, 0, 0)),
            resident((D, D), lambda i: (0, 0)),
            resident((D, D), lambda i: (0, 0)),
            resident((1, D), lambda i: (0, 0)),
        ],
        out_specs=tok,
        out_shape=jax.ShapeDtypeStruct((T, D), F32),
        compiler_params=pltpu.CompilerParams(dimension_semantics=("arbitrary",),
                                             vmem_limit_bytes=VMEM_LIMIT_BYTES),
        name="merge_out",
    )(x2, g1, sgl, mg, mod3, w_br_gla, w_out, final_g)


def _layer(x, c, norm_g, w_ada, b_ada, w_in, alpha_fw_w, alpha_fw_b, alpha_bw_w, alpha_bw_b,
           gla_norm_g, gmlp_ln_g, gmlp_ln_b, gmlp_ws, gmlp_bs, w_br_gla, w_br_gmlp, w_out, out_g):
    B, S, D = x.shape
    H, hv, R, DK = GLA_HEADS, GLA_HV, GLA_RANK, GLA_DK
    T = B * S

    o_ra = 2 * DK + 2 * D
    assert o_ra == N_LO_GROUPS * D
    w_lo = w_in[:, :o_ra].astype(BF16)
    w_hi = w_in[:, o_ra + 2 * R:].astype(BF16)
    w_ra = jnp.pad(w_in[:, o_ra:o_ra + 2 * R], ((0, 0), (0, RA_PAD - 2 * R))).astype(BF16)
    zeros = jnp.zeros_like(alpha_fw_w)
    wa = jnp.concatenate([jnp.concatenate([alpha_fw_w, zeros], axis=1),
                          jnp.concatenate([zeros, alpha_bw_w], axis=1)], axis=0)
    wa = jnp.pad(wa, ((0, RA_PAD - 2 * R), (0, 0))).astype(BF16)
    ba = jnp.concatenate([alpha_fw_b, alpha_bw_b]).reshape(1, 2 * DK)
    bsm = jnp.repeat(gmlp_bs.T, GMLP_GC, axis=1)

    mod3 = _adaln_mod(c, w_ada, b_ada).reshape(B, 3, D)
    x2 = x.reshape(T, D)
    gl, d, v, gz, sgl, mg = _projection(
        x2, mod3, norm_g.reshape(1, D), w_lo, w_hi, w_ra, wa, ba, gmlp_ln_g.reshape(1, D), gmlp_ln_b.reshape(1, D),
        gmlp_ws.astype(BF16), bsm, w_br_gmlp.astype(BF16), S)
    g1 = _gla(gl.reshape(B, S, N_GLA_OPERANDS * DK), v.reshape(B, S, D), d.reshape(B, S // GLA_CHUNK, 2 * DK),
              gz.reshape(B, S, D), gla_norm_g.reshape(H, 1, hv))
    out = _merge_out(x2, g1.reshape(T, D), sgl, mg, mod3, w_br_gla.astype(BF16), w_out.astype(BF16),
                     out_g.reshape(1, D), S)
    return out.reshape(B, S, D)


def kernel(x, c, norm_g, w_ada, b_ada, w_in, alpha_fw_w, alpha_fw_b, alpha_bw_w, alpha_bw_b, gla_norm_g,
           gmlp_ln_g, gmlp_ln_b, gmlp_ws, gmlp_bs, w_br_gla, w_br_gmlp, w_out, final_g):
    depth = norm_g.shape[0]
    assert depth == 1, "the final rmsnorm is fused into the layer's output kernel"
    return _layer(x, c, norm_g[0], w_ada[0], b_ada[0], w_in[0], alpha_fw_w[0], alpha_fw_b[0], alpha_bw_w[0],
                  alpha_bw_b[0], gla_norm_g[0], gmlp_ln_g[0], gmlp_ln_b[0], gmlp_ws[0], gmlp_bs[0],
                  w_br_gla[0], w_br_gmlp[0], w_out[0], final_g)
```

```python
import functools

import jax
import jax.numpy as jnp
from jax import lax
from jax.experimental import pallas as pl
from jax.experimental.pallas import tpu as pltpu

F32 = jnp.float32
BF16 = jnp.bfloat16

D_MODEL = 1024
GLA_HEADS = 4
GLA_HK = 128
GLA_HV = 256
GLA_DK = GLA_HEADS * GLA_HK
GLA_RANK = 16
GLA_TAU = 16.0
GLA_CHUNK = 64
LOG_DECAY_FLOOR = -1.25
GMLP_GROUPS = 8
GMLP_GC = 128
GMLP_CHUNK = 128
EPS = 1e-6
SQRT_HALF = 0.7071067811865476
LOG2_E = 1.4426950408889634

RA_PAD = 128
MXU_K = 256
N_LO_GROUPS = 3
VMEM_LIMIT_BYTES = 56 * 1024 * 1024

PROJ_TM = 512
OUT_TM = 1024
OUT_SUB = 256
GLA_BLK = 256
CHUNKS_PER_BLK = GLA_BLK // GLA_CHUNK
CHUNK_SHIFT = GLA_CHUNK.bit_length() - 1
N_GLA_OPERANDS = 6


def _silu(z):
    return z * jax.nn.sigmoid(z)


def _gelu_exact(z):
    return 0.5 * z * (1.0 + lax.erf(z * SQRT_HALF))


def _log_sigmoid(z):
    return jnp.minimum(z, 0.0) - jnp.log1p(jnp.exp(-jnp.abs(z)))


def _dot(a, b):
    return jnp.dot(a, b, preferred_element_type=F32)


def _dot_nt(a, b):
    return lax.dot_general(a, b, (((1,), (1,)), ((), ())), preferred_element_type=F32)


def _dot_tn(a, b):
    return lax.dot_general(a, b, (((0,), (0,)), ((), ())), preferred_element_type=F32)


def _aligned_rows(start, size):
    if isinstance(start, int):
        return pl.ds(start, size)
    return pl.ds(pl.multiple_of(start, size), size)


def _block_iotas():
    ti = lax.broadcasted_iota(jnp.int32, (GLA_BLK, GLA_BLK), 0)
    si = lax.broadcasted_iota(jnp.int32, (GLA_BLK, GLA_BLK), 1)
    return ti, si


def _mod_kernel(c_ref, w_ref, b_ref, o_ref):
    a = _silu(c_ref[...]).astype(BF16)
    o_ref[...] = _dot(a, w_ref[...].astype(BF16)) + b_ref[...]


def _adaln_mod(c, w_ada, b_ada):
    B, D = c.shape
    n_out = w_ada.shape[1]
    return pl.pallas_call(
        _mod_kernel,
        grid=(n_out // D,),
        in_specs=[
            pl.BlockSpec((B, D), lambda j: (0, 0)),
            pl.BlockSpec((D, D), lambda j: (0, j)),
            pl.BlockSpec((1, D), lambda j: (0, j)),
        ],
        out_specs=pl.BlockSpec((B, D), lambda j: (0, j)),
        out_shape=jax.ShapeDtypeStruct((B, n_out), F32),
        compiler_params=pltpu.CompilerParams(dimension_semantics=("arbitrary",)),
        name="adaln_mod",
    )(c, w_ada, b_ada.reshape(1, n_out))


def _weight_prep_kernel(a_ref, b_ref, o_ref):
    skip = 2 * GLA_RANK

    @pl.when(pl.program_id(0) < N_LO_GROUPS)
    def _():
        o_ref[...] = a_ref[...].astype(BF16)

    @pl.when(pl.program_id(0) >= N_LO_GROUPS)
    def _():
        o_ref[...] = jnp.concatenate([a_ref[:, skip:], b_ref[:, :skip]], axis=1).astype(BF16)


def _weight_prep(w_in):
    K = w_in.shape[0]
    D = D_MODEL
    n_groups = (w_in.shape[1] - 2 * GLA_RANK) // D
    return pl.pallas_call(
        _weight_prep_kernel,
        grid=(n_groups,),
        in_specs=[
            pl.BlockSpec((K, D), lambda g: (0, g)),
            pl.BlockSpec((K, RA_PAD), lambda g: (0, (g + 1) * (D // RA_PAD))),
        ],
        out_specs=pl.BlockSpec((K, D), lambda g: (0, g)),
        out_shape=jax.ShapeDtypeStruct((K, n_groups * D), BF16),
        compiler_params=pltpu.CompilerParams(dimension_semantics=("arbitrary",),
                                             vmem_limit_bytes=VMEM_LIMIT_BYTES),
        name="weight_prep",
    )(w_in, w_in)


def _proj_kernel(x_ref, mod_ref, ng_ref, wm_ref, wra_ref, wa_ref, ba_ref, lng_ref, lnb_ref, ws_ref, bsm_ref,
                 wbg_ref, gl_ref, d_ref, v_ref, gz_ref, sgl_ref, mg_ref, sg_scr):
    D = D_MODEL
    DK = GLA_DK
    hk = GLA_HK
    C = GLA_CHUNK
    tm = x_ref.shape[0]
    half = tm // 2
    shift = mod_ref[0, 0:1, :]
    gain = ng_ref[...] * (1.0 + mod_ref[0, 1:2, :])

    def normed(rows):
        x = x_ref[rows, :]
        r = lax.rsqrt(jnp.mean(x * x, axis=-1, keepdims=True) + EPS)
        return ((x * r) * gain + shift).astype(BF16)

    h_top = normed(slice(0, half))
    h_bot = normed(slice(half, tm))
    h = jnp.concatenate([h_top, h_bot], axis=0)

    def proj(j):
        return _dot(h, wm_ref[:, j * D:(j + 1) * D])


    qk = jnp.concatenate([_dot(h_top, wm_ref[:, :D]), _dot(h_bot, wm_ref[:, :D])], axis=0)
    ra = jnp.concatenate([_dot(h_top, wra_ref[...]), _dot(h_bot, wra_ref[...])], axis=0).astype(BF16)
    pre = _dot(ra, wa_ref[...]) + ba_ref[...]
    la = jnp.maximum(_log_sigmoid(pre) * (1.0 / GLA_TAU), LOG_DECAY_FLOOR) * LOG2_E

    v_ref[...] = proj(1).astype(BF16)

    hi = la.astype(BF16)
    r1 = la - hi.astype(F32)
    mid = r1.astype(BF16)
    lo = (r1 - mid.astype(F32)).astype(BF16)

    u_pre = proj(3)

    ti, si = _block_iotas()
    same_chunk = (ti >> CHUNK_SHIFT) == (si >> CHUNK_SHIFT)
    tri_f = jnp.where(same_chunk & (si <= ti), 1.0, 0.0).astype(BF16)
    tri_b = jnp.where(same_chunk & (si >= ti), 1.0, 0.0).astype(BF16)
    cum = []
    for blk in range(tm // GLA_BLK):
        rs = slice(blk * GLA_BLK, (blk + 1) * GLA_BLK)
        b_f = _dot(tri_f, hi[rs, :DK]) + _dot(tri_f, mid[rs, :DK]) + _dot(tri_f, lo[rs, :DK])
        b_b = _dot(tri_b, hi[rs, DK:]) + _dot(tri_b, mid[rs, DK:]) + _dot(tri_b, lo[rs, DK:])
        cum.append((b_f, b_b))

    sgl_ref[...] = jax.nn.sigmoid(proj(6)).astype(BF16)

    q_scale = GLA_HK ** -0.5
    for blk in range(tm // GLA_BLK):
        b_f, b_b = cum[blk]
        for c in range(CHUNKS_PER_BLK):
            cs = slice(c * C, (c + 1) * C)
            gr = slice(blk * GLA_BLK + c * C, blk * GLA_BLK + (c + 1) * C)
            n = blk * CHUNKS_PER_BLK + c
            q = qk[gr, :DK] * q_scale
            k = qk[gr, DK:]
            bf = b_f[cs]
            bb = b_b[cs]
            bf_last = bf[C - 1:C]
            bb_last = bb[0:1]
            operands = (q * jnp.exp2(bf), k * jnp.exp2(-bf), k * jnp.exp2(bf_last - bf),
                        q * jnp.exp2(bb), k * jnp.exp2(-bb), k * jnp.exp2(bb_last - bb))
            decays = (jnp.exp2(bf_last), jnp.exp2(bb_last))
            for hd in range(GLA_HEADS):
                hs = slice(hd * hk, (hd + 1) * hk)
                for g, val in enumerate(operands):
                    lane0 = (hd * N_GLA_OPERANDS + g) * hk
                    gl_ref[gr, lane0:lane0 + hk] = val[:, hs].astype(BF16)
                for g, val in enumerate(decays):
                    lane0 = (hd * len(decays) + g) * hk
                    d_ref[n:n + 1, lane0:lane0 + hk] = val[:, hs]

    zg = proj(5)

    vs = _gelu_exact(proj(4))
    mu = jnp.mean(vs, axis=-1, keepdims=True)
    vc = vs - mu
    var = jnp.mean(vc * vc, axis=-1, keepdims=True)
    vsn = ((vc * lax.rsqrt(var + EPS)) * lng_ref[...] + lnb_ref[...]).astype(BF16)

    gz_ref[...] = _silu(proj(2)).astype(BF16)

    n_pos = tm // GMLP_CHUNK
    for g in range(GMLP_GROUPS):
        cs = slice(g * GMLP_GC, (g + 1) * GMLP_GC)
        rhs = jnp.concatenate([vsn[n * GMLP_CHUNK:(n + 1) * GMLP_CHUNK, cs] for n in range(n_pos)], axis=1)
        mixed = _dot(ws_ref[g], rhs)
        for n in range(n_pos):
            sg_scr[n * GMLP_CHUNK:(n + 1) * GMLP_CHUNK, cs] = (
                mixed[:, n * GMLP_GC:(n + 1) * GMLP_GC] + bsm_ref[:, cs])
    merge_gate = jax.nn.sigmoid(proj(7))

    y_gmlp = None
    for kt in range(D // MXU_K):
        ks = slice(kt * MXU_K, (kt + 1) * MXU_K)
        g2 = ((_gelu_exact(u_pre[:, ks]) * sg_scr[:, ks]) * _silu(zg[:, ks])).astype(BF16)
        part = _dot(g2, wbg_ref[ks, :])
        y_gmlp = part if y_gmlp is None else y_gmlp + part
    mg_ref[...] = (merge_gate * y_gmlp).astype(BF16)


def _projection(x2, mod3, norm_g, w_main, w_ra, wa, ba, ln_g, ln_b, ws, bsm, w_br_gmlp, seq):
    T, D = x2.shape
    tm = PROJ_TM
    assert T % tm == 0 and seq % tm == 0 and tm % GMLP_CHUNK == 0 and tm % GLA_BLK == 0
    resident = functools.partial(pl.BlockSpec, pipeline_mode=pl.Buffered(1))
    tok = lambda w: pl.BlockSpec((tm, w), lambda i: (i, 0))
    tiles_per_seq = seq // tm
    out_bf = jax.ShapeDtypeStruct((T, D), BF16)
    return pl.pallas_call(
        _proj_kernel,
        grid=(T // tm,),
        in_specs=[
            tok(D),
            pl.BlockSpec((1, 3, D), lambda i: (i // tiles_per_seq, 0, 0)),
            resident((1, D), lambda i: (0, 0)),
            resident(w_main.shape, lambda i: (0, 0)),
            resident(w_ra.shape, lambda i: (0, 0)),
            resident(wa.shape, lambda i: (0, 0)),
            resident(ba.shape, lambda i: (0, 0)),
            resident((1, D), lambda i: (0, 0)),
            resident((1, D), lambda i: (0, 0)),
            resident(ws.shape, lambda i: (0, 0, 0)),
            resident(bsm.shape, lambda i: (0, 0)),
            resident(w_br_gmlp.shape, lambda i: (0, 0)),
        ],
        out_specs=[tok(N_GLA_OPERANDS * GLA_DK), pl.BlockSpec((tm // GLA_CHUNK, 2 * GLA_DK), lambda i: (i, 0)),
                   tok(D), tok(D), tok(D), tok(D)],
        out_shape=[jax.ShapeDtypeStruct((T, N_GLA_OPERANDS * GLA_DK), BF16),
                   jax.ShapeDtypeStruct((T // GLA_CHUNK, 2 * GLA_DK), F32),
                   out_bf, out_bf, out_bf, out_bf],
        scratch_shapes=[pltpu.VMEM((tm, D), F32)],
        compiler_params=pltpu.CompilerParams(dimension_semantics=("arbitrary",),
                                             vmem_limit_bytes=VMEM_LIMIT_BYTES),
        name="projection_gmlp",
    )(x2, mod3, norm_g, w_main, w_ra, wa, ba, ln_g, ln_b, ws, bsm, w_br_gmlp)


def _gla_kernel(gl_ref, v_ref, d_ref, gz_ref, gng_ref, o_ref, of_scr, ob_scr, kvf_scr, kvb_scr):
    S = v_ref.shape[1]
    C = GLA_CHUNK
    hk = GLA_HK
    n_blk = S // GLA_BLK
    ti, si = _block_iotas()
    chunk_lo = (ti >> CHUNK_SHIFT) << CHUNK_SHIFT
    chunk_hi = chunk_lo + (C - 1)
    pos_chunk = lax.broadcasted_iota(jnp.int32, (hk, GLA_BLK), 1) >> CHUNK_SHIFT
    chunk_sel = [jnp.where(pos_chunk == c, 1.0, 0.0).astype(BF16) for c in range(CHUNKS_PER_BLK)]
    d_cols = d_ref[0].T
    fwd = (0 * hk, 1 * hk, 2 * hk, d_cols[:hk], of_scr, kvf_scr)
    bwd = (3 * hk, 4 * hk, 5 * hk, d_cols[hk:], ob_scr, kvb_scr)

    def operand(rows, lane0):
        return gl_ref[0, rows, lane0:lane0 + hk]

    def scores(j, operands, forward):
        rows = _aligned_rows(j * GLA_BLK, GLA_BLK)
        a = _dot_nt(operand(rows, operands[0]), operand(rows, operands[1]))
        if forward:
            a = jnp.where(si <= ti, jnp.where(si >= chunk_lo, a, 0.0), 0.0)
        else:
            a = jnp.where(si > ti, jnp.where(si <= chunk_hi, a, 0.0), 0.0)
        return a.astype(BF16)

    def chunk_states(j, operands):
        kv_scr = operands[5]
        rows = _aligned_rows(j * GLA_BLK, GLA_BLK)
        ke = operand(rows, operands[2])
        ke_t = ke.T
        ke_bd = jnp.concatenate([ke_t * chunk_sel[c] for c in range(CHUNKS_PER_BLK)], axis=0)
        kv_scr[j] = _dot(ke_bd, v_ref[0, rows, :])

    def weighted_values(j, a, operands):
        rows = _aligned_rows(j * GLA_BLK, GLA_BLK)
        operands[4][rows, :] = _dot(a, v_ref[0, rows, :])

    def scan(j, state, operands, forward):
        qt_lane, _, _, decays, o_scr, kv_scr = operands
        o_inter = [None] * CHUNKS_PER_BLK
        for c in (range(CHUNKS_PER_BLK) if forward else reversed(range(CHUNKS_PER_BLK))):
            rows = _aligned_rows(j * GLA_BLK + c * C, C)
            n = j * CHUNKS_PER_BLK + c
            o_inter[c] = _dot(operand(rows, qt_lane), state.astype(BF16))
            state = state * decays[:, n:n + 1] + kv_scr[j, c * hk:(c + 1) * hk, :]
        o_scr[_aligned_rows(j * GLA_BLK, GLA_BLK), :] += jnp.concatenate(o_inter, axis=0)
        return state

    def finish(j):
        rows = _aligned_rows(j * GLA_BLK, GLA_BLK)
        o = of_scr[rows, :] + ob_scr[rows, :]
        on = o * lax.rsqrt(jnp.mean(o * o, axis=-1, keepdims=True) + EPS) * gng_ref[0]
        o_ref[0, rows, :] = (on * gz_ref[0, rows, :].astype(F32)).astype(BF16)

    sf = sb = jnp.zeros((hk, GLA_HV), F32)
    for i in range(-1, n_blk):
        jf, jb = i, n_blk - 1 - i
        prepare = i + 1 < n_blk
        if prepare:
            a_f = scores(jf + 1, fwd, True)
            a_b = scores(jb - 1, bwd, False)
        if i >= 0:
            sf = scan(jf, sf, fwd, True)
            sb = scan(jb, sb, bwd, False)
        if prepare:
            chunk_states(jf + 1, fwd)
            chunk_states(jb - 1, bwd)
            weighted_values(jf + 1, a_f, fwd)
            weighted_values(jb - 1, a_b, bwd)
        for j in sorted({jf, jb}):
            if i >= 0 and max(j, n_blk - 1 - j) == i:
                finish(j)


def _gla(gl3, v3, d3, gz3, gng):
    B, S, D = v3.shape
    H, hk, hv = GLA_HEADS, GLA_HK, GLA_HV
    n_chunks = S // GLA_CHUNK
    assert S % GLA_BLK == 0
    head_v = pl.BlockSpec((1, S, hv), lambda b, h: (b, 0, h))
    kv_scratch = pltpu.VMEM((S // GLA_BLK, CHUNKS_PER_BLK * hk, hv), F32)
    return pl.pallas_call(
        _gla_kernel,
        grid=(B, H),
        in_specs=[
            pl.BlockSpec((1, S, N_GLA_OPERANDS * hk), lambda b, h: (b, 0, h)),
            head_v,
            pl.BlockSpec((1, n_chunks, 2 * hk), lambda b, h: (b, 0, h)),
            head_v,
            pl.BlockSpec((1, 1, hv), lambda b, h: (h, 0, 0)),
        ],
        out_specs=head_v,
        out_shape=jax.ShapeDtypeStruct((B, S, D), BF16),
        scratch_shapes=[pltpu.VMEM((S, hv), F32), pltpu.VMEM((S, hv), F32), kv_scratch, kv_scratch],
        compiler_params=pltpu.CompilerParams(dimension_semantics=("arbitrary", "arbitrary"),
                                             vmem_limit_bytes=VMEM_LIMIT_BYTES),
        name="gla_bidir",
    )(gl3, v3, d3, gz3, gng)


def _out_kernel(x_ref, g1_ref, sgl_ref, mg_ref, mod_ref, wbr_ref, wout_ref, fg_ref, o_ref):
    gate = mod_ref[0, 2:3, :]
    n_sub = x_ref.shape[0] // OUT_SUB
    subs = [slice(s * OUT_SUB, (s + 1) * OUT_SUB) for s in range(n_sub)]
    y_gla = [None] * n_sub
    y_gla[0] = _dot(g1_ref[subs[0], :], wbr_ref[...])
    for s, rows in enumerate(subs):
        if s + 1 < n_sub:
            y_gla[s + 1] = _dot(g1_ref[subs[s + 1], :], wbr_ref[...])
        merged = (sgl_ref[rows, :].astype(F32) * y_gla[s] + mg_ref[rows, :].astype(F32)).astype(BF16)
        r = x_ref[rows, :] + gate * _dot(merged, wout_ref[...])
        o_ref[rows, :] = (r * lax.rsqrt(jnp.mean(r * r, axis=-1, keepdims=True) + EPS)) * fg_ref[...]


def _merge_out(x2, g1, sgl, mg, mod3, w_br_gla, w_out, final_g, seq):
    T, D = x2.shape
    tm = OUT_TM
    assert T % tm == 0 and seq % tm == 0
    resident = functools.partial(pl.BlockSpec, pipeline_mode=pl.Buffered(1))
    tok = pl.BlockSpec((tm, D), lambda i: (i, 0))
    tiles_per_seq = seq // tm
    return pl.pallas_call(
        _out_kernel,
        grid=(T // tm,),
        in_specs=[
            tok, tok, tok, tok,
            pl.BlockSpec((1, 3, D), lambda i: (i // tiles_per_seq, 0, 0)),
            resident((D, D), lambda i: (0, 0)),
            resident((D, D), lambda i: (0, 0)),
            resident((1, D), lambda i: (0, 0)),
        ],
        out_specs=tok,
        out_shape=jax.ShapeDtypeStruct((T, D), F32),
        compiler_params=pltpu.CompilerParams(dimension_semantics=("arbitrary",),
                                             vmem_limit_bytes=VMEM_LIMIT_BYTES),
        name="merge_out",
    )(x2, g1, sgl, mg, mod3, w_br_gla, w_out, final_g)


def _layer(x, c, norm_g, w_ada, b_ada, w_in, alpha_fw_w, alpha_fw_b, alpha_bw_w, alpha_bw_b,
           gla_norm_g, gmlp_ln_g, gmlp_ln_b, gmlp_ws, gmlp_bs, w_br_gla, w_br_gmlp, w_out, out_g):
    B, S, D = x.shape
    H, hv, R, DK = GLA_HEADS, GLA_HV, GLA_RANK, GLA_DK
    T = B * S

    o_ra = 2 * DK + 2 * D
    assert o_ra == N_LO_GROUPS * D
    w_main = _weight_prep(w_in)
    w_ra = jnp.pad(w_in[:, o_ra:o_ra + 2 * R], ((0, 0), (0, RA_PAD - 2 * R))).astype(BF16)
    zeros = jnp.zeros_like(alpha_fw_w)
    wa = jnp.concatenate([jnp.concatenate([alpha_fw_w, zeros], axis=1),
                          jnp.concatenate([zeros, alpha_bw_w], axis=1)], axis=0)
    wa = jnp.pad(wa, ((0, RA_PAD - 2 * R), (0, 0))).astype(BF16)
    ba = jnp.concatenate([alpha_fw_b, alpha_bw_b]).reshape(1, 2 * DK)
    bsm = jnp.repeat(gmlp_bs.T, GMLP_GC, axis=1)

    mod3 = _adaln_mod(c, w_ada, b_ada).reshape(B, 3, D)
    x2 = x.reshape(T, D)
    gl, d, v, gz, sgl, mg = _projection(
        x2, mod3, norm_g.reshape(1, D), w_main, w_ra, wa, ba, gmlp_ln_g.reshape(1, D), gmlp_ln_b.reshape(1, D),
        gmlp_ws.astype(BF16), bsm, w_br_gmlp.astype(BF16), S)
    g1 = _gla(gl.reshape(B, S, N_GLA_OPERANDS * DK), v.reshape(B, S, D), d.reshape(B, S // GLA_CHUNK, 2 * DK),
              gz.reshape(B, S, D), gla_norm_g.reshape(H, 1, hv))
    out = _merge_out(x2, g1.reshape(T, D), sgl, mg, mod3, w_br_gla.astype(BF16), w_out.astype(BF16),
                     out_g.reshape(1, D), S)
    return out.reshape(B, S, D)


def kernel(x, c, norm_g, w_ada, b_ada, w_in, alpha_fw_w, alpha_fw_b, alpha_bw_w, alpha_bw_b, gla_norm_g,
           gmlp_ln_g, gmlp_ln_b, gmlp_ws, gmlp_bs, w_br_gla, w_br_gmlp, w_out, final_g):
    depth = norm_g.shape[0]
    assert depth == 1, "the final rmsnorm is fused into the layer's output kernel"
    return _layer(x, c, norm_g[0], w_ada[0], b_ada[0], w_in[0], alpha_fw_w[0], alpha_fw_b[0], alpha_bw_w[0],
                  alpha_bw_b[0], gla_norm_g[0], gmlp_ln_g[0], gmlp_ln_b[0], gmlp_ws[0], gmlp_bs[0],
                  w_br_gla[0], w_br_gmlp[0], w_out[0], final_g)
```

```python
import functools

import jax
import jax.numpy as jnp
from jax import lax
from jax.experimental import pallas as pl
from jax.experimental.pallas import tpu as pltpu

F32 = jnp.float32
BF16 = jnp.bfloat16

D_MODEL = 1024
GLA_HEADS = 4
GLA_HK = 128
GLA_HV = 256
GLA_DK = GLA_HEADS * GLA_HK
GLA_RANK = 16
GLA_TAU = 16.0
GLA_CHUNK = 64
LOG_DECAY_FLOOR = -1.25
GMLP_GROUPS = 8
GMLP_GC = 128
GMLP_CHUNK = 128
EPS = 1e-6
SQRT_HALF = 0.7071067811865476
LOG2_E = 1.4426950408889634

RA_PAD = 128
MXU_K = 256
N_LO_GROUPS = 3
VMEM_LIMIT_BYTES = 56 * 1024 * 1024

PROJ_TM = 512
OUT_TM = 1024
OUT_SUB = 256
GLA_BLK = 256
CHUNKS_PER_BLK = GLA_BLK // GLA_CHUNK
CHUNK_SHIFT = GLA_CHUNK.bit_length() - 1
N_GLA_OPERANDS = 6


def _silu(z):
    return z * jax.nn.sigmoid(z)


def _gelu_exact(z):
    return 0.5 * z * (1.0 + lax.erf(z * SQRT_HALF))


def _log_sigmoid(z):
    return jnp.minimum(z, 0.0) - jnp.log1p(jnp.exp(-jnp.abs(z)))


def _dot(a, b):
    return jnp.dot(a, b, preferred_element_type=F32)


def _dot_nt(a, b):
    return lax.dot_general(a, b, (((1,), (1,)), ((), ())), preferred_element_type=F32)


def _dot_tn(a, b):
    return lax.dot_general(a, b, (((0,), (0,)), ((), ())), preferred_element_type=F32)


def _aligned_rows(start, size):
    if isinstance(start, int):
        return pl.ds(start, size)
    return pl.ds(pl.multiple_of(start, size), size)


def _block_iotas():
    ti = lax.broadcasted_iota(jnp.int32, (GLA_BLK, GLA_BLK), 0)
    si = lax.broadcasted_iota(jnp.int32, (GLA_BLK, GLA_BLK), 1)
    return ti, si


def _mod_kernel(c_ref, w_ref, b_ref, o_ref):
    a = _silu(c_ref[...]).astype(BF16)
    o_ref[...] = _dot(a, w_ref[...].astype(BF16)) + b_ref[...]


def _adaln_mod(c, w_ada, b_ada):
    B, D = c.shape
    n_out = w_ada.shape[1]
    return pl.pallas_call(
        _mod_kernel,
        grid=(n_out // D,),
        in_specs=[
            pl.BlockSpec((B, D), lambda j: (0, 0)),
            pl.BlockSpec((D, D), lambda j: (0, j)),
            pl.BlockSpec((1, D), lambda j: (0, j)),
        ],
        out_specs=pl.BlockSpec((B, D), lambda j: (0, j)),
        out_shape=jax.ShapeDtypeStruct((B, n_out), F32),
        compiler_params=pltpu.CompilerParams(dimension_semantics=("arbitrary",)),
        name="adaln_mod",
    )(c, w_ada, b_ada.reshape(1, n_out))


def _weight_prep_kernel(a_ref, b_ref, o_ref):
    skip = 2 * GLA_RANK

    @pl.when(pl.program_id(0) < N_LO_GROUPS)
    def _():
        o_ref[...] = a_ref[...].astype(BF16)

    @pl.when(pl.program_id(0) >= N_LO_GROUPS)
    def _():
        o_ref[...] = jnp.concatenate([a_ref[skip:, :], b_ref[...]], axis=0).astype(BF16)


def _weight_prep(w_t):
    K = w_t.shape[1]
    D = D_MODEL
    skip = 2 * GLA_RANK
    n_groups = (w_t.shape[0] - skip) // D
    return pl.pallas_call(
        _weight_prep_kernel,
        grid=(n_groups,),
        in_specs=[
            pl.BlockSpec((D, K), lambda g: (g, 0)),
            pl.BlockSpec((skip, K), lambda g: ((g + 1) * (D // skip), 0)),
        ],
        out_specs=pl.BlockSpec((D, K), lambda g: (g, 0)),
        out_shape=jax.ShapeDtypeStruct((n_groups * D, K), BF16),
        compiler_params=pltpu.CompilerParams(dimension_semantics=("arbitrary",),
                                             vmem_limit_bytes=VMEM_LIMIT_BYTES),
        name="weight_prep",
    )(w_t, w_t)


def _proj_kernel(x_ref, mod_ref, ng_ref, wm_ref, wra_ref, wa_ref, ba_ref, lng_ref, lnb_ref, ws_ref, bsm_ref,
                 wbg_ref, gl_ref, d_ref, v_ref, gz_ref, sgl_ref, mg_ref, sg_scr):
    D = D_MODEL
    DK = GLA_DK
    hk = GLA_HK
    C = GLA_CHUNK
    tm = x_ref.shape[0]
    half = tm // 2
    shift = mod_ref[0, 0:1, :]
    gain = ng_ref[...] * (1.0 + mod_ref[0, 1:2, :])

    def normed(rows):
        x = x_ref[rows, :]
        r = lax.rsqrt(jnp.mean(x * x, axis=-1, keepdims=True) + EPS)
        return ((x * r) * gain + shift).astype(BF16)

    h_top = normed(slice(0, half))
    h_bot = normed(slice(half, tm))
    h = jnp.concatenate([h_top, h_bot], axis=0)

    def proj(j, lhs=None):
        return _dot_nt(h if lhs is None else lhs, wm_ref[j * D:(j + 1) * D, :])


    qk = jnp.concatenate([proj(0, h_top), proj(0, h_bot)], axis=0)
    ra = jnp.concatenate([_dot_nt(h_top, wra_ref[...]), _dot_nt(h_bot, wra_ref[...])], axis=0).astype(BF16)
    pre = _dot(ra, wa_ref[...]) + ba_ref[...]
    la = jnp.maximum(_log_sigmoid(pre) * (1.0 / GLA_TAU), LOG_DECAY_FLOOR) * LOG2_E

    v_ref[...] = proj(1).astype(BF16)

    hi = la.astype(BF16)
    r1 = la - hi.astype(F32)
    mid = r1.astype(BF16)
    lo = (r1 - mid.astype(F32)).astype(BF16)

    u_pre = proj(3)

    ti, si = _block_iotas()
    same_chunk = (ti >> CHUNK_SHIFT) == (si >> CHUNK_SHIFT)
    tri_f = jnp.where(same_chunk & (si <= ti), 1.0, 0.0).astype(BF16)
    tri_b = jnp.where(same_chunk & (si >= ti), 1.0, 0.0).astype(BF16)
    cum = []
    for blk in range(tm // GLA_BLK):
        rs = slice(blk * GLA_BLK, (blk + 1) * GLA_BLK)
        b_f = _dot(tri_f, hi[rs, :DK]) + _dot(tri_f, mid[rs, :DK]) + _dot(tri_f, lo[rs, :DK])
        b_b = _dot(tri_b, hi[rs, DK:]) + _dot(tri_b, mid[rs, DK:]) + _dot(tri_b, lo[rs, DK:])
        cum.append((b_f, b_b))

    sgl_ref[...] = jax.nn.sigmoid(proj(6)).astype(BF16)

    q_scale = GLA_HK ** -0.5
    for blk in range(tm // GLA_BLK):
        b_f, b_b = cum[blk]
        for c in range(CHUNKS_PER_BLK):
            cs = slice(c * C, (c + 1) * C)
            gr = slice(blk * GLA_BLK + c * C, blk * GLA_BLK + (c + 1) * C)
            n = blk * CHUNKS_PER_BLK + c
            q = qk[gr, :DK] * q_scale
            k = qk[gr, DK:]
            bf = b_f[cs]
            bb = b_b[cs]
            bf_last = bf[C - 1:C]
            bb_last = bb[0:1]
            operands = (q * jnp.exp2(bf), k * jnp.exp2(-bf), k * jnp.exp2(bf_last - bf),
                        q * jnp.exp2(bb), k * jnp.exp2(-bb), k * jnp.exp2(bb_last - bb))
            decays = (jnp.exp2(bf_last), jnp.exp2(bb_last))
            for hd in range(GLA_HEADS):
                hs = slice(hd * hk, (hd + 1) * hk)
                for g, val in enumerate(operands):
                    lane0 = (hd * N_GLA_OPERANDS + g) * hk
                    gl_ref[gr, lane0:lane0 + hk] = val[:, hs].astype(BF16)
                for g, val in enumerate(decays):
                    lane0 = (hd * len(decays) + g) * hk
                    d_ref[n:n + 1, lane0:lane0 + hk] = val[:, hs]

    zg = proj(5)

    vs = _gelu_exact(proj(4))
    mu = jnp.mean(vs, axis=-1, keepdims=True)
    vc = vs - mu
    var = jnp.mean(vc * vc, axis=-1, keepdims=True)
    vsn = ((vc * lax.rsqrt(var + EPS)) * lng_ref[...] + lnb_ref[...]).astype(BF16)

    gz_ref[...] = _silu(proj(2)).astype(BF16)

    n_pos = tm // GMLP_CHUNK
    for g in range(GMLP_GROUPS):
        cs = slice(g * GMLP_GC, (g + 1) * GMLP_GC)
        rhs = jnp.concatenate([vsn[n * GMLP_CHUNK:(n + 1) * GMLP_CHUNK, cs] for n in range(n_pos)], axis=1)
        mixed = _dot(ws_ref[g], rhs)
        for n in range(n_pos):
            sg_scr[n * GMLP_CHUNK:(n + 1) * GMLP_CHUNK, cs] = (
                mixed[:, n * GMLP_GC:(n + 1) * GMLP_GC] + bsm_ref[:, cs])
    merge_gate = jax.nn.sigmoid(proj(7))

    y_gmlp = None
    for kt in range(D // MXU_K):
        ks = slice(kt * MXU_K, (kt + 1) * MXU_K)
        g2 = ((_gelu_exact(u_pre[:, ks]) * sg_scr[:, ks]) * _silu(zg[:, ks])).astype(BF16)
        part = _dot(g2, wbg_ref[ks, :])
        y_gmlp = part if y_gmlp is None else y_gmlp + part
    mg_ref[...] = (merge_gate * y_gmlp).astype(BF16)


def _projection(x2, mod3, norm_g, w_main, w_ra, wa, ba, ln_g, ln_b, ws, bsm, w_br_gmlp, seq):
    T, D = x2.shape
    tm = PROJ_TM
    assert T % tm == 0 and seq % tm == 0 and tm % GMLP_CHUNK == 0 and tm % GLA_BLK == 0
    resident = functools.partial(pl.BlockSpec, pipeline_mode=pl.Buffered(1))
    tok = lambda w: pl.BlockSpec((tm, w), lambda i: (i, 0))
    tiles_per_seq = seq // tm
    out_bf = jax.ShapeDtypeStruct((T, D), BF16)
    return pl.pallas_call(
        _proj_kernel,
        grid=(T // tm,),
        in_specs=[
            tok(D),
            pl.BlockSpec((1, 3, D), lambda i: (i // tiles_per_seq, 0, 0)),
            resident((1, D), lambda i: (0, 0)),
            resident(w_main.shape, lambda i: (0, 0)),
            resident(w_ra.shape, lambda i: (0, 0)),
            resident(wa.shape, lambda i: (0, 0)),
            resident(ba.shape, lambda i: (0, 0)),
            resident((1, D), lambda i: (0, 0)),
            resident((1, D), lambda i: (0, 0)),
            resident(ws.shape, lambda i: (0, 0, 0)),
            resident(bsm.shape, lambda i: (0, 0)),
            resident(w_br_gmlp.shape, lambda i: (0, 0)),
        ],
        out_specs=[tok(N_GLA_OPERANDS * GLA_DK), pl.BlockSpec((tm // GLA_CHUNK, 2 * GLA_DK), lambda i: (i, 0)),
                   tok(D), tok(D), tok(D), tok(D)],
        out_shape=[jax.ShapeDtypeStruct((T, N_GLA_OPERANDS * GLA_DK), BF16),
                   jax.ShapeDtypeStruct((T // GLA_CHUNK, 2 * GLA_DK), F32),
                   out_bf, out_bf, out_bf, out_bf],
        scratch_shapes=[pltpu.VMEM((tm, D), F32)],
        compiler_params=pltpu.CompilerParams(dimension_semantics=("arbitrary",),
                                             vmem_limit_bytes=VMEM_LIMIT_BYTES),
        name="projection_gmlp",
    )(x2, mod3, norm_g, w_main, w_ra, wa, ba, ln_g, ln_b, ws, bsm, w_br_gmlp)


def _gla_kernel(gl_ref, v_ref, d_ref, gz_ref, gng_ref, o_ref, of_scr, ob_scr, kvf_scr, kvb_scr):
    S = v_ref.shape[1]
    C = GLA_CHUNK
    hk = GLA_HK
    n_blk = S // GLA_BLK
    ti, si = _block_iotas()
    chunk_lo = (ti >> CHUNK_SHIFT) << CHUNK_SHIFT
    chunk_hi = chunk_lo + (C - 1)
    pos_chunk = lax.broadcasted_iota(jnp.int32, (hk, GLA_BLK), 1) >> CHUNK_SHIFT
    chunk_sel = [jnp.where(pos_chunk == c, 1.0, 0.0).astype(BF16) for c in range(CHUNKS_PER_BLK)]
    d_cols = d_ref[0].T
    fwd = (0 * hk, 1 * hk, 2 * hk, d_cols[:hk], of_scr, kvf_scr)
    bwd = (3 * hk, 4 * hk, 5 * hk, d_cols[hk:], ob_scr, kvb_scr)

    def operand(rows, lane0):
        return gl_ref[0, rows, lane0:lane0 + hk]

    def scores(j, operands, forward):
        rows = _aligned_rows(j * GLA_BLK, GLA_BLK)
        a = _dot_nt(operand(rows, operands[0]), operand(rows, operands[1]))
        if forward:
            a = jnp.where(si <= ti, jnp.where(si >= chunk_lo, a, 0.0), 0.0)
        else:
            a = jnp.where(si > ti, jnp.where(si <= chunk_hi, a, 0.0), 0.0)
        return a.astype(BF16)

    def chunk_states(j, operands):
        kv_scr = operands[5]
        rows = _aligned_rows(j * GLA_BLK, GLA_BLK)
        ke = operand(rows, operands[2])
        ke_t = ke.T
        ke_bd = jnp.concatenate([ke_t * chunk_sel[c] for c in range(CHUNKS_PER_BLK)], axis=0)
        kv_scr[j] = _dot(ke_bd, v_ref[0, rows, :])

    def weighted_values(j, a, operands):
        rows = _aligned_rows(j * GLA_BLK, GLA_BLK)
        operands[4][rows, :] = _dot(a, v_ref[0, rows, :])

    def scan(j, state, operands, forward):
        qt_lane, _, _, decays, o_scr, kv_scr = operands
        o_inter = [None] * CHUNKS_PER_BLK
        for c in (range(CHUNKS_PER_BLK) if forward else reversed(range(CHUNKS_PER_BLK))):
            rows = _aligned_rows(j * GLA_BLK + c * C, C)
            n = j * CHUNKS_PER_BLK + c
            o_inter[c] = _dot(operand(rows, qt_lane), state.astype(BF16))
            state = state * decays[:, n:n + 1] + kv_scr[j, c * hk:(c + 1) * hk, :]
        o_scr[_aligned_rows(j * GLA_BLK, GLA_BLK), :] += jnp.concatenate(o_inter, axis=0)
        return state

    def finish(j):
        rows = _aligned_rows(j * GLA_BLK, GLA_BLK)
        o = of_scr[rows, :] + ob_scr[rows, :]
        on = o * lax.rsqrt(jnp.mean(o * o, axis=-1, keepdims=True) + EPS) * gng_ref[0]
        o_ref[0, rows, :] = (on * gz_ref[0, rows, :].astype(F32)).astype(BF16)

    sf = sb = jnp.zeros((hk, GLA_HV), F32)
    for i in range(-1, n_blk):
        jf, jb = i, n_blk - 1 - i
        prepare = i + 1 < n_blk
        if prepare:
            a_f = scores(jf + 1, fwd, True)
            a_b = scores(jb - 1, bwd, False)
        if i >= 0:
            sf = scan(jf, sf, fwd, True)
            sb = scan(jb, sb, bwd, False)
        if prepare:
            chunk_states(jf + 1, fwd)
            chunk_states(jb - 1, bwd)
            weighted_values(jf + 1, a_f, fwd)
            weighted_values(jb - 1, a_b, bwd)
        for j in sorted({jf, jb}):
            if i >= 0 and max(j, n_blk - 1 - j) == i:
                finish(j)


def _gla(gl3, v3, d3, gz3, gng):
    B, S, D = v3.shape
    H, hk, hv = GLA_HEADS, GLA_HK, GLA_HV
    n_chunks = S // GLA_CHUNK
    assert S % GLA_BLK == 0
    head_v = pl.BlockSpec((1, S, hv), lambda b, h: (b, 0, h))
    kv_scratch = pltpu.VMEM((S // GLA_BLK, CHUNKS_PER_BLK * hk, hv), F32)
    return pl.pallas_call(
        _gla_kernel,
        grid=(B, H),
        in_specs=[
            pl.BlockSpec((1, S, N_GLA_OPERANDS * hk), lambda b, h: (b, 0, h)),
            head_v,
            pl.BlockSpec((1, n_chunks, 2 * hk), lambda b, h: (b, 0, h)),
            head_v,
            pl.BlockSpec((1, 1, hv), lambda b, h: (h, 0, 0)),
        ],
        out_specs=head_v,
        out_shape=jax.ShapeDtypeStruct((B, S, D), BF16),
        scratch_shapes=[pltpu.VMEM((S, hv), F32), pltpu.VMEM((S, hv), F32), kv_scratch, kv_scratch],
        compiler_params=pltpu.CompilerParams(dimension_semantics=("arbitrary", "arbitrary"),
                                             vmem_limit_bytes=VMEM_LIMIT_BYTES),
        name="gla_bidir",
    )(gl3, v3, d3, gz3, gng)


def _out_kernel(x_ref, g1_ref, sgl_ref, mg_ref, mod_ref, wbr_ref, wout_ref, fg_ref, o_ref):
    gate = mod_ref[0, 2:3, :]
    n_sub = x_ref.shape[0] // OUT_SUB
    subs = [slice(s * OUT_SUB, (s + 1) * OUT_SUB) for s in range(n_sub)]
    y_gla = [None] * n_sub
    y_gla[0] = _dot(g1_ref[subs[0], :], wbr_ref[...])
    for s, rows in enumerate(subs):
        if s + 1 < n_sub:
            y_gla[s + 1] = _dot(g1_ref[subs[s + 1], :], wbr_ref[...])
        merged = (sgl_ref[rows, :].astype(F32) * y_gla[s] + mg_ref[rows, :].astype(F32)).astype(BF16)
        r = x_ref[rows, :] + gate * _dot(merged, wout_ref[...])
        o_ref[rows, :] = (r * lax.rsqrt(jnp.mean(r * r, axis=-1, keepdims=True) + EPS)) * fg_ref[...]


def _merge_out(x2, g1, sgl, mg, mod3, w_br_gla, w_out, final_g, seq):
    T, D = x2.shape
    tm = OUT_TM
    assert T % tm == 0 and seq % tm == 0
    resident = functools.partial(pl.BlockSpec, pipeline_mode=pl.Buffered(1))
    tok = pl.BlockSpec((tm, D), lambda i: (i, 0))
    tiles_per_seq = seq // tm
    return pl.pallas_call(
        _out_kernel,
        grid=(T // tm,),
        in_specs=[
            tok, tok, tok, tok,
            pl.BlockSpec((1, 3, D), lambda i: (i // tiles_per_seq, 0, 0)),
            resident((D, D), lambda i: (0, 0)),
            resident((D, D), lambda i: (0, 0)),
            resident((1, D), lambda i: (0, 0)),
        ],
        out_specs=tok,
        out_shape=jax.ShapeDtypeStruct((T, D), F32),
        compiler_params=pltpu.CompilerParams(dimension_semantics=("arbitrary",),
                                             vmem_limit_bytes=VMEM_LIMIT_BYTES),
        name="merge_out",
    )(x2, g1, sgl, mg, mod3, w_br_gla, w_out, final_g)


def _layer(x, c, norm_g, w_ada, b_ada, w_in, alpha_fw_w, alpha_fw_b, alpha_bw_w, alpha_bw_b,
           gla_norm_g, gmlp_ln_g, gmlp_ln_b, gmlp_ws, gmlp_bs, w_br_gla, w_br_gmlp, w_out, out_g):
    B, S, D = x.shape
    H, hv, R, DK = GLA_HEADS, GLA_HV, GLA_RANK, GLA_DK
    T = B * S

    o_ra = 2 * DK + 2 * D
    assert o_ra == N_LO_GROUPS * D
    w_t = w_in.T
    w_main = _weight_prep(w_t)
    w_ra = jnp.pad(w_t[o_ra:o_ra + 2 * R], ((0, RA_PAD - 2 * R), (0, 0))).astype(BF16)
    zeros = jnp.zeros_like(alpha_fw_w)
    wa = jnp.concatenate([jnp.concatenate([alpha_fw_w, zeros], axis=1),
                          jnp.concatenate([zeros, alpha_bw_w], axis=1)], axis=0)
    wa = jnp.pad(wa, ((0, RA_PAD - 2 * R), (0, 0))).astype(BF16)
    ba = jnp.concatenate([alpha_fw_b, alpha_bw_b]).reshape(1, 2 * DK)
    bsm = jnp.repeat(gmlp_bs.T, GMLP_GC, axis=1)

    mod3 = _adaln_mod(c, w_ada, b_ada).reshape(B, 3, D)
    x2 = x.reshape(T, D)
    gl, d, v, gz, sgl, mg = _projection(
        x2, mod3, norm_g.reshape(1, D), w_main, w_ra, wa, ba, gmlp_ln_g.reshape(1, D), gmlp_ln_b.reshape(1, D),
        gmlp_ws.astype(BF16), bsm, w_br_gmlp.astype(BF16), S)
    g1 = _gla(gl.reshape(B, S, N_GLA_OPERANDS * DK), v.reshape(B, S, D), d.reshape(B, S // GLA_CHUNK, 2 * DK),
              gz.reshape(B, S, D), gla_norm_g.reshape(H, 1, hv))
    out = _merge_out(x2, g1.reshape(T, D), sgl, mg, mod3, w_br_gla.astype(BF16), w_out.astype(BF16),
                     out_g.reshape(1, D), S)
    return out.reshape(B, S, D)


def kernel(x, c, norm_g, w_ada, b_ada, w_in, alpha_fw_w, alpha_fw_b, alpha_bw_w, alpha_bw_b, gla_norm_g,
           gmlp_ln_g, gmlp_ln_b, gmlp_ws, gmlp_bs, w_br_gla, w_br_gmlp, w_out, final_g):
    depth = norm_g.shape[0]
    assert depth == 1, "the final rmsnorm is fused into the layer's output kernel"
    return _layer(x, c, norm_g[0], w_ada[0], b_ada[0], w_in[0], alpha_fw_w[0], alpha_fw_b[0], alpha_bw_w[0],
                  alpha_bw_b[0], gla_norm_g[0], gmlp_ln_g[0], gmlp_ln_b[0], gmlp_ws[0], gmlp_bs[0],
                  w_br_gla[0], w_br_gmlp[0], w_out[0], final_g)
```

```python
import functools

import jax
import jax.numpy as jnp
from jax import lax
from jax.experimental import pallas as pl
from jax.experimental.pallas import tpu as pltpu

F32 = jnp.float32
BF16 = jnp.bfloat16

D_MODEL = 1024
GLA_HEADS = 4
GLA_HK = 128
GLA_HV = 256
GLA_DK = GLA_HEADS * GLA_HK
GLA_RANK = 16
GLA_TAU = 16.0
GLA_CHUNK = 64
LOG_DECAY_FLOOR = -1.25
GMLP_GROUPS = 8
GMLP_GC = 128
GMLP_CHUNK = 128
EPS = 1e-6
SQRT_HALF = 0.7071067811865476
LOG2_E = 1.4426950408889634

RA_PAD = 128
MXU_K = 256
N_LO_GROUPS = 3
VMEM_LIMIT_BYTES = 56 * 1024 * 1024

PROJ_TM = 512
OUT_TM = 1024
OUT_SUB = 256
GLA_BLK = 256
CHUNKS_PER_BLK = GLA_BLK // GLA_CHUNK
CHUNK_SHIFT = GLA_CHUNK.bit_length() - 1
N_GLA_OPERANDS = 6


def _silu(z):
    return z * jax.nn.sigmoid(z)


def _gelu_exact(z):
    return 0.5 * z * (1.0 + lax.erf(z * SQRT_HALF))


def _log_sigmoid(z):
    return jnp.minimum(z, 0.0) - jnp.log1p(jnp.exp(-jnp.abs(z)))


def _dot(a, b):
    return jnp.dot(a, b, preferred_element_type=F32)


def _dot_nt(a, b):
    return lax.dot_general(a, b, (((1,), (1,)), ((), ())), preferred_element_type=F32)


def _dot_tn(a, b):
    return lax.dot_general(a, b, (((0,), (0,)), ((), ())), preferred_element_type=F32)


def _aligned_rows(start, size):
    if isinstance(start, int):
        return pl.ds(start, size)
    return pl.ds(pl.multiple_of(start, size), size)


def _block_iotas():
    ti = lax.broadcasted_iota(jnp.int32, (GLA_BLK, GLA_BLK), 0)
    si = lax.broadcasted_iota(jnp.int32, (GLA_BLK, GLA_BLK), 1)
    return ti, si


def _mod_kernel(c_ref, w_ref, b_ref, o_ref):
    a = _silu(c_ref[...]).astype(BF16)
    o_ref[...] = _dot(a, w_ref[...].astype(BF16)) + b_ref[...]


def _adaln_mod(c, w_ada, b_ada):
    B, D = c.shape
    n_out = w_ada.shape[1]
    return pl.pallas_call(
        _mod_kernel,
        grid=(n_out // D,),
        in_specs=[
            pl.BlockSpec((B, D), lambda j: (0, 0)),
            pl.BlockSpec((D, D), lambda j: (0, j)),
            pl.BlockSpec((1, D), lambda j: (0, j)),
        ],
        out_specs=pl.BlockSpec((B, D), lambda j: (0, j)),
        out_shape=jax.ShapeDtypeStruct((B, n_out), F32),
        compiler_params=pltpu.CompilerParams(dimension_semantics=("arbitrary",)),
        name="adaln_mod",
    )(c, w_ada, b_ada.reshape(1, n_out))


def _weight_prep_kernel(a_ref, b_ref, o_ref):
    skip = 2 * GLA_RANK

    @pl.when(pl.program_id(0) < N_LO_GROUPS)
    def _():
        o_ref[...] = a_ref[...].T.astype(BF16)

    @pl.when(pl.program_id(0) >= N_LO_GROUPS)
    def _():
        o_ref[...] = jnp.concatenate([a_ref[skip:, :], b_ref[...]], axis=0).T.astype(BF16)


def _weight_prep(w_t):
    K = w_t.shape[1]
    D = D_MODEL
    skip = 2 * GLA_RANK
    n_groups = (w_t.shape[0] - skip) // D
    return pl.pallas_call(
        _weight_prep_kernel,
        grid=(n_groups,),
        in_specs=[
            pl.BlockSpec((D, K), lambda g: (g, 0)),
            pl.BlockSpec((skip, K), lambda g: ((g + 1) * (D // skip), 0)),
        ],
        out_specs=pl.BlockSpec((K, D), lambda g: (0, g)),
        out_shape=jax.ShapeDtypeStruct((K, n_groups * D), BF16),
        compiler_params=pltpu.CompilerParams(dimension_semantics=("arbitrary",),
                                             vmem_limit_bytes=VMEM_LIMIT_BYTES),
        name="weight_prep",
    )(w_t, w_t)


def _proj_kernel(x_ref, mod_ref, ng_ref, wm_ref, wra_ref, wa_ref, ba_ref, lng_ref, lnb_ref, ws_ref, bsm_ref,
                 wbg_ref, gl_ref, d_ref, v_ref, gz_ref, sgl_ref, mg_ref, sg_scr):
    D = D_MODEL
    DK = GLA_DK
    hk = GLA_HK
    C = GLA_CHUNK
    tm = x_ref.shape[0]
    half = tm // 2
    shift = mod_ref[0, 0:1, :]
    gain = ng_ref[...] * (1.0 + mod_ref[0, 1:2, :])

    def normed(rows):
        x = x_ref[rows, :]
        r = lax.rsqrt(jnp.mean(x * x, axis=-1, keepdims=True) + EPS)
        return ((x * r) * gain + shift).astype(BF16)

    h_top = normed(slice(0, half))
    h_bot = normed(slice(half, tm))
    h = jnp.concatenate([h_top, h_bot], axis=0)

    def proj(j, lhs=None):
        return _dot(h if lhs is None else lhs, wm_ref[:, j * D:(j + 1) * D])


    qk = jnp.concatenate([proj(0, h_top), proj(0, h_bot)], axis=0)
    ra = jnp.concatenate([_dot_nt(h_top, wra_ref[...]), _dot_nt(h_bot, wra_ref[...])], axis=0).astype(BF16)
    pre = _dot(ra, wa_ref[...]) + ba_ref[...]
    la = jnp.maximum(_log_sigmoid(pre) * (1.0 / GLA_TAU), LOG_DECAY_FLOOR) * LOG2_E

    v_ref[...] = proj(1).astype(BF16)

    hi = la.astype(BF16)
    r1 = la - hi.astype(F32)
    mid = r1.astype(BF16)
    lo = (r1 - mid.astype(F32)).astype(BF16)

    u_pre = proj(3)

    ti, si = _block_iotas()
    same_chunk = (ti >> CHUNK_SHIFT) == (si >> CHUNK_SHIFT)
    tri_f = jnp.where(same_chunk & (si <= ti), 1.0, 0.0).astype(BF16)
    tri_b = jnp.where(same_chunk & (si >= ti), 1.0, 0.0).astype(BF16)
    cum = []
    for blk in range(tm // GLA_BLK):
        rs = slice(blk * GLA_BLK, (blk + 1) * GLA_BLK)
        b_f = _dot(tri_f, hi[rs, :DK]) + _dot(tri_f, mid[rs, :DK]) + _dot(tri_f, lo[rs, :DK])
        b_b = _dot(tri_b, hi[rs, DK:]) + _dot(tri_b, mid[rs, DK:]) + _dot(tri_b, lo[rs, DK:])
        cum.append((b_f, b_b))

    sgl_ref[...] = jax.nn.sigmoid(proj(6)).astype(BF16)

    q_scale = GLA_HK ** -0.5
    for blk in range(tm // GLA_BLK):
        b_f, b_b = cum[blk]
        for c in range(CHUNKS_PER_BLK):
            cs = slice(c * C, (c + 1) * C)
            gr = slice(blk * GLA_BLK + c * C, blk * GLA_BLK + (c + 1) * C)
            n = blk * CHUNKS_PER_BLK + c
            q = qk[gr, :DK] * q_scale
            k = qk[gr, DK:]
            bf = b_f[cs]
            bb = b_b[cs]
            bf_last = bf[C - 1:C]
            bb_last = bb[0:1]
            operands = (q * jnp.exp2(bf), k * jnp.exp2(-bf), k * jnp.exp2(bf_last - bf),
                        q * jnp.exp2(bb), k * jnp.exp2(-bb), k * jnp.exp2(bb_last - bb))
            decays = (jnp.exp2(bf_last), jnp.exp2(bb_last))
            for hd in range(GLA_HEADS):
                hs = slice(hd * hk, (hd + 1) * hk)
                for g, val in enumerate(operands):
                    lane0 = (hd * N_GLA_OPERANDS + g) * hk
                    gl_ref[gr, lane0:lane0 + hk] = val[:, hs].astype(BF16)
                for g, val in enumerate(decays):
                    lane0 = (hd * len(decays) + g) * hk
                    d_ref[n:n + 1, lane0:lane0 + hk] = val[:, hs]

    zg = proj(5)

    vs = _gelu_exact(proj(4))
    mu = jnp.mean(vs, axis=-1, keepdims=True)
    vc = vs - mu
    var = jnp.mean(vc * vc, axis=-1, keepdims=True)
    vsn = ((vc * lax.rsqrt(var + EPS)) * lng_ref[...] + lnb_ref[...]).astype(BF16)

    gz_ref[...] = _silu(proj(2)).astype(BF16)

    n_pos = tm // GMLP_CHUNK
    for g in range(GMLP_GROUPS):
        cs = slice(g * GMLP_GC, (g + 1) * GMLP_GC)
        rhs = jnp.concatenate([vsn[n * GMLP_CHUNK:(n + 1) * GMLP_CHUNK, cs] for n in range(n_pos)], axis=1)
        mixed = _dot(ws_ref[g], rhs)
        for n in range(n_pos):
            sg_scr[n * GMLP_CHUNK:(n + 1) * GMLP_CHUNK, cs] = (
                mixed[:, n * GMLP_GC:(n + 1) * GMLP_GC] + bsm_ref[:, cs])
    merge_gate = jax.nn.sigmoid(proj(7))

    y_gmlp = None
    for kt in range(D // MXU_K):
        ks = slice(kt * MXU_K, (kt + 1) * MXU_K)
        g2 = ((_gelu_exact(u_pre[:, ks]) * sg_scr[:, ks]) * _silu(zg[:, ks])).astype(BF16)
        part = _dot(g2, wbg_ref[ks, :])
        y_gmlp = part if y_gmlp is None else y_gmlp + part
    mg_ref[...] = (merge_gate * y_gmlp).astype(BF16)


def _projection(x2, mod3, norm_g, w_main, w_ra, wa, ba, ln_g, ln_b, ws, bsm, w_br_gmlp, seq):
    T, D = x2.shape
    tm = PROJ_TM
    assert T % tm == 0 and seq % tm == 0 and tm % GMLP_CHUNK == 0 and tm % GLA_BLK == 0
    resident = functools.partial(pl.BlockSpec, pipeline_mode=pl.Buffered(1))
    tok = lambda w: pl.BlockSpec((tm, w), lambda i: (i, 0))
    tiles_per_seq = seq // tm
    out_bf = jax.ShapeDtypeStruct((T, D), BF16)
    return pl.pallas_call(
        _proj_kernel,
        grid=(T // tm,),
        in_specs=[
            tok(D),
            pl.BlockSpec((1, 3, D), lambda i: (i // tiles_per_seq, 0, 0)),
            resident((1, D), lambda i: (0, 0)),
            resident(w_main.shape, lambda i: (0, 0)),
            resident(w_ra.shape, lambda i: (0, 0)),
            resident(wa.shape, lambda i: (0, 0)),
            resident(ba.shape, lambda i: (0, 0)),
            resident((1, D), lambda i: (0, 0)),
            resident((1, D), lambda i: (0, 0)),
            resident(ws.shape, lambda i: (0, 0, 0)),
            resident(bsm.shape, lambda i: (0, 0)),
            resident(w_br_gmlp.shape, lambda i: (0, 0)),
        ],
        out_specs=[tok(N_GLA_OPERANDS * GLA_DK), pl.BlockSpec((tm // GLA_CHUNK, 2 * GLA_DK), lambda i: (i, 0)),
                   tok(D), tok(D), tok(D), tok(D)],
        out_shape=[jax.ShapeDtypeStruct((T, N_GLA_OPERANDS * GLA_DK), BF16),
                   jax.ShapeDtypeStruct((T // GLA_CHUNK, 2 * GLA_DK), F32),
                   out_bf, out_bf, out_bf, out_bf],
        scratch_shapes=[pltpu.VMEM((tm, D), F32)],
        compiler_params=pltpu.CompilerParams(dimension_semantics=("arbitrary",),
                                             vmem_limit_bytes=VMEM_LIMIT_BYTES),
        name="projection_gmlp",
    )(x2, mod3, norm_g, w_main, w_ra, wa, ba, ln_g, ln_b, ws, bsm, w_br_gmlp)


def _gla_kernel(gl_ref, v_ref, d_ref, gz_ref, gng_ref, o_ref, of_scr, ob_scr, kvf_scr, kvb_scr):
    S = v_ref.shape[1]
    C = GLA_CHUNK
    hk = GLA_HK
    n_blk = S // GLA_BLK
    ti, si = _block_iotas()
    chunk_lo = (ti >> CHUNK_SHIFT) << CHUNK_SHIFT
    chunk_hi = chunk_lo + (C - 1)
    pos_chunk = lax.broadcasted_iota(jnp.int32, (hk, GLA_BLK), 1) >> CHUNK_SHIFT
    chunk_sel = [jnp.where(pos_chunk == c, 1.0, 0.0).astype(BF16) for c in range(CHUNKS_PER_BLK)]
    d_cols = d_ref[0].T
    fwd = (0 * hk, 1 * hk, 2 * hk, d_cols[:hk], of_scr, kvf_scr)
    bwd = (3 * hk, 4 * hk, 5 * hk, d_cols[hk:], ob_scr, kvb_scr)

    def operand(rows, lane0):
        return gl_ref[0, rows, lane0:lane0 + hk]

    def scores(j, operands, forward):
        rows = _aligned_rows(j * GLA_BLK, GLA_BLK)
        a = _dot_nt(operand(rows, operands[0]), operand(rows, operands[1]))
        if forward:
            a = jnp.where(si <= ti, jnp.where(si >= chunk_lo, a, 0.0), 0.0)
        else:
            a = jnp.where(si > ti, jnp.where(si <= chunk_hi, a, 0.0), 0.0)
        return a.astype(BF16)

    def chunk_states(j, operands):
        kv_scr = operands[5]
        rows = _aligned_rows(j * GLA_BLK, GLA_BLK)
        ke = operand(rows, operands[2])
        ke_t = ke.T
        ke_bd = jnp.concatenate([ke_t * chunk_sel[c] for c in range(CHUNKS_PER_BLK)], axis=0)
        kv_scr[j] = _dot(ke_bd, v_ref[0, rows, :])

    def weighted_values(j, a, operands):
        rows = _aligned_rows(j * GLA_BLK, GLA_BLK)
        operands[4][rows, :] = _dot(a, v_ref[0, rows, :])

    def scan(j, state, operands, forward):
        qt_lane, _, _, decays, o_scr, kv_scr = operands
        o_inter = [None] * CHUNKS_PER_BLK
        for c in (range(CHUNKS_PER_BLK) if forward else reversed(range(CHUNKS_PER_BLK))):
            rows = _aligned_rows(j * GLA_BLK + c * C, C)
            n = j * CHUNKS_PER_BLK + c
            o_inter[c] = _dot(operand(rows, qt_lane), state.astype(BF16))
            state = state * decays[:, n:n + 1] + kv_scr[j, c * hk:(c + 1) * hk, :]
        o_scr[_aligned_rows(j * GLA_BLK, GLA_BLK), :] += jnp.concatenate(o_inter, axis=0)
        return state

    def finish(j):
        rows = _aligned_rows(j * GLA_BLK, GLA_BLK)
        o = of_scr[rows, :] + ob_scr[rows, :]
        on = o * lax.rsqrt(jnp.mean(o * o, axis=-1, keepdims=True) + EPS) * gng_ref[0]
        o_ref[0, rows, :] = (on * gz_ref[0, rows, :].astype(F32)).astype(BF16)

    sf = sb = jnp.zeros((hk, GLA_HV), F32)
    for i in range(-1, n_blk):
        jf, jb = i, n_blk - 1 - i
        prepare = i + 1 < n_blk
        if prepare:
            a_f = scores(jf + 1, fwd, True)
            a_b = scores(jb - 1, bwd, False)
        if i >= 0:
            sf = scan(jf, sf, fwd, True)
            sb = scan(jb, sb, bwd, False)
        if prepare:
            chunk_states(jf + 1, fwd)
            chunk_states(jb - 1, bwd)
            weighted_values(jf + 1, a_f, fwd)
            weighted_values(jb - 1, a_b, bwd)
        for j in sorted({jf, jb}):
            if i >= 0 and max(j, n_blk - 1 - j) == i:
                finish(j)


def _gla(gl3, v3, d3, gz3, gng):
    B, S, D = v3.shape
    H, hk, hv = GLA_HEADS, GLA_HK, GLA_HV
    n_chunks = S // GLA_CHUNK
    assert S % GLA_BLK == 0
    head_v = pl.BlockSpec((1, S, hv), lambda b, h: (b, 0, h))
    kv_scratch = pltpu.VMEM((S // GLA_BLK, CHUNKS_PER_BLK * hk, hv), F32)
    return pl.pallas_call(
        _gla_kernel,
        grid=(B, H),
        in_specs=[
            pl.BlockSpec((1, S, N_GLA_OPERANDS * hk), lambda b, h: (b, 0, h)),
            head_v,
            pl.BlockSpec((1, n_chunks, 2 * hk), lambda b, h: (b, 0, h)),
            head_v,
            pl.BlockSpec((1, 1, hv), lambda b, h: (h, 0, 0)),
        ],
        out_specs=head_v,
        out_shape=jax.ShapeDtypeStruct((B, S, D), BF16),
        scratch_shapes=[pltpu.VMEM((S, hv), F32), pltpu.VMEM((S, hv), F32), kv_scratch, kv_scratch],
        compiler_params=pltpu.CompilerParams(dimension_semantics=("arbitrary", "arbitrary"),
                                             vmem_limit_bytes=VMEM_LIMIT_BYTES),
        name="gla_bidir",
    )(gl3, v3, d3, gz3, gng)


def _out_kernel(x_ref, g1_ref, sgl_ref, mg_ref, mod_ref, wbr_ref, wout_ref, fg_ref, o_ref):
    gate = mod_ref[0, 2:3, :]
    n_sub = x_ref.shape[0] // OUT_SUB
    subs = [slice(s * OUT_SUB, (s + 1) * OUT_SUB) for s in range(n_sub)]
    y_gla = [None] * n_sub
    y_gla[0] = _dot(g1_ref[subs[0], :], wbr_ref[...])
    for s, rows in enumerate(subs):
        if s + 1 < n_sub:
            y_gla[s + 1] = _dot(g1_ref[subs[s + 1], :], wbr_ref[...])
        merged = (sgl_ref[rows, :].astype(F32) * y_gla[s] + mg_ref[rows, :].astype(F32)).astype(BF16)
        r = x_ref[rows, :] + gate * _dot(merged, wout_ref[...])
        o_ref[rows, :] = (r * lax.rsqrt(jnp.mean(r * r, axis=-1, keepdims=True) + EPS)) * fg_ref[...]


def _merge_out(x2, g1, sgl, mg, mod3, w_br_gla, w_out, final_g, seq):
    T, D = x2.shape
    tm = OUT_TM
    assert T % tm == 0 and seq % tm == 0
    resident = functools.partial(pl.BlockSpec, pipeline_mode=pl.Buffered(1))
    tok = pl.BlockSpec((tm, D), lambda i: (i, 0))
    tiles_per_seq = seq // tm
    return pl.pallas_call(
        _out_kernel,
        grid=(T // tm,),
        in_specs=[
            tok, tok, tok, tok,
            pl.BlockSpec((1, 3, D), lambda i: (i // tiles_per_seq, 0, 0)),
            resident((D, D), lambda i: (0, 0)),
            resident((D, D), lambda i: (0, 0)),
            resident((1, D), lambda i: (0, 0)),
        ],
        out_specs=tok,
        out_shape=jax.ShapeDtypeStruct((T, D), F32),
        compiler_params=pltpu.CompilerParams(dimension_semantics=("arbitrary",),
                                             vmem_limit_bytes=VMEM_LIMIT_BYTES),
        name="merge_out",
    )(x2, g1, sgl, mg, mod3, w_br_gla, w_out, final_g)


def _layer(x, c, norm_g, w_ada, b_ada, w_in, alpha_fw_w, alpha_fw_b, alpha_bw_w, alpha_bw_b,
           gla_norm_g, gmlp_ln_g, gmlp_ln_b, gmlp_ws, gmlp_bs, w_br_gla, w_br_gmlp, w_out, out_g):
    B, S, D = x.shape
    H, hv, R, DK = GLA_HEADS, GLA_HV, GLA_RANK, GLA_DK
    T = B * S

    o_ra = 2 * DK + 2 * D
    assert o_ra == N_LO_GROUPS * D
    w_t = w_in.T
    w_main = _weight_prep(w_t)
    w_ra = jnp.pad(w_t[o_ra:o_ra + 2 * R], ((0, RA_PAD - 2 * R), (0, 0))).astype(BF16)
    zeros = jnp.zeros_like(alpha_fw_w)
    wa = jnp.concatenate([jnp.concatenate([alpha_fw_w, zeros], axis=1),
                          jnp.concatenate([zeros, alpha_bw_w], axis=1)], axis=0)
    wa = jnp.pad(wa, ((0, RA_PAD - 2 * R), (0, 0))).astype(BF16)
    ba = jnp.concatenate([alpha_fw_b, alpha_bw_b]).reshape(1, 2 * DK)
    bsm = jnp.repeat(gmlp_bs.T, GMLP_GC, axis=1)

    mod3 = _adaln_mod(c, w_ada, b_ada).reshape(B, 3, D)
    x2 = x.reshape(T, D)
    gl, d, v, gz, sgl, mg = _projection(
        x2, mod3, norm_g.reshape(1, D), w_main, w_ra, wa, ba, gmlp_ln_g.reshape(1, D), gmlp_ln_b.reshape(1, D),
        gmlp_ws.astype(BF16), bsm, w_br_gmlp.astype(BF16), S)
    g1 = _gla(gl.reshape(B, S, N_GLA_OPERANDS * DK), v.reshape(B, S, D), d.reshape(B, S // GLA_CHUNK, 2 * DK),
              gz.reshape(B, S, D), gla_norm_g.reshape(H, 1, hv))
    out = _merge_out(x2, g1.reshape(T, D), sgl, mg, mod3, w_br_gla.astype(BF16), w_out.astype(BF16),
                     out_g.reshape(1, D), S)
    return out.reshape(B, S, D)


def kernel(x, c, norm_g, w_ada, b_ada, w_in, alpha_fw_w, alpha_fw_b, alpha_bw_w, alpha_bw_b, gla_norm_g,
           gmlp_ln_g, gmlp_ln_b, gmlp_ws, gmlp_bs, w_br_gla, w_br_gmlp, w_out, final_g):
    depth = norm_g.shape[0]
    assert depth == 1, "the final rmsnorm is fused into the layer's output kernel"
    return _layer(x, c, norm_g[0], w_ada[0], b_ada[0], w_in[0], alpha_fw_w[0], alpha_fw_b[0], alpha_bw_w[0],
                  alpha_bw_b[0], gla_norm_g[0], gmlp_ln_g[0], gmlp_ln_b[0], gmlp_ws[0], gmlp_bs[0],
                  w_br_gla[0], w_br_gmlp[0], w_out[0], final_g)
```

```python
import functools

import jax
import jax.numpy as jnp
from jax import lax
from jax.experimental import pallas as pl
from jax.experimental.pallas import tpu as pltpu

F32 = jnp.float32
BF16 = jnp.bfloat16

D_MODEL = 1024
GLA_HEADS = 4
GLA_HK = 128
GLA_HV = 256
GLA_DK = GLA_HEADS * GLA_HK
GLA_RANK = 16
GLA_TAU = 16.0
GLA_CHUNK = 64
LOG_DECAY_FLOOR = -1.25
GMLP_GROUPS = 8
GMLP_GC = 128
GMLP_CHUNK = 128
EPS = 1e-6
SQRT_HALF = 0.7071067811865476
LOG2_E = 1.4426950408889634

RA_PAD = 128
MXU_K = 256
N_LO_GROUPS = 3
VMEM_LIMIT_BYTES = 56 * 1024 * 1024

PROJ_TM = 512
OUT_TM = 1024
OUT_SUB = 256
GLA_BLK = 256
CHUNKS_PER_BLK = GLA_BLK // GLA_CHUNK
CHUNK_SHIFT = GLA_CHUNK.bit_length() - 1
N_GLA_OPERANDS = 6
GLA_HEADS_PER_STEP = 2


def _silu(z):
    return z * jax.nn.sigmoid(z)


def _gelu_exact(z):
    return 0.5 * z * (1.0 + lax.erf(z * SQRT_HALF))


def _log_sigmoid(z):
    return jnp.minimum(z, 0.0) - jnp.log1p(jnp.exp(-jnp.abs(z)))


def _dot(a, b):
    return jnp.dot(a, b, preferred_element_type=F32)


def _dot_nt(a, b):
    return lax.dot_general(a, b, (((1,), (1,)), ((), ())), preferred_element_type=F32)


def _dot_tn(a, b):
    return lax.dot_general(a, b, (((0,), (0,)), ((), ())), preferred_element_type=F32)


def _aligned_rows(start, size):
    if isinstance(start, int):
        return pl.ds(start, size)
    return pl.ds(pl.multiple_of(start, size), size)


def _block_iotas():
    ti = lax.broadcasted_iota(jnp.int32, (GLA_BLK, GLA_BLK), 0)
    si = lax.broadcasted_iota(jnp.int32, (GLA_BLK, GLA_BLK), 1)
    return ti, si


def _mod_kernel(c_ref, w_ref, b_ref, o_ref):
    a = _silu(c_ref[...]).astype(BF16)
    o_ref[...] = _dot(a, w_ref[...].astype(BF16)) + b_ref[...]


def _adaln_mod(c, w_ada, b_ada):
    B, D = c.shape
    n_out = w_ada.shape[1]
    return pl.pallas_call(
        _mod_kernel,
        grid=(n_out // D,),
        in_specs=[
            pl.BlockSpec((B, D), lambda j: (0, 0)),
            pl.BlockSpec((D, D), lambda j: (0, j)),
            pl.BlockSpec((1, D), lambda j: (0, j)),
        ],
        out_specs=pl.BlockSpec((B, D), lambda j: (0, j)),
        out_shape=jax.ShapeDtypeStruct((B, n_out), F32),
        compiler_params=pltpu.CompilerParams(dimension_semantics=("arbitrary",)),
        name="adaln_mod",
    )(c, w_ada, b_ada.reshape(1, n_out))


def _weight_prep_kernel(a_ref, b_ref, o_ref):
    skip = 2 * GLA_RANK

    @pl.when(pl.program_id(0) < N_LO_GROUPS)
    def _():
        o_ref[...] = a_ref[...].T.astype(BF16)

    @pl.when(pl.program_id(0) >= N_LO_GROUPS)
    def _():
        o_ref[...] = jnp.concatenate([a_ref[skip:, :], b_ref[...]], axis=0).T.astype(BF16)


def _weight_prep(w_t):
    K = w_t.shape[1]
    D = D_MODEL
    skip = 2 * GLA_RANK
    n_groups = (w_t.shape[0] - skip) // D
    return pl.pallas_call(
        _weight_prep_kernel,
        grid=(n_groups,),
        in_specs=[
            pl.BlockSpec((D, K), lambda g: (g, 0)),
            pl.BlockSpec((skip, K), lambda g: ((g + 1) * (D // skip), 0)),
        ],
        out_specs=pl.BlockSpec((K, D), lambda g: (0, g)),
        out_shape=jax.ShapeDtypeStruct((K, n_groups * D), BF16),
        compiler_params=pltpu.CompilerParams(dimension_semantics=("arbitrary",),
                                             vmem_limit_bytes=VMEM_LIMIT_BYTES),
        name="weight_prep",
    )(w_t, w_t)


def _proj_kernel(x_ref, mod_ref, ng_ref, wm_ref, wra_ref, wa_ref, ba_ref, lng_ref, lnb_ref, ws_ref, bsm_ref,
                 wbg_ref, gl_ref, d_ref, v_ref, gz_ref, sgl_ref, mg_ref, sg_scr):
    D = D_MODEL
    DK = GLA_DK
    hk = GLA_HK
    C = GLA_CHUNK
    tm = x_ref.shape[0]
    half = tm // 2
    shift = mod_ref[0, 0:1, :]
    gain = ng_ref[...] * (1.0 + mod_ref[0, 1:2, :])

    def normed(rows):
        x = x_ref[rows, :]
        r = lax.rsqrt(jnp.mean(x * x, axis=-1, keepdims=True) + EPS)
        return ((x * r) * gain + shift).astype(BF16)

    h_top = normed(slice(0, half))
    h_bot = normed(slice(half, tm))
    h = jnp.concatenate([h_top, h_bot], axis=0)

    def proj(j, lhs=None):
        return _dot(h if lhs is None else lhs, wm_ref[:, j * D:(j + 1) * D])


    qk = jnp.concatenate([proj(0, h_top), proj(0, h_bot)], axis=0)
    ra = jnp.concatenate([_dot_nt(h_top, wra_ref[...]), _dot_nt(h_bot, wra_ref[...])], axis=0).astype(BF16)
    pre = _dot(ra, wa_ref[...]) + ba_ref[...]
    la = jnp.maximum(_log_sigmoid(pre) * (1.0 / GLA_TAU), LOG_DECAY_FLOOR) * LOG2_E

    v_ref[...] = proj(1).astype(BF16)

    hi = la.astype(BF16)
    r1 = la - hi.astype(F32)
    mid = r1.astype(BF16)
    lo = (r1 - mid.astype(F32)).astype(BF16)

    u_pre = proj(3)

    ti, si = _block_iotas()
    same_chunk = (ti >> CHUNK_SHIFT) == (si >> CHUNK_SHIFT)
    tri_f = jnp.where(same_chunk & (si <= ti), 1.0, 0.0).astype(BF16)
    tri_b = jnp.where(same_chunk & (si >= ti), 1.0, 0.0).astype(BF16)
    cum = []
    for blk in range(tm // GLA_BLK):
        rs = slice(blk * GLA_BLK, (blk + 1) * GLA_BLK)
        b_f = _dot(tri_f, hi[rs, :DK]) + _dot(tri_f, mid[rs, :DK]) + _dot(tri_f, lo[rs, :DK])
        b_b = _dot(tri_b, hi[rs, DK:]) + _dot(tri_b, mid[rs, DK:]) + _dot(tri_b, lo[rs, DK:])
        cum.append((b_f, b_b))

    sgl_ref[...] = jax.nn.sigmoid(proj(6)).astype(BF16)

    q_scale = GLA_HK ** -0.5
    for blk in range(tm // GLA_BLK):
        b_f, b_b = cum[blk]
        for c in range(CHUNKS_PER_BLK):
            cs = slice(c * C, (c + 1) * C)
            gr = slice(blk * GLA_BLK + c * C, blk * GLA_BLK + (c + 1) * C)
            n = blk * CHUNKS_PER_BLK + c
            q = qk[gr, :DK] * q_scale
            k = qk[gr, DK:]
            bf = b_f[cs]
            bb = b_b[cs]
            bf_last = bf[C - 1:C]
            bb_last = bb[0:1]
            operands = (q * jnp.exp2(bf), k * jnp.exp2(-bf), k * jnp.exp2(bf_last - bf),
                        q * jnp.exp2(bb), k * jnp.exp2(-bb), k * jnp.exp2(bb_last - bb))
            decays = (jnp.exp2(bf_last), jnp.exp2(bb_last))
            for hd in range(GLA_HEADS):
                hs = slice(hd * hk, (hd + 1) * hk)
                for g, val in enumerate(operands):
                    lane0 = (hd * N_GLA_OPERANDS + g) * hk
                    gl_ref[gr, lane0:lane0 + hk] = val[:, hs].astype(BF16)
                for g, val in enumerate(decays):
                    lane0 = (hd * len(decays) + g) * hk
                    d_ref[n:n + 1, lane0:lane0 + hk] = val[:, hs]

    zg = proj(5)

    vs = _gelu_exact(proj(4))
    mu = jnp.mean(vs, axis=-1, keepdims=True)
    vc = vs - mu
    var = jnp.mean(vc * vc, axis=-1, keepdims=True)
    vsn = ((vc * lax.rsqrt(var + EPS)) * lng_ref[...] + lnb_ref[...]).astype(BF16)

    gz_ref[...] = _silu(proj(2)).astype(BF16)

    n_pos = tm // GMLP_CHUNK
    for g in range(GMLP_GROUPS):
        cs = slice(g * GMLP_GC, (g + 1) * GMLP_GC)
        rhs = jnp.concatenate([vsn[n * GMLP_CHUNK:(n + 1) * GMLP_CHUNK, cs] for n in range(n_pos)], axis=1)
        mixed = _dot(ws_ref[g], rhs)
        for n in range(n_pos):
            sg_scr[n * GMLP_CHUNK:(n + 1) * GMLP_CHUNK, cs] = (
                mixed[:, n * GMLP_GC:(n + 1) * GMLP_GC] + bsm_ref[:, cs])
    merge_gate = jax.nn.sigmoid(proj(7))

    y_gmlp = None
    for kt in range(D // MXU_K):
        ks = slice(kt * MXU_K, (kt + 1) * MXU_K)
        g2 = ((_gelu_exact(u_pre[:, ks]) * sg_scr[:, ks]) * _silu(zg[:, ks])).astype(BF16)
        part = _dot(g2, wbg_ref[ks, :])
        y_gmlp = part if y_gmlp is None else y_gmlp + part
    mg_ref[...] = (merge_gate * y_gmlp).astype(BF16)


def _projection(x2, mod3, norm_g, w_main, w_ra, wa, ba, ln_g, ln_b, ws, bsm, w_br_gmlp, seq):
    T, D = x2.shape
    tm = PROJ_TM
    assert T % tm == 0 and seq % tm == 0 and tm % GMLP_CHUNK == 0 and tm % GLA_BLK == 0
    resident = functools.partial(pl.BlockSpec, pipeline_mode=pl.Buffered(1))
    tok = lambda w: pl.BlockSpec((tm, w), lambda i: (i, 0))
    tiles_per_seq = seq // tm
    out_bf = jax.ShapeDtypeStruct((T, D), BF16)
    return pl.pallas_call(
        _proj_kernel,
        grid=(T // tm,),
        in_specs=[
            tok(D),
            pl.BlockSpec((1, 3, D), lambda i: (i // tiles_per_seq, 0, 0)),
            resident((1, D), lambda i: (0, 0)),
            resident(w_main.shape, lambda i: (0, 0)),
            resident(w_ra.shape, lambda i: (0, 0)),
            resident(wa.shape, lambda i: (0, 0)),
            resident(ba.shape, lambda i: (0, 0)),
            resident((1, D), lambda i: (0, 0)),
            resident((1, D), lambda i: (0, 0)),
            resident(ws.shape, lambda i: (0, 0, 0)),
            resident(bsm.shape, lambda i: (0, 0)),
            resident(w_br_gmlp.shape, lambda i: (0, 0)),
        ],
        out_specs=[tok(N_GLA_OPERANDS * GLA_DK), pl.BlockSpec((tm // GLA_CHUNK, 2 * GLA_DK), lambda i: (i, 0)),
                   tok(D), tok(D), tok(D), tok(D)],
        out_shape=[jax.ShapeDtypeStruct((T, N_GLA_OPERANDS * GLA_DK), BF16),
                   jax.ShapeDtypeStruct((T // GLA_CHUNK, 2 * GLA_DK), F32),
                   out_bf, out_bf, out_bf, out_bf],
        scratch_shapes=[pltpu.VMEM((tm, D), F32)],
        compiler_params=pltpu.CompilerParams(dimension_semantics=("arbitrary",),
                                             vmem_limit_bytes=VMEM_LIMIT_BYTES),
        name="projection_gmlp",
    )(x2, mod3, norm_g, w_main, w_ra, wa, ba, ln_g, ln_b, ws, bsm, w_br_gmlp)


def _gla_kernel(gl_ref, v_ref, d_ref, gz_ref, gng_ref, o_ref, of_scr, ob_scr, kvf_scr, kvb_scr):
    S = v_ref.shape[1]
    C = GLA_CHUNK
    hk = GLA_HK
    n_blk = S // GLA_BLK
    ti, si = _block_iotas()
    chunk_lo = (ti >> CHUNK_SHIFT) << CHUNK_SHIFT
    chunk_hi = chunk_lo + (C - 1)
    pos_chunk = lax.broadcasted_iota(jnp.int32, (hk, GLA_BLK), 1) >> CHUNK_SHIFT
    chunk_sel = [jnp.where(pos_chunk == c, 1.0, 0.0).astype(BF16) for c in range(CHUNKS_PER_BLK)]
    d_cols = d_ref[0].T
    hv = GLA_HV

    def direction(hd, forward):
        lane0 = (hd * N_GLA_OPERANDS + (0 if forward else N_GLA_OPERANDS // 2)) * hk
        row0 = (2 * hd + (0 if forward else 1)) * hk
        return (hd, lane0, lane0 + hk, lane0 + 2 * hk, d_cols[row0:row0 + hk],
                of_scr if forward else ob_scr, kvf_scr if forward else kvb_scr)

    def operand(rows, lane0):
        return gl_ref[0, rows, lane0:lane0 + hk]

    def values(hd, rows):
        return v_ref[0, rows, hd * hv:(hd + 1) * hv]

    def scores(j, ctx, forward):
        rows = _aligned_rows(j * GLA_BLK, GLA_BLK)
        a = _dot_nt(operand(rows, ctx[1]), operand(rows, ctx[2]))
        if forward:
            a = jnp.where(si <= ti, jnp.where(si >= chunk_lo, a, 0.0), 0.0)
        else:
            a = jnp.where(si > ti, jnp.where(si <= chunk_hi, a, 0.0), 0.0)
        return a.astype(BF16)

    def chunk_states(j, ctx):
        rows = _aligned_rows(j * GLA_BLK, GLA_BLK)
        ke = operand(rows, ctx[3])
        ke_t = ke.T
        ke_bd = jnp.concatenate([ke_t * chunk_sel[c] for c in range(CHUNKS_PER_BLK)], axis=0)
        ctx[6][j] = _dot(ke_bd, values(ctx[0], rows))

    def weighted_values(j, a, ctx):
        rows = _aligned_rows(j * GLA_BLK, GLA_BLK)
        ctx[5][ctx[0], rows, :] = _dot(a, values(ctx[0], rows))

    def scan(j, state, ctx, forward):
        hd, qt_lane, _, _, decays, o_scr, kv_scr = ctx
        o_inter = [None] * CHUNKS_PER_BLK
        for c in (range(CHUNKS_PER_BLK) if forward else reversed(range(CHUNKS_PER_BLK))):
            rows = _aligned_rows(j * GLA_BLK + c * C, C)
            n = j * CHUNKS_PER_BLK + c
            o_inter[c] = _dot(operand(rows, qt_lane), state.astype(BF16))
            state = state * decays[:, n:n + 1] + kv_scr[j, c * hk:(c + 1) * hk, :]
        o_scr[hd, _aligned_rows(j * GLA_BLK, GLA_BLK), :] += jnp.concatenate(o_inter, axis=0)
        return state

    def finish(hd, j):
        rows = _aligned_rows(j * GLA_BLK, GLA_BLK)
        lanes = slice(hd * hv, (hd + 1) * hv)
        o = of_scr[hd, rows, :] + ob_scr[hd, rows, :]
        on = o * lax.rsqrt(jnp.mean(o * o, axis=-1, keepdims=True) + EPS) * gng_ref[hd]
        o_ref[0, rows, lanes] = (on * gz_ref[0, rows, lanes].astype(F32)).astype(BF16)

    steps = [(hd, i) for hd in range(GLA_HEADS_PER_STEP) for i in range(n_blk)]
    sf = sb = None
    for cur, nxt in zip([None] + steps, steps + [None]):
        if nxt is not None:
            n_fwd, n_bwd = direction(nxt[0], True), direction(nxt[0], False)
            a_f = scores(nxt[1], n_fwd, True)
            a_b = scores(n_blk - 1 - nxt[1], n_bwd, False)
        if cur is not None:
            hd, i = cur
            if i == 0:
                sf = sb = jnp.zeros((hk, hv), F32)
            sf = scan(i, sf, direction(hd, True), True)
            sb = scan(n_blk - 1 - i, sb, direction(hd, False), False)
        if nxt is not None:
            chunk_states(nxt[1], n_fwd)
            chunk_states(n_blk - 1 - nxt[1], n_bwd)
            weighted_values(nxt[1], a_f, n_fwd)
            weighted_values(n_blk - 1 - nxt[1], a_b, n_bwd)
        if cur is not None:
            for j in sorted({i, n_blk - 1 - i}):
                if max(j, n_blk - 1 - j) == i:
                    finish(hd, j)


def _gla(gl3, v3, d3, gz3, gng):
    B, S, D = v3.shape
    H, hk, hv = GLA_HEADS, GLA_HK, GLA_HV
    n_chunks = S // GLA_CHUNK
    assert S % GLA_BLK == 0
    hps = GLA_HEADS_PER_STEP
    assert H % hps == 0
    head_v = pl.BlockSpec((1, S, hps * hv), lambda b, h: (b, 0, h))
    kv_scratch = pltpu.VMEM((S // GLA_BLK, CHUNKS_PER_BLK * hk, hv), F32)
    o_scratch = pltpu.VMEM((hps, S, hv), F32)
    return pl.pallas_call(
        _gla_kernel,
        grid=(B, H // hps),
        in_specs=[
            pl.BlockSpec((1, S, hps * N_GLA_OPERANDS * hk), lambda b, h: (b, 0, h)),
            head_v,
            pl.BlockSpec((1, n_chunks, hps * 2 * hk), lambda b, h: (b, 0, h)),
            head_v,
            pl.BlockSpec((hps, 1, hv), lambda b, h: (h, 0, 0)),
        ],
        out_specs=head_v,
        out_shape=jax.ShapeDtypeStruct((B, S, D), BF16),
        scratch_shapes=[o_scratch, o_scratch, kv_scratch, kv_scratch],
        compiler_params=pltpu.CompilerParams(dimension_semantics=("arbitrary", "arbitrary"),
                                             vmem_limit_bytes=VMEM_LIMIT_BYTES),
        name="gla_bidir",
    )(gl3, v3, d3, gz3, gng)


def _out_kernel(x_ref, g1_ref, sgl_ref, mg_ref, mod_ref, wbr_ref, wout_ref, fg_ref, o_ref):
    gate = mod_ref[0, 2:3, :]
    n_sub = x_ref.shape[0] // OUT_SUB
    subs = [slice(s * OUT_SUB, (s + 1) * OUT_SUB) for s in range(n_sub)]
    y_gla = [None] * n_sub
    y_gla[0] = _dot(g1_ref[subs[0], :], wbr_ref[...])
    for s, rows in enumerate(subs):
        if s + 1 < n_sub:
            y_gla[s + 1] = _dot(g1_ref[subs[s + 1], :], wbr_ref[...])
        merged = (sgl_ref[rows, :].astype(F32) * y_gla[s] + mg_ref[rows, :].astype(F32)).astype(BF16)
        r = x_ref[rows, :] + gate * _dot(merged, wout_ref[...])
        o_ref[rows, :] = (r * lax.rsqrt(jnp.mean(r * r, axis=-1, keepdims=True) + EPS)) * fg_ref[...]


def _merge_out(x2, g1, sgl, mg, mod3, w_br_gla, w_out, final_g, seq):
    T, D = x2.shape
    tm = OUT_TM
    assert T % tm == 0 and seq % tm == 0
    resident = functools.partial(pl.BlockSpec, pipeline_mode=pl.Buffered(1))
    tok = pl.BlockSpec((tm, D), lambda i: (i, 0))
    tiles_per_seq = seq // tm
    return pl.pallas_call(
        _out_kernel,
        grid=(T // tm,),
        in_specs=[
            tok, tok, tok, tok,
            pl.BlockSpec((1, 3, D), lambda i: (i // tiles_per_seq, 0, 0)),
            resident((D, D), lambda i: (0, 0)),
            resident((D, D), lambda i: (0, 0)),
            resident((1, D), lambda i: (0, 0)),
        ],
        out_specs=tok,
        out_shape=jax.ShapeDtypeStruct((T, D), F32),
        compiler_params=pltpu.CompilerParams(dimension_semantics=("arbitrary",),
                                             vmem_limit_bytes=VMEM_LIMIT_BYTES),
        name="merge_out",
    )(x2, g1, sgl, mg, mod3, w_br_gla, w_out, final_g)


def _layer(x, c, norm_g, w_ada, b_ada, w_in, alpha_fw_w, alpha_fw_b, alpha_bw_w, alpha_bw_b,
           gla_norm_g, gmlp_ln_g, gmlp_ln_b, gmlp_ws, gmlp_bs, w_br_gla, w_br_gmlp, w_out, out_g):
    B, S, D = x.shape
    H, hv, R, DK = GLA_HEADS, GLA_HV, GLA_RANK, GLA_DK
    T = B * S

    o_ra = 2 * DK + 2 * D
    assert o_ra == N_LO_GROUPS * D
    w_t = w_in.T
    w_main = _weight_prep(w_t)
    w_ra = jnp.pad(w_t[o_ra:o_ra + 2 * R], ((0, RA_PAD - 2 * R), (0, 0))).astype(BF16)
    zeros = jnp.zeros_like(alpha_fw_w)
    wa = jnp.concatenate([jnp.concatenate([alpha_fw_w, zeros], axis=1),
                          jnp.concatenate([zeros, alpha_bw_w], axis=1)], axis=0)
    wa = jnp.pad(wa, ((0, RA_PAD - 2 * R), (0, 0))).astype(BF16)
    ba = jnp.concatenate([alpha_fw_b, alpha_bw_b]).reshape(1, 2 * DK)
    bsm = jnp.repeat(gmlp_bs.T, GMLP_GC, axis=1)

    mod3 = _adaln_mod(c, w_ada, b_ada).reshape(B, 3, D)
    x2 = x.reshape(T, D)
    gl, d, v, gz, sgl, mg = _projection(
        x2, mod3, norm_g.reshape(1, D), w_main, w_ra, wa, ba, gmlp_ln_g.reshape(1, D), gmlp_ln_b.reshape(1, D),
        gmlp_ws.astype(BF16), bsm, w_br_gmlp.astype(BF16), S)
    g1 = _gla(gl.reshape(B, S, N_GLA_OPERANDS * DK), v.reshape(B, S, D), d.reshape(B, S // GLA_CHUNK, 2 * DK),
              gz.reshape(B, S, D), gla_norm_g.reshape(H, 1, hv))
    out = _merge_out(x2, g1.reshape(T, D), sgl, mg, mod3, w_br_gla.astype(BF16), w_out.astype(BF16),
                     out_g.reshape(1, D), S)
    return out.reshape(B, S, D)


def kernel(x, c, norm_g, w_ada, b_ada, w_in, alpha_fw_w, alpha_fw_b, alpha_bw_w, alpha_bw_b, gla_norm_g,
           gmlp_ln_g, gmlp_ln_b, gmlp_ws, gmlp_bs, w_br_gla, w_br_gmlp, w_out, final_g):
    depth = norm_g.shape[0]
    assert depth == 1, "the final rmsnorm is fused into the layer's output kernel"
    return _layer(x, c, norm_g[0], w_ada[0], b_ada[0], w_in[0], alpha_fw_w[0], alpha_fw_b[0], alpha_bw_w[0],
                  alpha_bw_b[0], gla_norm_g[0], gmlp_ln_g[0], gmlp_ln_b[0], gmlp_ws[0], gmlp_bs[0],
                  w_br_gla[0], w_br_gmlp[0], w_out[0], final_g)
```

```python
import functools

import jax
import jax.numpy as jnp
from jax import lax
from jax.experimental import pallas as pl
from jax.experimental.pallas import tpu as pltpu

F32 = jnp.float32
BF16 = jnp.bfloat16

D_MODEL = 1024
GLA_HEADS = 4
GLA_HK = 128
GLA_HV = 256
GLA_DK = GLA_HEADS * GLA_HK
GLA_RANK = 16
GLA_TAU = 16.0
GLA_CHUNK = 64
LOG_DECAY_FLOOR = -1.25
GMLP_GROUPS = 8
GMLP_GC = 128
GMLP_CHUNK = 128
EPS = 1e-6
SQRT_HALF = 0.7071067811865476
LOG2_E = 1.4426950408889634

RA_PAD = 128
MXU_K = 256
N_LO_GROUPS = 3
VMEM_LIMIT_BYTES = 56 * 1024 * 1024

PROJ_TM = 512
OUT_TM = 1024
OUT_SUB = 256
GLA_BLK = 256
CHUNKS_PER_BLK = GLA_BLK // GLA_CHUNK
CHUNK_SHIFT = GLA_CHUNK.bit_length() - 1
N_GLA_OPERANDS = 6
GLA_HEADS_PER_STEP = 2


def _sigmoid(z):
    return 0.5 * jnp.tanh(0.5 * z) + 0.5


def _silu(z):
    hz = 0.5 * z
    return hz * jnp.tanh(hz) + hz


def _gelu_exact(z):
    return 0.5 * z * (1.0 + lax.erf(z * SQRT_HALF))


def _log_sigmoid(z):
    return jnp.minimum(z, 0.0) - jnp.log1p(jnp.exp(-jnp.abs(z)))


def _dot(a, b):
    return jnp.dot(a, b, preferred_element_type=F32)


def _dot_nt(a, b):
    return lax.dot_general(a, b, (((1,), (1,)), ((), ())), preferred_element_type=F32)


def _dot_tn(a, b):
    return lax.dot_general(a, b, (((0,), (0,)), ((), ())), preferred_element_type=F32)


def _aligned_rows(start, size):
    if isinstance(start, int):
        return pl.ds(start, size)
    return pl.ds(pl.multiple_of(start, size), size)


def _block_iotas():
    ti = lax.broadcasted_iota(jnp.int32, (GLA_BLK, GLA_BLK), 0)
    si = lax.broadcasted_iota(jnp.int32, (GLA_BLK, GLA_BLK), 1)
    return ti, si


def _mod_kernel(c_ref, w_ref, b_ref, o_ref):
    a = _silu(c_ref[...]).astype(BF16)
    o_ref[...] = _dot(a, w_ref[...].astype(BF16)) + b_ref[...]


def _adaln_mod(c, w_ada, b_ada):
    B, D = c.shape
    n_out = w_ada.shape[1]
    return pl.pallas_call(
        _mod_kernel,
        grid=(n_out // D,),
        in_specs=[
            pl.BlockSpec((B, D), lambda j: (0, 0)),
            pl.BlockSpec((D, D), lambda j: (0, j)),
            pl.BlockSpec((1, D), lambda j: (0, j)),
        ],
        out_specs=pl.BlockSpec((B, D), lambda j: (0, j)),
        out_shape=jax.ShapeDtypeStruct((B, n_out), F32),
        compiler_params=pltpu.CompilerParams(dimension_semantics=("arbitrary",)),
        name="adaln_mod",
    )(c, w_ada, b_ada.reshape(1, n_out))


def _weight_prep_kernel(a_ref, b_ref, o_ref):
    skip = 2 * GLA_RANK

    @pl.when(pl.program_id(0) < N_LO_GROUPS)
    def _():
        o_ref[...] = a_ref[...].T.astype(BF16)

    @pl.when(pl.program_id(0) >= N_LO_GROUPS)
    def _():
        o_ref[...] = jnp.concatenate([a_ref[skip:, :], b_ref[...]], axis=0).T.astype(BF16)


def _weight_prep(w_t):
    K = w_t.shape[1]
    D = D_MODEL
    skip = 2 * GLA_RANK
    n_groups = (w_t.shape[0] - skip) // D
    return pl.pallas_call(
        _weight_prep_kernel,
        grid=(n_groups,),
        in_specs=[
            pl.BlockSpec((D, K), lambda g: (g, 0)),
            pl.BlockSpec((skip, K), lambda g: ((g + 1) * (D // skip), 0)),
        ],
        out_specs=pl.BlockSpec((K, D), lambda g: (0, g)),
        out_shape=jax.ShapeDtypeStruct((K, n_groups * D), BF16),
        compiler_params=pltpu.CompilerParams(dimension_semantics=("arbitrary",),
                                             vmem_limit_bytes=VMEM_LIMIT_BYTES),
        name="weight_prep",
    )(w_t, w_t)


def _proj_kernel(x_ref, mod_ref, ng_ref, wm_ref, wra_ref, wa_ref, ba_ref, lng_ref, lnb_ref, ws_ref, bsm_ref,
                 wbg_ref, gl_ref, d_ref, v_ref, gz_ref, sgl_ref, mg_ref, sg_scr):
    D = D_MODEL
    DK = GLA_DK
    hk = GLA_HK
    C = GLA_CHUNK
    tm = x_ref.shape[0]
    half = tm // 2
    shift = mod_ref[0, 0:1, :]
    gain = ng_ref[...] * (1.0 + mod_ref[0, 1:2, :])

    def normed(rows):
        x = x_ref[rows, :]
        r = lax.rsqrt(jnp.mean(x * x, axis=-1, keepdims=True) + EPS)
        return ((x * r) * gain + shift).astype(BF16)

    h_top = normed(slice(0, half))
    h_bot = normed(slice(half, tm))
    h = jnp.concatenate([h_top, h_bot], axis=0)

    def proj(j, lhs=None):
        return _dot(h if lhs is None else lhs, wm_ref[:, j * D:(j + 1) * D])


    qk = jnp.concatenate([proj(0, h_top), proj(0, h_bot)], axis=0)
    ra = jnp.concatenate([_dot_nt(h_top, wra_ref[...]), _dot_nt(h_bot, wra_ref[...])], axis=0).astype(BF16)
    pre = _dot(ra, wa_ref[...]) + ba_ref[...]
    la = jnp.maximum(_log_sigmoid(pre) * (1.0 / GLA_TAU), LOG_DECAY_FLOOR) * LOG2_E

    v_ref[...] = proj(1).astype(BF16)

    hi = la.astype(BF16)
    r1 = la - hi.astype(F32)
    mid = r1.astype(BF16)
    lo = (r1 - mid.astype(F32)).astype(BF16)

    u_pre = proj(3)

    ti, si = _block_iotas()
    same_chunk = (ti >> CHUNK_SHIFT) == (si >> CHUNK_SHIFT)
    tri_f = jnp.where(same_chunk & (si <= ti), 1.0, 0.0).astype(BF16)
    tri_b = jnp.where(same_chunk & (si >= ti), 1.0, 0.0).astype(BF16)
    cum = []
    for blk in range(tm // GLA_BLK):
        rs = slice(blk * GLA_BLK, (blk + 1) * GLA_BLK)
        b_f = _dot(tri_f, hi[rs, :DK]) + _dot(tri_f, mid[rs, :DK]) + _dot(tri_f, lo[rs, :DK])
        b_b = _dot(tri_b, hi[rs, DK:]) + _dot(tri_b, mid[rs, DK:]) + _dot(tri_b, lo[rs, DK:])
        cum.append((b_f, b_b))

    sgl_ref[...] = _sigmoid(proj(6)).astype(BF16)

    q_scale = GLA_HK ** -0.5
    for blk in range(tm // GLA_BLK):
        b_f, b_b = cum[blk]
        for c in range(CHUNKS_PER_BLK):
            cs = slice(c * C, (c + 1) * C)
            gr = slice(blk * GLA_BLK + c * C, blk * GLA_BLK + (c + 1) * C)
            n = blk * CHUNKS_PER_BLK + c
            q = qk[gr, :DK] * q_scale
            k = qk[gr, DK:]
            bf = b_f[cs]
            bb = b_b[cs]
            bf_last = bf[C - 1:C]
            bb_last = bb[0:1]
            operands = (q * jnp.exp2(bf), k * jnp.exp2(-bf), k * jnp.exp2(bf_last - bf),
                        q * jnp.exp2(bb), k * jnp.exp2(-bb), k * jnp.exp2(bb_last - bb))
            decays = (jnp.exp2(bf_last), jnp.exp2(bb_last))
            for hd in range(GLA_HEADS):
                hs = slice(hd * hk, (hd + 1) * hk)
                for g, val in enumerate(operands):
                    lane0 = (hd * N_GLA_OPERANDS + g) * hk
                    gl_ref[gr, lane0:lane0 + hk] = val[:, hs].astype(BF16)
                for g, val in enumerate(decays):
                    lane0 = (hd * len(decays) + g) * hk
                    d_ref[n:n + 1, lane0:lane0 + hk] = val[:, hs]

    zg = proj(5)

    vs = _gelu_exact(proj(4))
    mu = jnp.mean(vs, axis=-1, keepdims=True)
    vc = vs - mu
    var = jnp.mean(vc * vc, axis=-1, keepdims=True)
    vsn = ((vc * lax.rsqrt(var + EPS)) * lng_ref[...] + lnb_ref[...]).astype(BF16)

    gz_ref[...] = _silu(proj(2)).astype(BF16)

    n_pos = tm // GMLP_CHUNK
    for g in range(GMLP_GROUPS):
        cs = slice(g * GMLP_GC, (g + 1) * GMLP_GC)
        rhs = jnp.concatenate([vsn[n * GMLP_CHUNK:(n + 1) * GMLP_CHUNK, cs] for n in range(n_pos)], axis=1)
        mixed = _dot(ws_ref[g], rhs)
        for n in range(n_pos):
            sg_scr[n * GMLP_CHUNK:(n + 1) * GMLP_CHUNK, cs] = (
                mixed[:, n * GMLP_GC:(n + 1) * GMLP_GC] + bsm_ref[:, cs])
    merge_gate = _sigmoid(proj(7))

    y_gmlp = None
    for kt in range(D // MXU_K):
        ks = slice(kt * MXU_K, (kt + 1) * MXU_K)
        g2 = ((_gelu_exact(u_pre[:, ks]) * sg_scr[:, ks]) * _silu(zg[:, ks])).astype(BF16)
        part = _dot(g2, wbg_ref[ks, :])
        y_gmlp = part if y_gmlp is None else y_gmlp + part
    mg_ref[...] = (merge_gate * y_gmlp).astype(BF16)


def _projection(x2, mod3, norm_g, w_main, w_ra, wa, ba, ln_g, ln_b, ws, bsm, w_br_gmlp, seq):
    T, D = x2.shape
    tm = PROJ_TM
    assert T % tm == 0 and seq % tm == 0 and tm % GMLP_CHUNK == 0 and tm % GLA_BLK == 0
    resident = functools.partial(pl.BlockSpec, pipeline_mode=pl.Buffered(1))
    tok = lambda w: pl.BlockSpec((tm, w), lambda i: (i, 0))
    tiles_per_seq = seq // tm
    out_bf = jax.ShapeDtypeStruct((T, D), BF16)
    return pl.pallas_call(
        _proj_kernel,
        grid=(T // tm,),
        in_specs=[
            tok(D),
            pl.BlockSpec((1, 3, D), lambda i: (i // tiles_per_seq, 0, 0)),
            resident((1, D), lambda i: (0, 0)),
            resident(w_main.shape, lambda i: (0, 0)),
            resident(w_ra.shape, lambda i: (0, 0)),
            resident(wa.shape, lambda i: (0, 0)),
            resident(ba.shape, lambda i: (0, 0)),
            resident((1, D), lambda i: (0, 0)),
            resident((1, D), lambda i: (0, 0)),
            resident(ws.shape, lambda i: (0, 0, 0)),
            resident(bsm.shape, lambda i: (0, 0)),
            resident(w_br_gmlp.shape, lambda i: (0, 0)),
        ],
        out_specs=[tok(N_GLA_OPERANDS * GLA_DK), pl.BlockSpec((tm // GLA_CHUNK, 2 * GLA_DK), lambda i: (i, 0)),
                   tok(D), tok(D), tok(D), tok(D)],
        out_shape=[jax.ShapeDtypeStruct((T, N_GLA_OPERANDS * GLA_DK), BF16),
                   jax.ShapeDtypeStruct((T // GLA_CHUNK, 2 * GLA_DK), F32),
                   out_bf, out_bf, out_bf, out_bf],
        scratch_shapes=[pltpu.VMEM((tm, D), F32)],
        compiler_params=pltpu.CompilerParams(dimension_semantics=("arbitrary",),
                                             vmem_limit_bytes=VMEM_LIMIT_BYTES),
        name="projection_gmlp",
    )(x2, mod3, norm_g, w_main, w_ra, wa, ba, ln_g, ln_b, ws, bsm, w_br_gmlp)


def _gla_kernel(gl_ref, v_ref, d_ref, gz_ref, gng_ref, o_ref, of_scr, ob_scr, kvf_scr, kvb_scr):
    S = v_ref.shape[1]
    C = GLA_CHUNK
    hk = GLA_HK
    n_blk = S // GLA_BLK
    ti, si = _block_iotas()
    chunk_lo = (ti >> CHUNK_SHIFT) << CHUNK_SHIFT
    chunk_hi = chunk_lo + (C - 1)
    pos_chunk = lax.broadcasted_iota(jnp.int32, (hk, GLA_BLK), 1) >> CHUNK_SHIFT
    chunk_sel = [jnp.where(pos_chunk == c, 1.0, 0.0).astype(BF16) for c in range(CHUNKS_PER_BLK)]
    d_cols = d_ref[0].T
    hv = GLA_HV

    def direction(hd, forward):
        lane0 = (hd * N_GLA_OPERANDS + (0 if forward else N_GLA_OPERANDS // 2)) * hk
        row0 = (2 * hd + (0 if forward else 1)) * hk
        return (hd, lane0, lane0 + hk, lane0 + 2 * hk, d_cols[row0:row0 + hk],
                of_scr if forward else ob_scr, kvf_scr if forward else kvb_scr)

    def operand(rows, lane0):
        return gl_ref[0, rows, lane0:lane0 + hk]

    def values(hd, rows):
        return v_ref[0, rows, hd * hv:(hd + 1) * hv]

    def scores(j, ctx, forward):
        rows = _aligned_rows(j * GLA_BLK, GLA_BLK)
        a = _dot_nt(operand(rows, ctx[1]), operand(rows, ctx[2]))
        if forward:
            a = jnp.where(si <= ti, jnp.where(si >= chunk_lo, a, 0.0), 0.0)
        else:
            a = jnp.where(si > ti, jnp.where(si <= chunk_hi, a, 0.0), 0.0)
        return a.astype(BF16)

    def chunk_states(j, ctx):
        rows = _aligned_rows(j * GLA_BLK, GLA_BLK)
        ke = operand(rows, ctx[3])
        ke_t = ke.T
        ke_bd = jnp.concatenate([ke_t * chunk_sel[c] for c in range(CHUNKS_PER_BLK)], axis=0)
        ctx[6][j] = _dot(ke_bd, values(ctx[0], rows))

    def weighted_values(j, a, ctx):
        rows = _aligned_rows(j * GLA_BLK, GLA_BLK)
        ctx[5][ctx[0], rows, :] = _dot(a, values(ctx[0], rows))

    def scan(j, state, ctx, forward):
        hd, qt_lane, _, _, decays, o_scr, kv_scr = ctx
        o_inter = [None] * CHUNKS_PER_BLK
        for c in (range(CHUNKS_PER_BLK) if forward else reversed(range(CHUNKS_PER_BLK))):
            rows = _aligned_rows(j * GLA_BLK + c * C, C)
            n = j * CHUNKS_PER_BLK + c
            o_inter[c] = _dot(operand(rows, qt_lane), state.astype(BF16))
            state = state * decays[:, n:n + 1] + kv_scr[j, c * hk:(c + 1) * hk, :]
        o_scr[hd, _aligned_rows(j * GLA_BLK, GLA_BLK), :] += jnp.concatenate(o_inter, axis=0)
        return state

    def finish(hd, j):
        rows = _aligned_rows(j * GLA_BLK, GLA_BLK)
        lanes = slice(hd * hv, (hd + 1) * hv)
        o = of_scr[hd, rows, :] + ob_scr[hd, rows, :]
        on = o * lax.rsqrt(jnp.mean(o * o, axis=-1, keepdims=True) + EPS) * gng_ref[hd]
        o_ref[0, rows, lanes] = (on * gz_ref[0, rows, lanes].astype(F32)).astype(BF16)

    steps = [(hd, i) for hd in range(GLA_HEADS_PER_STEP) for i in range(n_blk)]
    sf = sb = None
    for cur, nxt in zip([None] + steps, steps + [None]):
        if nxt is not None:
            n_fwd, n_bwd = direction(nxt[0], True), direction(nxt[0], False)
            a_f = scores(nxt[1], n_fwd, True)
            a_b = scores(n_blk - 1 - nxt[1], n_bwd, False)
        if cur is not None:
            hd, i = cur
            if i == 0:
                sf = sb = jnp.zeros((hk, hv), F32)
            sf = scan(i, sf, direction(hd, True), True)
            sb = scan(n_blk - 1 - i, sb, direction(hd, False), False)
        if nxt is not None:
            chunk_states(nxt[1], n_fwd)
            chunk_states(n_blk - 1 - nxt[1], n_bwd)
            weighted_values(nxt[1], a_f, n_fwd)
            weighted_values(n_blk - 1 - nxt[1], a_b, n_bwd)
        if cur is not None:
            for j in sorted({i, n_blk - 1 - i}):
                if max(j, n_blk - 1 - j) == i:
                    finish(hd, j)


def _gla(gl3, v3, d3, gz3, gng):
    B, S, D = v3.shape
    H, hk, hv = GLA_HEADS, GLA_HK, GLA_HV
    n_chunks = S // GLA_CHUNK
    assert S % GLA_BLK == 0
    hps = GLA_HEADS_PER_STEP
    assert H % hps == 0
    head_v = pl.BlockSpec((1, S, hps * hv), lambda b, h: (b, 0, h))
    kv_scratch = pltpu.VMEM((S // GLA_BLK, CHUNKS_PER_BLK * hk, hv), F32)
    o_scratch = pltpu.VMEM((hps, S, hv), F32)
    return pl.pallas_call(
        _gla_kernel,
        grid=(B, H // hps),
        in_specs=[
            pl.BlockSpec((1, S, hps * N_GLA_OPERANDS * hk), lambda b, h: (b, 0, h)),
            head_v,
            pl.BlockSpec((1, n_chunks, hps * 2 * hk), lambda b, h: (b, 0, h)),
            head_v,
            pl.BlockSpec((hps, 1, hv), lambda b, h: (h, 0, 0)),
        ],
        out_specs=head_v,
        out_shape=jax.ShapeDtypeStruct((B, S, D), BF16),
        scratch_shapes=[o_scratch, o_scratch, kv_scratch, kv_scratch],
        compiler_params=pltpu.CompilerParams(dimension_semantics=("arbitrary", "arbitrary"),
                                             vmem_limit_bytes=VMEM_LIMIT_BYTES),
        name="gla_bidir",
    )(gl3, v3, d3, gz3, gng)


def _out_kernel(x_ref, g1_ref, sgl_ref, mg_ref, mod_ref, wbr_ref, wout_ref, fg_ref, o_ref):
    gate = mod_ref[0, 2:3, :]
    n_sub = x_ref.shape[0] // OUT_SUB
    subs = [slice(s * OUT_SUB, (s + 1) * OUT_SUB) for s in range(n_sub)]
    y_gla = [None] * n_sub
    y_gla[0] = _dot(g1_ref[subs[0], :], wbr_ref[...])
    for s, rows in enumerate(subs):
        if s + 1 < n_sub:
            y_gla[s + 1] = _dot(g1_ref[subs[s + 1], :], wbr_ref[...])
        merged = (sgl_ref[rows, :].astype(F32) * y_gla[s] + mg_ref[rows, :].astype(F32)).astype(BF16)
        r = x_ref[rows, :] + gate * _dot(merged, wout_ref[...])
        o_ref[rows, :] = (r * lax.rsqrt(jnp.mean(r * r, axis=-1, keepdims=True) + EPS)) * fg_ref[...]


def _merge_out(x2, g1, sgl, mg, mod3, w_br_gla, w_out, final_g, seq):
    T, D = x2.shape
    tm = OUT_TM
    assert T % tm == 0 and seq % tm == 0
    resident = functools.partial(pl.BlockSpec, pipeline_mode=pl.Buffered(1))
    tok = pl.BlockSpec((tm, D), lambda i: (i, 0))
    tiles_per_seq = seq // tm
    return pl.pallas_call(
        _out_kernel,
        grid=(T // tm,),
        in_specs=[
            tok, tok, tok, tok,
            pl.BlockSpec((1, 3, D), lambda i: (i // tiles_per_seq, 0, 0)),
            resident((D, D), lambda i: (0, 0)),
            resident((D, D), lambda i: (0, 0)),
            resident((1, D), lambda i: (0, 0)),
        ],
        out_specs=tok,
        out_shape=jax.ShapeDtypeStruct((T, D), F32),
        compiler_params=pltpu.CompilerParams(dimension_semantics=("arbitrary",),
                                             vmem_limit_bytes=VMEM_LIMIT_BYTES),
        name="merge_out",
    )(x2, g1, sgl, mg, mod3, w_br_gla, w_out, final_g)


def _layer(x, c, norm_g, w_ada, b_ada, w_in, alpha_fw_w, alpha_fw_b, alpha_bw_w, alpha_bw_b,
           gla_norm_g, gmlp_ln_g, gmlp_ln_b, gmlp_ws, gmlp_bs, w_br_gla, w_br_gmlp, w_out, out_g):
    B, S, D = x.shape
    H, hv, R, DK = GLA_HEADS, GLA_HV, GLA_RANK, GLA_DK
    T = B * S

    o_ra = 2 * DK + 2 * D
    assert o_ra == N_LO_GROUPS * D
    w_t = w_in.T
    w_main = _weight_prep(w_t)
    w_ra = jnp.pad(w_t[o_ra:o_ra + 2 * R], ((0, RA_PAD - 2 * R), (0, 0))).astype(BF16)
    zeros = jnp.zeros_like(alpha_fw_w)
    wa = jnp.concatenate([jnp.concatenate([alpha_fw_w, zeros], axis=1),
                          jnp.concatenate([zeros, alpha_bw_w], axis=1)], axis=0)
    wa = jnp.pad(wa, ((0, RA_PAD - 2 * R), (0, 0))).astype(BF16)
    ba = jnp.concatenate([alpha_fw_b, alpha_bw_b]).reshape(1, 2 * DK)
    bsm = jnp.repeat(gmlp_bs.T, GMLP_GC, axis=1)

    mod3 = _adaln_mod(c, w_ada, b_ada).reshape(B, 3, D)
    x2 = x.reshape(T, D)
    gl, d, v, gz, sgl, mg = _projection(
        x2, mod3, norm_g.reshape(1, D), w_main, w_ra, wa, ba, gmlp_ln_g.reshape(1, D), gmlp_ln_b.reshape(1, D),
        gmlp_ws.astype(BF16), bsm, w_br_gmlp.astype(BF16), S)
    g1 = _gla(gl.reshape(B, S, N_GLA_OPERANDS * DK), v.reshape(B, S, D), d.reshape(B, S // GLA_CHUNK, 2 * DK),
              gz.reshape(B, S, D), gla_norm_g.reshape(H, 1, hv))
    out = _merge_out(x2, g1.reshape(T, D), sgl, mg, mod3, w_br_gla.astype(BF16), w_out.astype(BF16),
                     out_g.reshape(1, D), S)
    return out.reshape(B, S, D)


def kernel(x, c, norm_g, w_ada, b_ada, w_in, alpha_fw_w, alpha_fw_b, alpha_bw_w, alpha_bw_b, gla_norm_g,
           gmlp_ln_g, gmlp_ln_b, gmlp_ws, gmlp_bs, w_br_gla, w_br_gmlp, w_out, final_g):
    depth = norm_g.shape[0]
    assert depth == 1, "the final rmsnorm is fused into the layer's output kernel"
    return _layer(x, c, norm_g[0], w_ada[0], b_ada[0], w_in[0], alpha_fw_w[0], alpha_fw_b[0], alpha_bw_w[0],
                  alpha_bw_b[0], gla_norm_g[0], gmlp_ln_g[0], gmlp_ln_b[0], gmlp_ws[0], gmlp_bs[0],
                  w_br_gla[0], w_br_gmlp[0], w_out[0], final_g)
```

```python
import functools

import jax
import jax.numpy as jnp
from jax import lax
from jax.experimental import pallas as pl
from jax.experimental.pallas import tpu as pltpu

F32 = jnp.float32
BF16 = jnp.bfloat16

D_MODEL = 1024
GLA_HEADS = 4
GLA_HK = 128
GLA_HV = 256
GLA_DK = GLA_HEADS * GLA_HK
GLA_RANK = 16
GLA_TAU = 16.0
GLA_CHUNK = 64
LOG_DECAY_FLOOR = -1.25
GMLP_GROUPS = 8
GMLP_GC = 128
GMLP_CHUNK = 128
EPS = 1e-6
SQRT_HALF = 0.7071067811865476
LOG2_E = 1.4426950408889634

RA_PAD = 128
MXU_K = 256
N_LO_GROUPS = 3
VMEM_LIMIT_BYTES = 56 * 1024 * 1024

PROJ_TM = 512
OUT_TM = 1024
OUT_SUB = 256
GLA_BLK = 256
CHUNKS_PER_BLK = GLA_BLK // GLA_CHUNK
CHUNK_SHIFT = GLA_CHUNK.bit_length() - 1
N_GLA_OPERANDS = 6
GLA_HEADS_PER_STEP = 2


def _sigmoid(z):
    return 0.5 * jnp.tanh(0.5 * z) + 0.5


def _silu(z):
    hz = 0.5 * z
    return hz * jnp.tanh(hz) + hz


def _gelu_exact(z):
    return 0.5 * z * (1.0 + lax.erf(z * SQRT_HALF))


def _log2_decay(z):
    soft = jnp.log(1.0 + jnp.exp(-z))
    return jnp.maximum(soft * (-LOG2_E / GLA_TAU), LOG_DECAY_FLOOR * LOG2_E)


def _dot(a, b):
    return jnp.dot(a, b, preferred_element_type=F32)


def _dot_nt(a, b):
    return lax.dot_general(a, b, (((1,), (1,)), ((), ())), preferred_element_type=F32)


def _dot_tn(a, b):
    return lax.dot_general(a, b, (((0,), (0,)), ((), ())), preferred_element_type=F32)


def _aligned_rows(start, size):
    if isinstance(start, int):
        return pl.ds(start, size)
    return pl.ds(pl.multiple_of(start, size), size)


def _block_iotas():
    ti = lax.broadcasted_iota(jnp.int32, (GLA_BLK, GLA_BLK), 0)
    si = lax.broadcasted_iota(jnp.int32, (GLA_BLK, GLA_BLK), 1)
    return ti, si


def _mod_kernel(c_ref, w_ref, b_ref, o_ref):
    a = _silu(c_ref[...]).astype(BF16)
    o_ref[...] = _dot(a, w_ref[...].astype(BF16)) + b_ref[...]


def _adaln_mod(c, w_ada, b_ada):
    B, D = c.shape
    n_out = w_ada.shape[1]
    return pl.pallas_call(
        _mod_kernel,
        grid=(n_out // D,),
        in_specs=[
            pl.BlockSpec((B, D), lambda j: (0, 0)),
            pl.BlockSpec((D, D), lambda j: (0, j)),
            pl.BlockSpec((1, D), lambda j: (0, j)),
        ],
        out_specs=pl.BlockSpec((B, D), lambda j: (0, j)),
        out_shape=jax.ShapeDtypeStruct((B, n_out), F32),
        compiler_params=pltpu.CompilerParams(dimension_semantics=("arbitrary",)),
        name="adaln_mod",
    )(c, w_ada, b_ada.reshape(1, n_out))


def _weight_prep_kernel(a_ref, b_ref, o_ref):
    skip = 2 * GLA_RANK

    @pl.when(pl.program_id(0) < N_LO_GROUPS)
    def _():
        o_ref[...] = a_ref[...].T.astype(BF16)

    @pl.when(pl.program_id(0) >= N_LO_GROUPS)
    def _():
        o_ref[...] = jnp.concatenate([a_ref[skip:, :], b_ref[...]], axis=0).T.astype(BF16)


def _weight_prep(w_t):
    K = w_t.shape[1]
    D = D_MODEL
    skip = 2 * GLA_RANK
    n_groups = (w_t.shape[0] - skip) // D
    return pl.pallas_call(
        _weight_prep_kernel,
        grid=(n_groups,),
        in_specs=[
            pl.BlockSpec((D, K), lambda g: (g, 0)),
            pl.BlockSpec((skip, K), lambda g: ((g + 1) * (D // skip), 0)),
        ],
        out_specs=pl.BlockSpec((K, D), lambda g: (0, g)),
        out_shape=jax.ShapeDtypeStruct((K, n_groups * D), BF16),
        compiler_params=pltpu.CompilerParams(dimension_semantics=("arbitrary",),
                                             vmem_limit_bytes=VMEM_LIMIT_BYTES),
        name="weight_prep",
    )(w_t, w_t)


def _proj_kernel(x_ref, mod_ref, ng_ref, wm_ref, wra_ref, wa_ref, ba_ref, lng_ref, lnb_ref, ws_ref, bsm_ref,
                 wbg_ref, gl_ref, d_ref, v_ref, gz_ref, sgl_ref, mg_ref, sg_scr):
    D = D_MODEL
    DK = GLA_DK
    hk = GLA_HK
    C = GLA_CHUNK
    tm = x_ref.shape[0]
    half = tm // 2
    shift = mod_ref[0, 0:1, :]
    gain = ng_ref[...] * (1.0 + mod_ref[0, 1:2, :])

    def normed(rows):
        x = x_ref[rows, :]
        r = lax.rsqrt(jnp.mean(x * x, axis=-1, keepdims=True) + EPS)
        return ((x * r) * gain + shift).astype(BF16)

    h_top = normed(slice(0, half))
    h_bot = normed(slice(half, tm))
    h = jnp.concatenate([h_top, h_bot], axis=0)

    def proj(j, lhs=None):
        return _dot(h if lhs is None else lhs, wm_ref[:, j * D:(j + 1) * D])


    qk = jnp.concatenate([proj(0, h_top), proj(0, h_bot)], axis=0)
    ra = jnp.concatenate([_dot_nt(h_top, wra_ref[...]), _dot_nt(h_bot, wra_ref[...])], axis=0).astype(BF16)
    pre = _dot(ra, wa_ref[...]) + ba_ref[...]
    la = _log2_decay(pre)

    v_ref[...] = proj(1).astype(BF16)

    hi = la.astype(BF16)
    r1 = la - hi.astype(F32)
    mid = r1.astype(BF16)
    lo = (r1 - mid.astype(F32)).astype(BF16)

    u_pre = proj(3)

    ti, si = _block_iotas()
    same_chunk = (ti >> CHUNK_SHIFT) == (si >> CHUNK_SHIFT)
    tri_f = jnp.where(same_chunk & (si <= ti), 1.0, 0.0).astype(BF16)
    tri_b = jnp.where(same_chunk & (si >= ti), 1.0, 0.0).astype(BF16)
    cum = []
    for blk in range(tm // GLA_BLK):
        rs = slice(blk * GLA_BLK, (blk + 1) * GLA_BLK)
        b_f = _dot(tri_f, hi[rs, :DK]) + _dot(tri_f, mid[rs, :DK]) + _dot(tri_f, lo[rs, :DK])
        b_b = _dot(tri_b, hi[rs, DK:]) + _dot(tri_b, mid[rs, DK:]) + _dot(tri_b, lo[rs, DK:])
        cum.append((b_f, b_b))

    sgl_ref[...] = _sigmoid(proj(6)).astype(BF16)

    q_scale = GLA_HK ** -0.5
    for blk in range(tm // GLA_BLK):
        b_f, b_b = cum[blk]
        for c in range(CHUNKS_PER_BLK):
            cs = slice(c * C, (c + 1) * C)
            gr = slice(blk * GLA_BLK + c * C, blk * GLA_BLK + (c + 1) * C)
            n = blk * CHUNKS_PER_BLK + c
            q = qk[gr, :DK] * q_scale
            k = qk[gr, DK:]
            bf = b_f[cs]
            bb = b_b[cs]
            bf_last = bf[C - 1:C]
            bb_last = bb[0:1]
            operands = (q * jnp.exp2(bf), k * jnp.exp2(-bf), k * jnp.exp2(bf_last - bf),
                        q * jnp.exp2(bb), k * jnp.exp2(-bb), k * jnp.exp2(bb_last - bb))
            decays = (jnp.exp2(bf_last), jnp.exp2(bb_last))
            for hd in range(GLA_HEADS):
                hs = slice(hd * hk, (hd + 1) * hk)
                for g, val in enumerate(operands):
                    lane0 = (hd * N_GLA_OPERANDS + g) * hk
                    gl_ref[gr, lane0:lane0 + hk] = val[:, hs].astype(BF16)
                for g, val in enumerate(decays):
                    lane0 = (hd * len(decays) + g) * hk
                    d_ref[n:n + 1, lane0:lane0 + hk] = val[:, hs]

    zg = proj(5)

    vs = _gelu_exact(proj(4))
    mu = jnp.mean(vs, axis=-1, keepdims=True)
    vc = vs - mu
    var = jnp.mean(vc * vc, axis=-1, keepdims=True)
    vsn = ((vc * lax.rsqrt(var + EPS)) * lng_ref[...] + lnb_ref[...]).astype(BF16)

    gz_ref[...] = _silu(proj(2)).astype(BF16)

    n_pos = tm // GMLP_CHUNK
    for g in range(GMLP_GROUPS):
        cs = slice(g * GMLP_GC, (g + 1) * GMLP_GC)
        rhs = jnp.concatenate([vsn[n * GMLP_CHUNK:(n + 1) * GMLP_CHUNK, cs] for n in range(n_pos)], axis=1)
        mixed = _dot(ws_ref[g], rhs)
        for n in range(n_pos):
            sg_scr[n * GMLP_CHUNK:(n + 1) * GMLP_CHUNK, cs] = (
                mixed[:, n * GMLP_GC:(n + 1) * GMLP_GC] + bsm_ref[:, cs])
    merge_gate = _sigmoid(proj(7))

    y_gmlp = None
    for kt in range(D // MXU_K):
        ks = slice(kt * MXU_K, (kt + 1) * MXU_K)
        g2 = ((_gelu_exact(u_pre[:, ks]) * sg_scr[:, ks]) * _silu(zg[:, ks])).astype(BF16)
        part = _dot(g2, wbg_ref[ks, :])
        y_gmlp = part if y_gmlp is None else y_gmlp + part
    mg_ref[...] = (merge_gate * y_gmlp).astype(BF16)


def _projection(x2, mod3, norm_g, w_main, w_ra, wa, ba, ln_g, ln_b, ws, bsm, w_br_gmlp, seq):
    T, D = x2.shape
    tm = PROJ_TM
    assert T % tm == 0 and seq % tm == 0 and tm % GMLP_CHUNK == 0 and tm % GLA_BLK == 0
    resident = functools.partial(pl.BlockSpec, pipeline_mode=pl.Buffered(1))
    tok = lambda w: pl.BlockSpec((tm, w), lambda i: (i, 0))
    tiles_per_seq = seq // tm
    out_bf = jax.ShapeDtypeStruct((T, D), BF16)
    return pl.pallas_call(
        _proj_kernel,
        grid=(T // tm,),
        in_specs=[
            tok(D),
            pl.BlockSpec((1, 3, D), lambda i: (i // tiles_per_seq, 0, 0)),
            resident((1, D), lambda i: (0, 0)),
            resident(w_main.shape, lambda i: (0, 0)),
            resident(w_ra.shape, lambda i: (0, 0)),
            resident(wa.shape, lambda i: (0, 0)),
            resident(ba.shape, lambda i: (0, 0)),
            resident((1, D), lambda i: (0, 0)),
            resident((1, D), lambda i: (0, 0)),
            resident(ws.shape, lambda i: (0, 0, 0)),
            resident(bsm.shape, lambda i: (0, 0)),
            resident(w_br_gmlp.shape, lambda i: (0, 0)),
        ],
        out_specs=[tok(N_GLA_OPERANDS * GLA_DK), pl.BlockSpec((tm // GLA_CHUNK, 2 * GLA_DK), lambda i: (i, 0)),
                   tok(D), tok(D), tok(D), tok(D)],
        out_shape=[jax.ShapeDtypeStruct((T, N_GLA_OPERANDS * GLA_DK), BF16),
                   jax.ShapeDtypeStruct((T // GLA_CHUNK, 2 * GLA_DK), F32),
                   out_bf, out_bf, out_bf, out_bf],
        scratch_shapes=[pltpu.VMEM((tm, D), F32)],
        compiler_params=pltpu.CompilerParams(dimension_semantics=("arbitrary",),
                                             vmem_limit_bytes=VMEM_LIMIT_BYTES),
        name="projection_gmlp",
    )(x2, mod3, norm_g, w_main, w_ra, wa, ba, ln_g, ln_b, ws, bsm, w_br_gmlp)


def _gla_kernel(gl_ref, v_ref, d_ref, gz_ref, gng_ref, o_ref, of_scr, ob_scr, kvf_scr, kvb_scr):
    S = v_ref.shape[1]
    C = GLA_CHUNK
    hk = GLA_HK
    n_blk = S // GLA_BLK
    ti, si = _block_iotas()
    chunk_lo = (ti >> CHUNK_SHIFT) << CHUNK_SHIFT
    chunk_hi = chunk_lo + (C - 1)
    pos_chunk = lax.broadcasted_iota(jnp.int32, (hk, GLA_BLK), 1) >> CHUNK_SHIFT
    chunk_sel = [jnp.where(pos_chunk == c, 1.0, 0.0).astype(BF16) for c in range(CHUNKS_PER_BLK)]
    d_cols = d_ref[0].T
    hv = GLA_HV

    def direction(hd, forward):
        lane0 = (hd * N_GLA_OPERANDS + (0 if forward else N_GLA_OPERANDS // 2)) * hk
        row0 = (2 * hd + (0 if forward else 1)) * hk
        return (hd, lane0, lane0 + hk, lane0 + 2 * hk, d_cols[row0:row0 + hk],
                of_scr if forward else ob_scr, kvf_scr if forward else kvb_scr)

    def operand(rows, lane0):
        return gl_ref[0, rows, lane0:lane0 + hk]

    def values(hd, rows):
        return v_ref[0, rows, hd * hv:(hd + 1) * hv]

    def scores(j, ctx, forward):
        rows = _aligned_rows(j * GLA_BLK, GLA_BLK)
        a = _dot_nt(operand(rows, ctx[1]), operand(rows, ctx[2]))
        if forward:
            a = jnp.where(si <= ti, jnp.where(si >= chunk_lo, a, 0.0), 0.0)
        else:
            a = jnp.where(si > ti, jnp.where(si <= chunk_hi, a, 0.0), 0.0)
        return a.astype(BF16)

    def chunk_states(j, ctx):
        rows = _aligned_rows(j * GLA_BLK, GLA_BLK)
        ke = operand(rows, ctx[3])
        ke_t = ke.T
        ke_bd = jnp.concatenate([ke_t * chunk_sel[c] for c in range(CHUNKS_PER_BLK)], axis=0)
        ctx[6][j] = _dot(ke_bd, values(ctx[0], rows))

    def weighted_values(j, a, ctx):
        rows = _aligned_rows(j * GLA_BLK, GLA_BLK)
        ctx[5][ctx[0], rows, :] = _dot(a, values(ctx[0], rows))

    def scan(j, state, ctx, forward):
        hd, qt_lane, _, _, decays, o_scr, kv_scr = ctx
        o_inter = [None] * CHUNKS_PER_BLK
        for c in (range(CHUNKS_PER_BLK) if forward else reversed(range(CHUNKS_PER_BLK))):
            rows = _aligned_rows(j * GLA_BLK + c * C, C)
            n = j * CHUNKS_PER_BLK + c
            o_inter[c] = _dot(operand(rows, qt_lane), state.astype(BF16))
            state = state * decays[:, n:n + 1] + kv_scr[j, c * hk:(c + 1) * hk, :]
        o_scr[hd, _aligned_rows(j * GLA_BLK, GLA_BLK), :] += jnp.concatenate(o_inter, axis=0)
        return state

    def finish(hd, j):
        rows = _aligned_rows(j * GLA_BLK, GLA_BLK)
        lanes = slice(hd * hv, (hd + 1) * hv)
        o = of_scr[hd, rows, :] + ob_scr[hd, rows, :]
        on = o * lax.rsqrt(jnp.mean(o * o, axis=-1, keepdims=True) + EPS) * gng_ref[hd]
        o_ref[0, rows, lanes] = (on * gz_ref[0, rows, lanes].astype(F32)).astype(BF16)

    steps = [(hd, i) for hd in range(GLA_HEADS_PER_STEP) for i in range(n_blk)]
    sf = sb = None
    for cur, nxt in zip([None] + steps, steps + [None]):
        if nxt is not None:
            n_fwd, n_bwd = direction(nxt[0], True), direction(nxt[0], False)
            a_f = scores(nxt[1], n_fwd, True)
            a_b = scores(n_blk - 1 - nxt[1], n_bwd, False)
        if cur is not None:
            hd, i = cur
            if i == 0:
                sf = sb = jnp.zeros((hk, hv), F32)
            sf = scan(i, sf, direction(hd, True), True)
            sb = scan(n_blk - 1 - i, sb, direction(hd, False), False)
        if nxt is not None:
            chunk_states(nxt[1], n_fwd)
            chunk_states(n_blk - 1 - nxt[1], n_bwd)
            weighted_values(nxt[1], a_f, n_fwd)
            weighted_values(n_blk - 1 - nxt[1], a_b, n_bwd)
        if cur is not None:
            for j in sorted({i, n_blk - 1 - i}):
                if max(j, n_blk - 1 - j) == i:
                    finish(hd, j)


def _gla(gl3, v3, d3, gz3, gng):
    B, S, D = v3.shape
    H, hk, hv = GLA_HEADS, GLA_HK, GLA_HV
    n_chunks = S // GLA_CHUNK
    assert S % GLA_BLK == 0
    hps = GLA_HEADS_PER_STEP
    assert H % hps == 0
    head_v = pl.BlockSpec((1, S, hps * hv), lambda b, h: (b, 0, h))
    kv_scratch = pltpu.VMEM((S // GLA_BLK, CHUNKS_PER_BLK * hk, hv), F32)
    o_scratch = pltpu.VMEM((hps, S, hv), F32)
    return pl.pallas_call(
        _gla_kernel,
        grid=(B, H // hps),
        in_specs=[
            pl.BlockSpec((1, S, hps * N_GLA_OPERANDS * hk), lambda b, h: (b, 0, h)),
            head_v,
            pl.BlockSpec((1, n_chunks, hps * 2 * hk), lambda b, h: (b, 0, h)),
            head_v,
            pl.BlockSpec((hps, 1, hv), lambda b, h: (h, 0, 0)),
        ],
        out_specs=head_v,
        out_shape=jax.ShapeDtypeStruct((B, S, D), BF16),
        scratch_shapes=[o_scratch, o_scratch, kv_scratch, kv_scratch],
        compiler_params=pltpu.CompilerParams(dimension_semantics=("arbitrary", "arbitrary"),
                                             vmem_limit_bytes=VMEM_LIMIT_BYTES),
        name="gla_bidir",
    )(gl3, v3, d3, gz3, gng)


def _out_kernel(x_ref, g1_ref, sgl_ref, mg_ref, mod_ref, wbr_ref, wout_ref, fg_ref, o_ref):
    gate = mod_ref[0, 2:3, :]
    n_sub = x_ref.shape[0] // OUT_SUB
    subs = [slice(s * OUT_SUB, (s + 1) * OUT_SUB) for s in range(n_sub)]
    y_gla = [None] * n_sub
    y_gla[0] = _dot(g1_ref[subs[0], :], wbr_ref[...])
    for s, rows in enumerate(subs):
        if s + 1 < n_sub:
            y_gla[s + 1] = _dot(g1_ref[subs[s + 1], :], wbr_ref[...])
        merged = (sgl_ref[rows, :].astype(F32) * y_gla[s] + mg_ref[rows, :].astype(F32)).astype(BF16)
        r = x_ref[rows, :] + gate * _dot(merged, wout_ref[...])
        o_ref[rows, :] = (r * lax.rsqrt(jnp.mean(r * r, axis=-1, keepdims=True) + EPS)) * fg_ref[...]


def _merge_out(x2, g1, sgl, mg, mod3, w_br_gla, w_out, final_g, seq):
    T, D = x2.shape
    tm = OUT_TM
    assert T % tm == 0 and seq % tm == 0
    resident = functools.partial(pl.BlockSpec, pipeline_mode=pl.Buffered(1))
    tok = pl.BlockSpec((tm, D), lambda i: (i, 0))
    tiles_per_seq = seq // tm
    return pl.pallas_call(
        _out_kernel,
        grid=(T // tm,),
        in_specs=[
            tok, tok, tok, tok,
            pl.BlockSpec((1, 3, D), lambda i: (i // tiles_per_seq, 0, 0)),
            resident((D, D), lambda i: (0, 0)),
            resident((D, D), lambda i: (0, 0)),
            resident((1, D), lambda i: (0, 0)),
        ],
        out_specs=tok,
        out_shape=jax.ShapeDtypeStruct((T, D), F32),
        compiler_params=pltpu.CompilerParams(dimension_semantics=("arbitrary",),
                                             vmem_limit_bytes=VMEM_LIMIT_BYTES),
        name="merge_out",
    )(x2, g1, sgl, mg, mod3, w_br_gla, w_out, final_g)


def _layer(x, c, norm_g, w_ada, b_ada, w_in, alpha_fw_w, alpha_fw_b, alpha_bw_w, alpha_bw_b,
           gla_norm_g, gmlp_ln_g, gmlp_ln_b, gmlp_ws, gmlp_bs, w_br_gla, w_br_gmlp, w_out, out_g):
    B, S, D = x.shape
    H, hv, R, DK = GLA_HEADS, GLA_HV, GLA_RANK, GLA_DK
    T = B * S

    o_ra = 2 * DK + 2 * D
    assert o_ra == N_LO_GROUPS * D
    w_t = w_in.T
    w_main = _weight_prep(w_t)
    w_ra = jnp.pad(w_t[o_ra:o_ra + 2 * R], ((0, RA_PAD - 2 * R), (0, 0))).astype(BF16)
    zeros = jnp.zeros_like(alpha_fw_w)
    wa = jnp.concatenate([jnp.concatenate([alpha_fw_w, zeros], axis=1),
                          jnp.concatenate([zeros, alpha_bw_w], axis=1)], axis=0)
    wa = jnp.pad(wa, ((0, RA_PAD - 2 * R), (0, 0))).astype(BF16)
    ba = jnp.concatenate([alpha_fw_b, alpha_bw_b]).reshape(1, 2 * DK)
    bsm = jnp.repeat(gmlp_bs.T, GMLP_GC, axis=1)

    mod3 = _adaln_mod(c, w_ada, b_ada).reshape(B, 3, D)
    x2 = x.reshape(T, D)
    gl, d, v, gz, sgl, mg = _projection(
        x2, mod3, norm_g.reshape(1, D), w_main, w_ra, wa, ba, gmlp_ln_g.reshape(1, D), gmlp_ln_b.reshape(1, D),
        gmlp_ws.astype(BF16), bsm, w_br_gmlp.astype(BF16), S)
    g1 = _gla(gl.reshape(B, S, N_GLA_OPERANDS * DK), v.reshape(B, S, D), d.reshape(B, S // GLA_CHUNK, 2 * DK),
              gz.reshape(B, S, D), gla_norm_g.reshape(H, 1, hv))
    out = _merge_out(x2, g1.reshape(T, D), sgl, mg, mod3, w_br_gla.astype(BF16), w_out.astype(BF16),
                     out_g.reshape(1, D), S)
    return out.reshape(B, S, D)


def kernel(x, c, norm_g, w_ada, b_ada, w_in, alpha_fw_w, alpha_fw_b, alpha_bw_w, alpha_bw_b, gla_norm_g,
           gmlp_ln_g, gmlp_ln_b, gmlp_ws, gmlp_bs, w_br_gla, w_br_gmlp, w_out, final_g):
    depth = norm_g.shape[0]
    assert depth == 1, "the final rmsnorm is fused into the layer's output kernel"
    return _layer(x, c, norm_g[0], w_ada[0], b_ada[0], w_in[0], alpha_fw_w[0], alpha_fw_b[0], alpha_bw_w[0],
                  alpha_bw_b[0], gla_norm_g[0], gmlp_ln_g[0], gmlp_ln_b[0], gmlp_ws[0], gmlp_bs[0],
                  w_br_gla[0], w_br_gmlp[0], w_out[0], final_g)
```

```python
import functools

import jax
import jax.numpy as jnp
from jax import lax
from jax.experimental import pallas as pl
from jax.experimental.pallas import tpu as pltpu

F32 = jnp.float32
BF16 = jnp.bfloat16

D_MODEL = 1024
GLA_HEADS = 4
GLA_HK = 128
GLA_HV = 256
GLA_DK = GLA_HEADS * GLA_HK
GLA_RANK = 16
GLA_TAU = 16.0
GLA_CHUNK = 64
LOG_DECAY_FLOOR = -1.25
GMLP_GROUPS = 8
GMLP_GC = 128
GMLP_CHUNK = 128
EPS = 1e-6
SQRT_TWO = 1.4142135623730951
LOG2_E = 1.4426950408889634

RA_PAD = 128
N_LO_GROUPS = 3
N_FULL_SCALE_GROUPS = 2
VMEM_LIMIT_BYTES = 56 * 1024 * 1024

PROJ_TM = 512
OUT_TM = 1024
OUT_SUB = 256
GLA_BLK = 256
CHUNKS_PER_BLK = GLA_BLK // GLA_CHUNK
CHUNK_SHIFT = GLA_CHUNK.bit_length() - 1
N_GLA_OPERANDS = 6
GLA_HEADS_PER_STEP = 2


def _sigmoid_h(hz):
    return 0.5 * jnp.tanh(hz) + 0.5


def _silu_h(hz):
    return hz * jnp.tanh(hz) + hz


def _gelu_exact_h(hz):
    return hz * (1.0 + lax.erf(hz * SQRT_TWO))


def _log2_decay(z):
    soft = jnp.log(1.0 + jnp.exp(-z))
    return jnp.maximum(soft * (-LOG2_E / GLA_TAU), LOG_DECAY_FLOOR * LOG2_E)


def _dot(a, b):
    return jnp.dot(a, b, preferred_element_type=F32)


def _dot_nt(a, b):
    return lax.dot_general(a, b, (((1,), (1,)), ((), ())), preferred_element_type=F32)


def _dot_tn(a, b):
    return lax.dot_general(a, b, (((0,), (0,)), ((), ())), preferred_element_type=F32)


def _aligned_rows(start, size):
    if isinstance(start, int):
        return pl.ds(start, size)
    return pl.ds(pl.multiple_of(start, size), size)


def _block_iotas():
    ti = lax.broadcasted_iota(jnp.int32, (GLA_BLK, GLA_BLK), 0)
    si = lax.broadcasted_iota(jnp.int32, (GLA_BLK, GLA_BLK), 1)
    return ti, si


def _mod_kernel(c_ref, w_ref, b_ref, o_ref):
    a = _silu_h(0.5 * c_ref[...]).astype(BF16)
    o_ref[...] = _dot(a, w_ref[...].astype(BF16)) + b_ref[...]


def _adaln_mod(c, w_ada, b_ada):
    B, D = c.shape
    n_out = w_ada.shape[1]
    return pl.pallas_call(
        _mod_kernel,
        grid=(n_out // D,),
        in_specs=[
            pl.BlockSpec((B, D), lambda j: (0, 0)),
            pl.BlockSpec((D, D), lambda j: (0, j)),
            pl.BlockSpec((1, D), lambda j: (0, j)),
        ],
        out_specs=pl.BlockSpec((B, D), lambda j: (0, j)),
        out_shape=jax.ShapeDtypeStruct((B, n_out), F32),
        compiler_params=pltpu.CompilerParams(dimension_semantics=("arbitrary",)),
        name="adaln_mod",
    )(c, w_ada, b_ada.reshape(1, n_out))


def _weight_prep_kernel(a_ref, b_ref, o_ref):
    skip = 2 * GLA_RANK
    scale = jnp.where(pl.program_id(0) >= N_FULL_SCALE_GROUPS, 0.5, 1.0).astype(F32)

    @pl.when(pl.program_id(0) < N_LO_GROUPS)
    def _():
        o_ref[...] = (a_ref[...] * scale).T.astype(BF16)

    @pl.when(pl.program_id(0) >= N_LO_GROUPS)
    def _():
        o_ref[...] = (jnp.concatenate([a_ref[skip:, :], b_ref[...]], axis=0) * scale).T.astype(BF16)


def _weight_prep(w_t):
    K = w_t.shape[1]
    D = D_MODEL
    skip = 2 * GLA_RANK
    n_groups = (w_t.shape[0] - skip) // D
    return pl.pallas_call(
        _weight_prep_kernel,
        grid=(n_groups,),
        in_specs=[
            pl.BlockSpec((D, K), lambda g: (g, 0)),
            pl.BlockSpec((skip, K), lambda g: ((g + 1) * (D // skip), 0)),
        ],
        out_specs=pl.BlockSpec((K, D), lambda g: (0, g)),
        out_shape=jax.ShapeDtypeStruct((K, n_groups * D), BF16),
        compiler_params=pltpu.CompilerParams(dimension_semantics=("arbitrary",),
                                             vmem_limit_bytes=VMEM_LIMIT_BYTES),
        name="weight_prep",
    )(w_t, w_t)


def _proj_kernel(x_ref, mod_ref, ng_ref, wm_ref, wra_ref, wa_ref, ba_ref, lng_ref, lnb_ref, ws_ref, bsm_ref,
                 wbg_ref, gl_ref, d_ref, v_ref, gz_ref, sgl_ref, mg_ref, sg_scr):
    D = D_MODEL
    DK = GLA_DK
    hk = GLA_HK
    C = GLA_CHUNK
    tm = x_ref.shape[0]
    half = tm // 2
    shift = mod_ref[0, 0:1, :]
    gain = ng_ref[...] * (1.0 + mod_ref[0, 1:2, :])

    def normed(rows):
        x = x_ref[rows, :]
        r = lax.rsqrt(jnp.mean(x * x, axis=-1, keepdims=True) + EPS)
        return ((x * r) * gain + shift).astype(BF16)

    h_top = normed(slice(0, half))
    h_bot = normed(slice(half, tm))
    h = jnp.concatenate([h_top, h_bot], axis=0)

    def proj(j, lhs=None):
        return _dot(h if lhs is None else lhs, wm_ref[:, j * D:(j + 1) * D])


    qk = jnp.concatenate([proj(0, h_top), proj(0, h_bot)], axis=0)
    ra = jnp.concatenate([_dot_nt(h_top, wra_ref[...]), _dot_nt(h_bot, wra_ref[...])], axis=0).astype(BF16)
    pre = _dot(ra, wa_ref[...]) + ba_ref[...]
    la = _log2_decay(pre)

    v_ref[...] = proj(1).astype(BF16)

    hi = la.astype(BF16)
    r1 = la - hi.astype(F32)
    mid = r1.astype(BF16)
    lo = (r1 - mid.astype(F32)).astype(BF16)

    u_pre = proj(3)

    ti, si = _block_iotas()
    same_chunk = (ti >> CHUNK_SHIFT) == (si >> CHUNK_SHIFT)
    tri_f = jnp.where(same_chunk & (si <= ti), 1.0, 0.0).astype(BF16)
    tri_b = jnp.where(same_chunk & (si >= ti), 1.0, 0.0).astype(BF16)
    cum = []
    for blk in range(tm // GLA_BLK):
        rs = slice(blk * GLA_BLK, (blk + 1) * GLA_BLK)
        b_f = _dot(tri_f, hi[rs, :DK]) + _dot(tri_f, mid[rs, :DK]) + _dot(tri_f, lo[rs, :DK])
        b_b = _dot(tri_b, hi[rs, DK:]) + _dot(tri_b, mid[rs, DK:]) + _dot(tri_b, lo[rs, DK:])
        cum.append((b_f, b_b))

    sgl_ref[...] = _sigmoid_h(proj(6)).astype(BF16)

    q_scale = GLA_HK ** -0.5
    for blk in range(tm // GLA_BLK):
        b_f, b_b = cum[blk]
        for c in range(CHUNKS_PER_BLK):
            cs = slice(c * C, (c + 1) * C)
            gr = slice(blk * GLA_BLK + c * C, blk * GLA_BLK + (c + 1) * C)
            n = blk * CHUNKS_PER_BLK + c
            q = qk[gr, :DK] * q_scale
            k = qk[gr, DK:]
            bf = b_f[cs]
            bb = b_b[cs]
            bf_last = bf[C - 1:C]
            bb_last = bb[0:1]
            operands = (q * jnp.exp2(bf), k * jnp.exp2(-bf), k * jnp.exp2(bf_last - bf),
                        q * jnp.exp2(bb), k * jnp.exp2(-bb), k * jnp.exp2(bb_last - bb))
            decays = (jnp.exp2(bf_last), jnp.exp2(bb_last))
            for hd in range(GLA_HEADS):
                hs = slice(hd * hk, (hd + 1) * hk)
                for g, val in enumerate(operands):
                    lane0 = (hd * N_GLA_OPERANDS + g) * hk
                    gl_ref[gr, lane0:lane0 + hk] = val[:, hs].astype(BF16)
                for g, val in enumerate(decays):
                    lane0 = (hd * len(decays) + g) * hk
                    d_ref[n:n + 1, lane0:lane0 + hk] = val[:, hs]

    zg = proj(5)

    vs = _gelu_exact_h(proj(4))
    mu = jnp.mean(vs, axis=-1, keepdims=True)
    vc = vs - mu
    var = jnp.mean(vc * vc, axis=-1, keepdims=True)
    vsn = ((vc * lax.rsqrt(var + EPS)) * lng_ref[...] + lnb_ref[...]).astype(BF16)

    gz_ref[...] = _silu_h(proj(2)).astype(BF16)

    n_pos = tm // GMLP_CHUNK
    for g in range(GMLP_GROUPS):
        cs = slice(g * GMLP_GC, (g + 1) * GMLP_GC)
        rhs = jnp.concatenate([vsn[n * GMLP_CHUNK:(n + 1) * GMLP_CHUNK, cs] for n in range(n_pos)], axis=1)
        mixed = _dot(ws_ref[g], rhs)
        for n in range(n_pos):
            sg_scr[n * GMLP_CHUNK:(n + 1) * GMLP_CHUNK, cs] = (
                mixed[:, n * GMLP_GC:(n + 1) * GMLP_GC] + bsm_ref[:, cs])
    merge_gate = _sigmoid_h(proj(7))

    g2 = ((_gelu_exact_h(u_pre) * sg_scr[...]) * _silu_h(zg)).astype(BF16)
    mg_ref[...] = (merge_gate * _dot(g2, wbg_ref[...])).astype(BF16)


def _projection(x2, mod3, norm_g, w_main, w_ra, wa, ba, ln_g, ln_b, ws, bsm, w_br_gmlp, seq):
    T, D = x2.shape
    tm = PROJ_TM
    assert T % tm == 0 and seq % tm == 0 and tm % GMLP_CHUNK == 0 and tm % GLA_BLK == 0
    resident = functools.partial(pl.BlockSpec, pipeline_mode=pl.Buffered(1))
    tok = lambda w: pl.BlockSpec((tm, w), lambda i: (i, 0))
    tiles_per_seq = seq // tm
    out_bf = jax.ShapeDtypeStruct((T, D), BF16)
    return pl.pallas_call(
        _proj_kernel,
        grid=(T // tm,),
        in_specs=[
            tok(D),
            pl.BlockSpec((1, 3, D), lambda i: (i // tiles_per_seq, 0, 0)),
            resident((1, D), lambda i: (0, 0)),
            resident(w_main.shape, lambda i: (0, 0)),
            resident(w_ra.shape, lambda i: (0, 0)),
            resident(wa.shape, lambda i: (0, 0)),
            resident(ba.shape, lambda i: (0, 0)),
            resident((1, D), lambda i: (0, 0)),
            resident((1, D), lambda i: (0, 0)),
            resident(ws.shape, lambda i: (0, 0, 0)),
            resident(bsm.shape, lambda i: (0, 0)),
            resident(w_br_gmlp.shape, lambda i: (0, 0)),
        ],
        out_specs=[tok(N_GLA_OPERANDS * GLA_DK), pl.BlockSpec((tm // GLA_CHUNK, 2 * GLA_DK), lambda i: (i, 0)),
                   tok(D), tok(D), tok(D), tok(D)],
        out_shape=[jax.ShapeDtypeStruct((T, N_GLA_OPERANDS * GLA_DK), BF16),
                   jax.ShapeDtypeStruct((T // GLA_CHUNK, 2 * GLA_DK), F32),
                   out_bf, out_bf, out_bf, out_bf],
        scratch_shapes=[pltpu.VMEM((tm, D), F32)],
        compiler_params=pltpu.CompilerParams(dimension_semantics=("arbitrary",),
                                             vmem_limit_bytes=VMEM_LIMIT_BYTES),
        name="projection_gmlp",
    )(x2, mod3, norm_g, w_main, w_ra, wa, ba, ln_g, ln_b, ws, bsm, w_br_gmlp)


def _gla_kernel(gl_ref, v_ref, d_ref, gz_ref, gng_ref, o_ref, of_scr, ob_scr, kvf_scr, kvb_scr):
    S = v_ref.shape[1]
    C = GLA_CHUNK
    hk = GLA_HK
    n_blk = S // GLA_BLK
    ti, si = _block_iotas()
    chunk_lo = (ti >> CHUNK_SHIFT) << CHUNK_SHIFT
    chunk_hi = chunk_lo + (C - 1)
    pos_chunk = lax.broadcasted_iota(jnp.int32, (hk, GLA_BLK), 1) >> CHUNK_SHIFT
    chunk_sel = [jnp.where(pos_chunk == c, 1.0, 0.0).astype(BF16) for c in range(CHUNKS_PER_BLK)]
    d_cols = d_ref[0].T
    hv = GLA_HV

    def direction(hd, forward):
        lane0 = (hd * N_GLA_OPERANDS + (0 if forward else N_GLA_OPERANDS // 2)) * hk
        row0 = (2 * hd + (0 if forward else 1)) * hk
        return (hd, lane0, lane0 + hk, lane0 + 2 * hk, d_cols[row0:row0 + hk],
                of_scr if forward else ob_scr, kvf_scr if forward else kvb_scr)

    def operand(rows, lane0):
        return gl_ref[0, rows, lane0:lane0 + hk]

    def values(hd, rows):
        return v_ref[0, rows, hd * hv:(hd + 1) * hv]

    def scores(j, ctx, forward):
        rows = _aligned_rows(j * GLA_BLK, GLA_BLK)
        a = _dot_nt(operand(rows, ctx[1]), operand(rows, ctx[2]))
        if forward:
            a = jnp.where(si <= ti, jnp.where(si >= chunk_lo, a, 0.0), 0.0)
        else:
            a = jnp.where(si > ti, jnp.where(si <= chunk_hi, a, 0.0), 0.0)
        return a.astype(BF16)

    def chunk_states(j, ctx):
        rows = _aligned_rows(j * GLA_BLK, GLA_BLK)
        ke = operand(rows, ctx[3])
        ke_t = ke.T
        ke_bd = jnp.concatenate([ke_t * chunk_sel[c] for c in range(CHUNKS_PER_BLK)], axis=0)
        ctx[6][j] = _dot(ke_bd, values(ctx[0], rows))

    def weighted_values(j, a, ctx):
        rows = _aligned_rows(j * GLA_BLK, GLA_BLK)
        ctx[5][ctx[0], rows, :] = _dot(a, values(ctx[0], rows))

    def scan(j, state, ctx, forward):
        hd, qt_lane, _, _, decays, o_scr, kv_scr = ctx
        o_inter = [None] * CHUNKS_PER_BLK
        for c in (range(CHUNKS_PER_BLK) if forward else reversed(range(CHUNKS_PER_BLK))):
            rows = _aligned_rows(j * GLA_BLK + c * C, C)
            n = j * CHUNKS_PER_BLK + c
            o_inter[c] = _dot(operand(rows, qt_lane), state.astype(BF16))
            state = state * decays[:, n:n + 1] + kv_scr[j, c * hk:(c + 1) * hk, :]
        o_scr[hd, _aligned_rows(j * GLA_BLK, GLA_BLK), :] += jnp.concatenate(o_inter, axis=0)
        return state

    def finish(hd, j):
        rows = _aligned_rows(j * GLA_BLK, GLA_BLK)
        lanes = slice(hd * hv, (hd + 1) * hv)
        o = of_scr[hd, rows, :] + ob_scr[hd, rows, :]
        on = o * lax.rsqrt(jnp.mean(o * o, axis=-1, keepdims=True) + EPS) * gng_ref[hd]
        o_ref[0, rows, lanes] = (on * gz_ref[0, rows, lanes].astype(F32)).astype(BF16)

    steps = [(hd, i) for hd in range(GLA_HEADS_PER_STEP) for i in range(n_blk)]
    sf = sb = None
    for cur, nxt in zip([None] + steps, steps + [None]):
        if nxt is not None:
            n_fwd, n_bwd = direction(nxt[0], True), direction(nxt[0], False)
            a_f = scores(nxt[1], n_fwd, True)
            a_b = scores(n_blk - 1 - nxt[1], n_bwd, False)
        if cur is not None:
            hd, i = cur
            if i == 0:
                sf = sb = jnp.zeros((hk, hv), F32)
            sf = scan(i, sf, direction(hd, True), True)
            sb = scan(n_blk - 1 - i, sb, direction(hd, False), False)
        if nxt is not None:
            chunk_states(nxt[1], n_fwd)
            chunk_states(n_blk - 1 - nxt[1], n_bwd)
            weighted_values(nxt[1], a_f, n_fwd)
            weighted_values(n_blk - 1 - nxt[1], a_b, n_bwd)
        if cur is not None:
            for j in sorted({i, n_blk - 1 - i}):
                if max(j, n_blk - 1 - j) == i:
                    finish(hd, j)


def _gla(gl3, v3, d3, gz3, gng):
    B, S, D = v3.shape
    H, hk, hv = GLA_HEADS, GLA_HK, GLA_HV
    n_chunks = S // GLA_CHUNK
    assert S % GLA_BLK == 0
    hps = GLA_HEADS_PER_STEP
    assert H % hps == 0
    head_v = pl.BlockSpec((1, S, hps * hv), lambda b, h: (b, 0, h))
    kv_scratch = pltpu.VMEM((S // GLA_BLK, CHUNKS_PER_BLK * hk, hv), F32)
    o_scratch = pltpu.VMEM((hps, S, hv), F32)
    return pl.pallas_call(
        _gla_kernel,
        grid=(B, H // hps),
        in_specs=[
            pl.BlockSpec((1, S, hps * N_GLA_OPERANDS * hk), lambda b, h: (b, 0, h)),
            head_v,
            pl.BlockSpec((1, n_chunks, hps * 2 * hk), lambda b, h: (b, 0, h)),
            head_v,
            pl.BlockSpec((hps, 1, hv), lambda b, h: (h, 0, 0)),
        ],
        out_specs=head_v,
        out_shape=jax.ShapeDtypeStruct((B, S, D), BF16),
        scratch_shapes=[o_scratch, o_scratch, kv_scratch, kv_scratch],
        compiler_params=pltpu.CompilerParams(dimension_semantics=("arbitrary", "arbitrary"),
                                             vmem_limit_bytes=VMEM_LIMIT_BYTES),
        name="gla_bidir",
    )(gl3, v3, d3, gz3, gng)


def _out_kernel(x_ref, g1_ref, sgl_ref, mg_ref, mod_ref, wbr_ref, wout_ref, fg_ref, o_ref):
    gate = mod_ref[0, 2:3, :]
    n_sub = x_ref.shape[0] // OUT_SUB
    subs = [slice(s * OUT_SUB, (s + 1) * OUT_SUB) for s in range(n_sub)]
    y_gla = [None] * n_sub
    y_gla[0] = _dot(g1_ref[subs[0], :], wbr_ref[...])
    for s, rows in enumerate(subs):
        if s + 1 < n_sub:
            y_gla[s + 1] = _dot(g1_ref[subs[s + 1], :], wbr_ref[...])
        merged = (sgl_ref[rows, :].astype(F32) * y_gla[s] + mg_ref[rows, :].astype(F32)).astype(BF16)
        r = x_ref[rows, :] + gate * _dot(merged, wout_ref[...])
        o_ref[rows, :] = (r * lax.rsqrt(jnp.mean(r * r, axis=-1, keepdims=True) + EPS)) * fg_ref[...]


def _merge_out(x2, g1, sgl, mg, mod3, w_br_gla, w_out, final_g, seq):
    T, D = x2.shape
    tm = OUT_TM
    assert T % tm == 0 and seq % tm == 0
    resident = functools.partial(pl.BlockSpec, pipeline_mode=pl.Buffered(1))
    tok = pl.BlockSpec((tm, D), lambda i: (i, 0))
    tiles_per_seq = seq // tm
    return pl.pallas_call(
        _out_kernel,
        grid=(T // tm,),
        in_specs=[
            tok, tok, tok, tok,
            pl.BlockSpec((1, 3, D), lambda i: (i // tiles_per_seq, 0, 0)),
            resident((D, D), lambda i: (0, 0)),
            resident((D, D), lambda i: (0, 0)),
            resident((1, D), lambda i: (0, 0)),
        ],
        out_specs=tok,
        out_shape=jax.ShapeDtypeStruct((T, D), F32),
        compiler_params=pltpu.CompilerParams(dimension_semantics=("arbitrary",),
                                             vmem_limit_bytes=VMEM_LIMIT_BYTES),
        name="merge_out",
    )(x2, g1, sgl, mg, mod3, w_br_gla, w_out, final_g)


def _layer(x, c, norm_g, w_ada, b_ada, w_in, alpha_fw_w, alpha_fw_b, alpha_bw_w, alpha_bw_b,
           gla_norm_g, gmlp_ln_g, gmlp_ln_b, gmlp_ws, gmlp_bs, w_br_gla, w_br_gmlp, w_out, out_g):
    B, S, D = x.shape
    H, hv, R, DK = GLA_HEADS, GLA_HV, GLA_RANK, GLA_DK
    T = B * S

    o_ra = 2 * DK + 2 * D
    assert o_ra == N_LO_GROUPS * D
    w_t = w_in.T
    w_main = _weight_prep(w_t)
    w_ra = jnp.pad(w_t[o_ra:o_ra + 2 * R], ((0, RA_PAD - 2 * R), (0, 0))).astype(BF16)
    zeros = jnp.zeros_like(alpha_fw_w)
    wa = jnp.concatenate([jnp.concatenate([alpha_fw_w, zeros], axis=1),
                          jnp.concatenate([zeros, alpha_bw_w], axis=1)], axis=0)
    wa = jnp.pad(wa, ((0, RA_PAD - 2 * R), (0, 0))).astype(BF16)
    ba = jnp.concatenate([alpha_fw_b, alpha_bw_b]).reshape(1, 2 * DK)
    bsm = jnp.repeat(gmlp_bs.T, GMLP_GC, axis=1)

    mod3 = _adaln_mod(c, w_ada, b_ada).reshape(B, 3, D)
    x2 = x.reshape(T, D)
    gl, d, v, gz, sgl, mg = _projection(
        x2, mod3, norm_g.reshape(1, D), w_main, w_ra, wa, ba, gmlp_ln_g.reshape(1, D), gmlp_ln_b.reshape(1, D),
        gmlp_ws.astype(BF16), bsm, w_br_gmlp.astype(BF16), S)
    g1 = _gla(gl.reshape(B, S, N_GLA_OPERANDS * DK), v.reshape(B, S, D), d.reshape(B, S // GLA_CHUNK, 2 * DK),
              gz.reshape(B, S, D), gla_norm_g.reshape(H, 1, hv))
    out = _merge_out(x2, g1.reshape(T, D), sgl, mg, mod3, w_br_gla.astype(BF16), w_out.astype(BF16),
                     out_g.reshape(1, D), S)
    return out.reshape(B, S, D)


def kernel(x, c, norm_g, w_ada, b_ada, w_in, alpha_fw_w, alpha_fw_b, alpha_bw_w, alpha_bw_b, gla_norm_g,
           gmlp_ln_g, gmlp_ln_b, gmlp_ws, gmlp_bs, w_br_gla, w_br_gmlp, w_out, final_g):
    depth = norm_g.shape[0]
    assert depth == 1, "the final rmsnorm is fused into the layer's output kernel"
    return _layer(x, c, norm_g[0], w_ada[0], b_ada[0], w_in[0], alpha_fw_w[0], alpha_fw_b[0], alpha_bw_w[0],
                  alpha_bw_b[0], gla_norm_g[0], gmlp_ln_g[0], gmlp_ln_b[0], gmlp_ws[0], gmlp_bs[0],
                  w_br_gla[0], w_br_gmlp[0], w_out[0], final_g)
```

```python
import functools

import jax
import jax.numpy as jnp
from jax import lax
from jax.experimental import pallas as pl
from jax.experimental.pallas import tpu as pltpu

F32 = jnp.float32
BF16 = jnp.bfloat16

D_MODEL = 1024
GLA_HEADS = 4
GLA_HK = 128
GLA_HV = 256
GLA_DK = GLA_HEADS * GLA_HK
GLA_RANK = 16
GLA_TAU = 16.0
GLA_CHUNK = 64
LOG_DECAY_FLOOR = -1.25
GMLP_GROUPS = 8
GMLP_GC = 128
GMLP_CHUNK = 128
EPS = 1e-6
SQRT_TWO = 1.4142135623730951
LOG2_E = 1.4426950408889634

RA_PAD = 128
MXU_N = 256
N_LO_GROUPS = 3
N_FULL_SCALE_GROUPS = 2
VMEM_LIMIT_BYTES = 56 * 1024 * 1024

PROJ_TM = 512
OUT_TM = 1024
OUT_SUB = 256
GLA_BLK = 256
CHUNKS_PER_BLK = GLA_BLK // GLA_CHUNK
CHUNK_SHIFT = GLA_CHUNK.bit_length() - 1
N_GLA_OPERANDS = 6
GLA_HEADS_PER_STEP = 2


def _sigmoid_h(hz):
    return 0.5 * jnp.tanh(hz) + 0.5


def _silu_h(hz):
    return hz * jnp.tanh(hz) + hz


def _gelu_exact_h(hz):
    return hz * (1.0 + lax.erf(hz * SQRT_TWO))


def _log2_decay(z):
    soft = jnp.log(1.0 + jnp.exp(-z))
    return jnp.maximum(soft * (-LOG2_E / GLA_TAU), LOG_DECAY_FLOOR * LOG2_E)


def _dot(a, b):
    return jnp.dot(a, b, preferred_element_type=F32)


def _dot_nt(a, b):
    return lax.dot_general(a, b, (((1,), (1,)), ((), ())), preferred_element_type=F32)


def _dot_tn(a, b):
    return lax.dot_general(a, b, (((0,), (0,)), ((), ())), preferred_element_type=F32)


def _aligned_rows(start, size):
    if isinstance(start, int):
        return pl.ds(start, size)
    return pl.ds(pl.multiple_of(start, size), size)


def _block_iotas():
    ti = lax.broadcasted_iota(jnp.int32, (GLA_BLK, GLA_BLK), 0)
    si = lax.broadcasted_iota(jnp.int32, (GLA_BLK, GLA_BLK), 1)
    return ti, si


def _mod_kernel(c_ref, w_ref, b_ref, o_ref):
    a = _silu_h(0.5 * c_ref[...]).astype(BF16)
    o_ref[...] = _dot(a, w_ref[...].astype(BF16)) + b_ref[...]


def _adaln_mod(c, w_ada, b_ada):
    B, D = c.shape
    n_out = w_ada.shape[1]
    return pl.pallas_call(
        _mod_kernel,
        grid=(n_out // D,),
        in_specs=[
            pl.BlockSpec((B, D), lambda j: (0, 0)),
            pl.BlockSpec((D, D), lambda j: (0, j)),
            pl.BlockSpec((1, D), lambda j: (0, j)),
        ],
        out_specs=pl.BlockSpec((B, D), lambda j: (0, j)),
        out_shape=jax.ShapeDtypeStruct((B, n_out), F32),
        compiler_params=pltpu.CompilerParams(dimension_semantics=("arbitrary",)),
        name="adaln_mod",
    )(c, w_ada, b_ada.reshape(1, n_out))


def _store_column_tiles(o_ref, w):
    for t in range(o_ref.shape[0]):
        o_ref[t] = w[:, t * MXU_N:(t + 1) * MXU_N]


def _weight_prep_kernel(a_ref, b_ref, o_ref):
    skip = 2 * GLA_RANK
    scale = jnp.where(pl.program_id(0) >= N_FULL_SCALE_GROUPS, 0.5, 1.0).astype(F32)

    @pl.when(pl.program_id(0) < N_LO_GROUPS)
    def _():
        _store_column_tiles(o_ref, (a_ref[...] * scale).T.astype(BF16))

    @pl.when(pl.program_id(0) >= N_LO_GROUPS)
    def _():
        _store_column_tiles(o_ref, (jnp.concatenate([a_ref[skip:, :], b_ref[...]], axis=0) * scale).T.astype(BF16))


def _weight_prep(w_t):
    K = w_t.shape[1]
    D = D_MODEL
    skip = 2 * GLA_RANK
    n_groups = (w_t.shape[0] - skip) // D
    return pl.pallas_call(
        _weight_prep_kernel,
        grid=(n_groups,),
        in_specs=[
            pl.BlockSpec((D, K), lambda g: (g, 0)),
            pl.BlockSpec((skip, K), lambda g: ((g + 1) * (D // skip), 0)),
        ],
        out_specs=pl.BlockSpec((D // MXU_N, K, MXU_N), lambda g: (g, 0, 0)),
        out_shape=jax.ShapeDtypeStruct((n_groups * D // MXU_N, K, MXU_N), BF16),
        compiler_params=pltpu.CompilerParams(dimension_semantics=("arbitrary",),
                                             vmem_limit_bytes=VMEM_LIMIT_BYTES),
        name="weight_prep",
    )(w_t, w_t)


def _proj_kernel(x_ref, mod_ref, ng_ref, wm_ref, wra_ref, wa_ref, ba_ref, lng_ref, lnb_ref, ws_ref, bsm_ref,
                 wbg_ref, tri_ref, gl_ref, d_ref, v_ref, gz_ref, sgl_ref, mg_ref, sg_scr):
    D = D_MODEL
    DK = GLA_DK
    hk = GLA_HK
    C = GLA_CHUNK
    tm = x_ref.shape[0]
    half = tm // 2
    shift = mod_ref[0, 0:1, :]
    gain = ng_ref[...] * (1.0 + mod_ref[0, 1:2, :])

    def normed(rows):
        x = x_ref[rows, :]
        r = lax.rsqrt(jnp.mean(x * x, axis=-1, keepdims=True) + EPS)
        return ((x * r) * gain + shift).astype(BF16)

    h_top = normed(slice(0, half))
    h_bot = normed(slice(half, tm))
    h = jnp.concatenate([h_top, h_bot], axis=0)

    def proj(j, lhs=None):
        lhs = h if lhs is None else lhs
        tiles = D // MXU_N
        return jnp.concatenate([_dot(lhs, wm_ref[j * tiles + t]) for t in range(tiles)], axis=1)


    qk = jnp.concatenate([proj(0, h_top), proj(0, h_bot)], axis=0)
    ra = jnp.concatenate([_dot_nt(h_top, wra_ref[...]), _dot_nt(h_bot, wra_ref[...])], axis=0).astype(BF16)
    pre = _dot(ra, wa_ref[...]) + ba_ref[...]
    la = _log2_decay(pre)

    v_ref[...] = proj(1).astype(BF16)

    hi = la.astype(BF16)
    r1 = la - hi.astype(F32)
    mid = r1.astype(BF16)
    lo = (r1 - mid.astype(F32)).astype(BF16)

    u_pre = proj(3)

    tri_f = tri_ref[0]
    tri_b = tri_ref[1]
    cum = []
    for blk in range(tm // GLA_BLK):
        rs = slice(blk * GLA_BLK, (blk + 1) * GLA_BLK)
        b_f = _dot(tri_f, hi[rs, :DK]) + _dot(tri_f, mid[rs, :DK]) + _dot(tri_f, lo[rs, :DK])
        b_b = _dot(tri_b, hi[rs, DK:]) + _dot(tri_b, mid[rs, DK:]) + _dot(tri_b, lo[rs, DK:])
        cum.append((b_f, b_b))

    sgl_ref[...] = _sigmoid_h(proj(6)).astype(BF16)

    q_scale = GLA_HK ** -0.5
    for blk in range(tm // GLA_BLK):
        b_f, b_b = cum[blk]
        for c in range(CHUNKS_PER_BLK):
            cs = slice(c * C, (c + 1) * C)
            gr = slice(blk * GLA_BLK + c * C, blk * GLA_BLK + (c + 1) * C)
            n = blk * CHUNKS_PER_BLK + c
            q = qk[gr, :DK] * q_scale
            k = qk[gr, DK:]
            bf = b_f[cs]
            bb = b_b[cs]
            bf_last = bf[C - 1:C]
            bb_last = bb[0:1]
            operands = (q * jnp.exp2(bf), k * jnp.exp2(-bf), k * jnp.exp2(bf_last - bf),
                        q * jnp.exp2(bb), k * jnp.exp2(-bb), k * jnp.exp2(bb_last - bb))
            decays = (jnp.exp2(bf_last), jnp.exp2(bb_last))
            for hd in range(GLA_HEADS):
                hs = slice(hd * hk, (hd + 1) * hk)
                for g, val in enumerate(operands):
                    lane0 = (hd * N_GLA_OPERANDS + g) * hk
                    gl_ref[gr, lane0:lane0 + hk] = val[:, hs].astype(BF16)
                for g, val in enumerate(decays):
                    lane0 = (hd * len(decays) + g) * hk
                    d_ref[n:n + 1, lane0:lane0 + hk] = val[:, hs]

    zg = proj(5)

    vs = _gelu_exact_h(proj(4))
    mu = jnp.mean(vs, axis=-1, keepdims=True)
    vc = vs - mu
    var = jnp.mean(vc * vc, axis=-1, keepdims=True)
    vsn = ((vc * lax.rsqrt(var + EPS)) * lng_ref[...] + lnb_ref[...]).astype(BF16)

    gz_ref[...] = _silu_h(proj(2)).astype(BF16)

    n_pos = tm // GMLP_CHUNK
    for g in range(GMLP_GROUPS):
        cs = slice(g * GMLP_GC, (g + 1) * GMLP_GC)
        rhs = jnp.concatenate([vsn[n * GMLP_CHUNK:(n + 1) * GMLP_CHUNK, cs] for n in range(n_pos)], axis=1)
        mixed = _dot(ws_ref[g], rhs)
        for n in range(n_pos):
            sg_scr[n * GMLP_CHUNK:(n + 1) * GMLP_CHUNK, cs] = (
                mixed[:, n * GMLP_GC:(n + 1) * GMLP_GC] + bsm_ref[:, cs])
    merge_gate = _sigmoid_h(proj(7))

    g2 = ((_gelu_exact_h(u_pre) * sg_scr[...]) * _silu_h(zg)).astype(BF16)
    mg_ref[...] = (merge_gate * _dot(g2, wbg_ref[...])).astype(BF16)


def _cumsum_matrices():
    ti, si = _block_iotas()
    same_chunk = (ti >> CHUNK_SHIFT) == (si >> CHUNK_SHIFT)
    return jnp.stack([same_chunk & (si <= ti), same_chunk & (si >= ti)]).astype(BF16)


def _projection(x2, mod3, norm_g, w_main, w_ra, wa, ba, ln_g, ln_b, ws, bsm, w_br_gmlp, seq):
    T, D = x2.shape
    tm = PROJ_TM
    assert T % tm == 0 and seq % tm == 0 and tm % GMLP_CHUNK == 0 and tm % GLA_BLK == 0
    resident = functools.partial(pl.BlockSpec, pipeline_mode=pl.Buffered(1))
    tok = lambda w: pl.BlockSpec((tm, w), lambda i: (i, 0))
    tiles_per_seq = seq // tm
    out_bf = jax.ShapeDtypeStruct((T, D), BF16)
    return pl.pallas_call(
        _proj_kernel,
        grid=(T // tm,),
        in_specs=[
            tok(D),
            pl.BlockSpec((1, 3, D), lambda i: (i // tiles_per_seq, 0, 0)),
            resident((1, D), lambda i: (0, 0)),
            resident(w_main.shape, lambda i: (0, 0, 0)),
            resident(w_ra.shape, lambda i: (0, 0)),
            resident(wa.shape, lambda i: (0, 0)),
            resident(ba.shape, lambda i: (0, 0)),
            resident((1, D), lambda i: (0, 0)),
            resident((1, D), lambda i: (0, 0)),
            resident(ws.shape, lambda i: (0, 0, 0)),
            resident(bsm.shape, lambda i: (0, 0)),
            resident(w_br_gmlp.shape, lambda i: (0, 0)),
            resident((2, GLA_BLK, GLA_BLK), lambda i: (0, 0, 0)),
        ],
        out_specs=[tok(N_GLA_OPERANDS * GLA_DK), pl.BlockSpec((tm // GLA_CHUNK, 2 * GLA_DK), lambda i: (i, 0)),
                   tok(D), tok(D), tok(D), tok(D)],
        out_shape=[jax.ShapeDtypeStruct((T, N_GLA_OPERANDS * GLA_DK), BF16),
                   jax.ShapeDtypeStruct((T // GLA_CHUNK, 2 * GLA_DK), F32),
                   out_bf, out_bf, out_bf, out_bf],
        scratch_shapes=[pltpu.VMEM((tm, D), F32)],
        compiler_params=pltpu.CompilerParams(dimension_semantics=("arbitrary",),
                                             vmem_limit_bytes=VMEM_LIMIT_BYTES),
        name="projection_gmlp",
    )(x2, mod3, norm_g, w_main, w_ra, wa, ba, ln_g, ln_b, ws, bsm, w_br_gmlp, _cumsum_matrices())


def _gla_kernel(gl_ref, v_ref, d_ref, gz_ref, gng_ref, o_ref, of_scr, ob_scr, kvf_scr, kvb_scr):
    S = v_ref.shape[1]
    C = GLA_CHUNK
    hk = GLA_HK
    n_blk = S // GLA_BLK
    ti, si = _block_iotas()
    chunk_lo = (ti >> CHUNK_SHIFT) << CHUNK_SHIFT
    chunk_hi = chunk_lo + (C - 1)
    pos_chunk = lax.broadcasted_iota(jnp.int32, (hk, GLA_BLK), 1) >> CHUNK_SHIFT
    chunk_sel = [jnp.where(pos_chunk == c, 1.0, 0.0).astype(BF16) for c in range(CHUNKS_PER_BLK)]
    d_cols = d_ref[0].T
    hv = GLA_HV

    def direction(hd, forward):
        lane0 = (hd * N_GLA_OPERANDS + (0 if forward else N_GLA_OPERANDS // 2)) * hk
        row0 = (2 * hd + (0 if forward else 1)) * hk
        return (hd, lane0, lane0 + hk, lane0 + 2 * hk, d_cols[row0:row0 + hk],
                of_scr if forward else ob_scr, kvf_scr if forward else kvb_scr)

    def operand(rows, lane0):
        return gl_ref[0, rows, lane0:lane0 + hk]

    def values(hd, rows):
        return v_ref[0, rows, hd * hv:(hd + 1) * hv]

    def scores(j, ctx, forward):
        rows = _aligned_rows(j * GLA_BLK, GLA_BLK)
        a = _dot_nt(operand(rows, ctx[1]), operand(rows, ctx[2]))
        if forward:
            a = jnp.where(si <= ti, jnp.where(si >= chunk_lo, a, 0.0), 0.0)
        else:
            a = jnp.where(si > ti, jnp.where(si <= chunk_hi, a, 0.0), 0.0)
        return a.astype(BF16)

    def chunk_states(j, ctx):
        rows = _aligned_rows(j * GLA_BLK, GLA_BLK)
        ke = operand(rows, ctx[3])
        ke_t = ke.T
        ke_bd = jnp.concatenate([ke_t * chunk_sel[c] for c in range(CHUNKS_PER_BLK)], axis=0)
        ctx[6][j] = _dot(ke_bd, values(ctx[0], rows))

    def weighted_values(j, a, ctx):
        rows = _aligned_rows(j * GLA_BLK, GLA_BLK)
        ctx[5][ctx[0], rows, :] = _dot(a, values(ctx[0], rows))

    def scan(j, state, ctx, forward):
        hd, qt_lane, _, _, decays, o_scr, kv_scr = ctx
        o_inter = [None] * CHUNKS_PER_BLK
        for c in (range(CHUNKS_PER_BLK) if forward else reversed(range(CHUNKS_PER_BLK))):
            rows = _aligned_rows(j * GLA_BLK + c * C, C)
            n = j * CHUNKS_PER_BLK + c
            o_inter[c] = _dot(operand(rows, qt_lane), state.astype(BF16))
            state = state * decays[:, n:n + 1] + kv_scr[j, c * hk:(c + 1) * hk, :]
        o_scr[hd, _aligned_rows(j * GLA_BLK, GLA_BLK), :] += jnp.concatenate(o_inter, axis=0)
        return state

    def finish(hd, j):
        rows = _aligned_rows(j * GLA_BLK, GLA_BLK)
        lanes = slice(hd * hv, (hd + 1) * hv)
        o = of_scr[hd, rows, :] + ob_scr[hd, rows, :]
        on = o * lax.rsqrt(jnp.mean(o * o, axis=-1, keepdims=True) + EPS) * gng_ref[hd]
        o_ref[0, rows, lanes] = (on * gz_ref[0, rows, lanes].astype(F32)).astype(BF16)

    steps = [(hd, i) for hd in range(GLA_HEADS_PER_STEP) for i in range(n_blk)]
    sf = sb = None
    for cur, nxt in zip([None] + steps, steps + [None]):
        if nxt is not None:
            n_fwd, n_bwd = direction(nxt[0], True), direction(nxt[0], False)
            a_f = scores(nxt[1], n_fwd, True)
            a_b = scores(n_blk - 1 - nxt[1], n_bwd, False)
        if cur is not None:
            hd, i = cur
            if i == 0:
                sf = sb = jnp.zeros((hk, hv), F32)
            sf = scan(i, sf, direction(hd, True), True)
            sb = scan(n_blk - 1 - i, sb, direction(hd, False), False)
        if nxt is not None:
            chunk_states(nxt[1], n_fwd)
            chunk_states(n_blk - 1 - nxt[1], n_bwd)
            weighted_values(nxt[1], a_f, n_fwd)
            weighted_values(n_blk - 1 - nxt[1], a_b, n_bwd)
        if cur is not None:
            for j in sorted({i, n_blk - 1 - i}):
                if max(j, n_blk - 1 - j) == i:
                    finish(hd, j)


def _gla(gl3, v3, d3, gz3, gng):
    B, S, D = v3.shape
    H, hk, hv = GLA_HEADS, GLA_HK, GLA_HV
    n_chunks = S // GLA_CHUNK
    assert S % GLA_BLK == 0
    hps = GLA_HEADS_PER_STEP
    assert H % hps == 0
    head_v = pl.BlockSpec((1, S, hps * hv), lambda b, h: (b, 0, h))
    kv_scratch = pltpu.VMEM((S // GLA_BLK, CHUNKS_PER_BLK * hk, hv), F32)
    o_scratch = pltpu.VMEM((hps, S, hv), F32)
    return pl.pallas_call(
        _gla_kernel,
        grid=(B, H // hps),
        in_specs=[
            pl.BlockSpec((1, S, hps * N_GLA_OPERANDS * hk), lambda b, h: (b, 0, h)),
            head_v,
            pl.BlockSpec((1, n_chunks, hps * 2 * hk), lambda b, h: (b, 0, h)),
            head_v,
            pl.BlockSpec((hps, 1, hv), lambda b, h: (h, 0, 0)),
        ],
        out_specs=head_v,
        out_shape=jax.ShapeDtypeStruct((B, S, D), BF16),
        scratch_shapes=[o_scratch, o_scratch, kv_scratch, kv_scratch],
        compiler_params=pltpu.CompilerParams(dimension_semantics=("arbitrary", "arbitrary"),
                                             vmem_limit_bytes=VMEM_LIMIT_BYTES),
        name="gla_bidir",
    )(gl3, v3, d3, gz3, gng)


def _out_kernel(x_ref, g1_ref, sgl_ref, mg_ref, mod_ref, wbr_ref, wout_ref, fg_ref, o_ref):
    gate = mod_ref[0, 2:3, :]
    n_sub = x_ref.shape[0] // OUT_SUB
    subs = [slice(s * OUT_SUB, (s + 1) * OUT_SUB) for s in range(n_sub)]
    y_gla = [None] * n_sub
    y_gla[0] = _dot(g1_ref[subs[0], :], wbr_ref[...])
    for s, rows in enumerate(subs):
        if s + 1 < n_sub:
            y_gla[s + 1] = _dot(g1_ref[subs[s + 1], :], wbr_ref[...])
        merged = (sgl_ref[rows, :].astype(F32) * y_gla[s] + mg_ref[rows, :].astype(F32)).astype(BF16)
        r = x_ref[rows, :] + gate * _dot(merged, wout_ref[...])
        o_ref[rows, :] = (r * lax.rsqrt(jnp.mean(r * r, axis=-1, keepdims=True) + EPS)) * fg_ref[...]


def _merge_out(x2, g1, sgl, mg, mod3, w_br_gla, w_out, final_g, seq):
    T, D = x2.shape
    tm = OUT_TM
    assert T % tm == 0 and seq % tm == 0
    resident = functools.partial(pl.BlockSpec, pipeline_mode=pl.Buffered(1))
    tok = pl.BlockSpec((tm, D), lambda i: (i, 0))
    tiles_per_seq = seq // tm
    return pl.pallas_call(
        _out_kernel,
        grid=(T // tm,),
        in_specs=[
            tok, tok, tok, tok,
            pl.BlockSpec((1, 3, D), lambda i: (i // tiles_per_seq, 0, 0)),
            resident((D, D), lambda i: (0, 0)),
            resident((D, D), lambda i: (0, 0)),
            resident((1, D), lambda i: (0, 0)),
        ],
        out_specs=tok,
        out_shape=jax.ShapeDtypeStruct((T, D), F32),
        compiler_params=pltpu.CompilerParams(dimension_semantics=("arbitrary",),
                                             vmem_limit_bytes=VMEM_LIMIT_BYTES),
        name="merge_out",
    )(x2, g1, sgl, mg, mod3, w_br_gla, w_out, final_g)


def _layer(x, c, norm_g, w_ada, b_ada, w_in, alpha_fw_w, alpha_fw_b, alpha_bw_w, alpha_bw_b,
           gla_norm_g, gmlp_ln_g, gmlp_ln_b, gmlp_ws, gmlp_bs, w_br_gla, w_br_gmlp, w_out, out_g):
    B, S, D = x.shape
    H, hv, R, DK = GLA_HEADS, GLA_HV, GLA_RANK, GLA_DK
    T = B * S

    o_ra = 2 * DK + 2 * D
    assert o_ra == N_LO_GROUPS * D
    w_t = w_in.T
    w_main = _weight_prep(w_t)
    w_ra = jnp.pad(w_t[o_ra:o_ra + 2 * R], ((0, RA_PAD - 2 * R), (0, 0))).astype(BF16)
    zeros = jnp.zeros_like(alpha_fw_w)
    wa = jnp.concatenate([jnp.concatenate([alpha_fw_w, zeros], axis=1),
                          jnp.concatenate([zeros, alpha_bw_w], axis=1)], axis=0)
    wa = jnp.pad(wa, ((0, RA_PAD - 2 * R), (0, 0))).astype(BF16)
    ba = jnp.concatenate([alpha_fw_b, alpha_bw_b]).reshape(1, 2 * DK)
    bsm = jnp.repeat(gmlp_bs.T, GMLP_GC, axis=1)

    mod3 = _adaln_mod(c, w_ada, b_ada).reshape(B, 3, D)
    x2 = x.reshape(T, D)
    gl, d, v, gz, sgl, mg = _projection(
        x2, mod3, norm_g.reshape(1, D), w_main, w_ra, wa, ba, gmlp_ln_g.reshape(1, D), gmlp_ln_b.reshape(1, D),
        gmlp_ws.astype(BF16), bsm, w_br_gmlp.astype(BF16), S)
    g1 = _gla(gl.reshape(B, S, N_GLA_OPERANDS * DK), v.reshape(B, S, D), d.reshape(B, S // GLA_CHUNK, 2 * DK),
              gz.reshape(B, S, D), gla_norm_g.reshape(H, 1, hv))
    out = _merge_out(x2, g1.reshape(T, D), sgl, mg, mod3, w_br_gla.astype(BF16), w_out.astype(BF16),
                     out_g.reshape(1, D), S)
    return out.reshape(B, S, D)


def kernel(x, c, norm_g, w_ada, b_ada, w_in, alpha_fw_w, alpha_fw_b, alpha_bw_w, alpha_bw_b, gla_norm_g,
           gmlp_ln_g, gmlp_ln_b, gmlp_ws, gmlp_bs, w_br_gla, w_br_gmlp, w_out, final_g):
    depth = norm_g.shape[0]
    assert depth == 1, "the final rmsnorm is fused into the layer's output kernel"
    return _layer(x, c, norm_g[0], w_ada[0], b_ada[0], w_in[0], alpha_fw_w[0], alpha_fw_b[0], alpha_bw_w[0],
                  alpha_bw_b[0], gla_norm_g[0], gmlp_ln_g[0], gmlp_ln_b[0], gmlp_ws[0], gmlp_bs[0],
                  w_br_gla[0], w_br_gmlp[0], w_out[0], final_g)
```

```python
import functools

import jax
import jax.numpy as jnp
from jax import lax
from jax.experimental import pallas as pl
from jax.experimental.pallas import tpu as pltpu

F32 = jnp.float32
BF16 = jnp.bfloat16

D_MODEL = 1024
GLA_HEADS = 4
GLA_HK = 128
GLA_HV = 256
GLA_DK = GLA_HEADS * GLA_HK
GLA_RANK = 16
GLA_TAU = 16.0
GLA_CHUNK = 64
LOG_DECAY_FLOOR = -1.25
GMLP_GROUPS = 8
GMLP_GC = 128
GMLP_CHUNK = 128
EPS = 1e-6
SQRT_TWO = 1.4142135623730951
LOG2_E = 1.4426950408889634

RA_PAD = 128
MXU_N = 256
N_LO_GROUPS = 3
N_FULL_SCALE_GROUPS = 2
VMEM_LIMIT_BYTES = 56 * 1024 * 1024

PROJ_TM = 512
OUT_TM = 1024
OUT_SUB = 256
GLA_BLK = 256
CHUNKS_PER_BLK = GLA_BLK // GLA_CHUNK
CHUNK_SHIFT = GLA_CHUNK.bit_length() - 1
N_GLA_OPERANDS = 6
GLA_HEADS_PER_STEP = 2


def _sigmoid_h(hz):
    return 0.5 * jnp.tanh(hz) + 0.5


def _silu_h(hz):
    return hz * jnp.tanh(hz) + hz


def _gelu_exact_h(hz):
    return hz * (1.0 + lax.erf(hz * SQRT_TWO))


def _log2_decay(z):
    soft = jnp.log(1.0 + jnp.exp(-z))
    return jnp.maximum(soft * (-LOG2_E / GLA_TAU), LOG_DECAY_FLOOR * LOG2_E)


def _dot(a, b):
    return jnp.dot(a, b, preferred_element_type=F32)


def _dot_nt(a, b):
    return lax.dot_general(a, b, (((1,), (1,)), ((), ())), preferred_element_type=F32)


def _dot_tn(a, b):
    return lax.dot_general(a, b, (((0,), (0,)), ((), ())), preferred_element_type=F32)


def _aligned_rows(start, size):
    if isinstance(start, int):
        return pl.ds(start, size)
    return pl.ds(pl.multiple_of(start, size), size)


def _block_iotas():
    ti = lax.broadcasted_iota(jnp.int32, (GLA_BLK, GLA_BLK), 0)
    si = lax.broadcasted_iota(jnp.int32, (GLA_BLK, GLA_BLK), 1)
    return ti, si


def _mod_kernel(c_ref, w_ref, b_ref, o_ref):
    a = _silu_h(0.5 * c_ref[...]).astype(BF16)
    o_ref[...] = _dot(a, w_ref[...].astype(BF16)) + b_ref[...]


def _adaln_mod(c, w_ada, b_ada):
    B, D = c.shape
    n_out = w_ada.shape[1]
    return pl.pallas_call(
        _mod_kernel,
        grid=(n_out // D,),
        in_specs=[
            pl.BlockSpec((B, D), lambda j: (0, 0)),
            pl.BlockSpec((D, D), lambda j: (0, j)),
            pl.BlockSpec((1, D), lambda j: (0, j)),
        ],
        out_specs=pl.BlockSpec((B, D), lambda j: (0, j)),
        out_shape=jax.ShapeDtypeStruct((B, n_out), F32),
        compiler_params=pltpu.CompilerParams(dimension_semantics=("arbitrary",)),
        name="adaln_mod",
    )(c, w_ada, b_ada.reshape(1, n_out))


def _store_column_tiles(o_ref, w):
    for t in range(o_ref.shape[0]):
        o_ref[t] = w[:, t * MXU_N:(t + 1) * MXU_N]


def _weight_prep_kernel(a_ref, b_ref, o_ref):
    skip = 2 * GLA_RANK
    scale = jnp.where(pl.program_id(0) >= N_FULL_SCALE_GROUPS, 0.5, 1.0).astype(F32)

    @pl.when(pl.program_id(0) < N_LO_GROUPS)
    def _():
        _store_column_tiles(o_ref, (a_ref[...] * scale).T.astype(BF16))

    @pl.when(pl.program_id(0) >= N_LO_GROUPS)
    def _():
        _store_column_tiles(o_ref, (jnp.concatenate([a_ref[skip:, :], b_ref[...]], axis=0) * scale).T.astype(BF16))


def _weight_prep(w_t):
    K = w_t.shape[1]
    D = D_MODEL
    skip = 2 * GLA_RANK
    n_groups = (w_t.shape[0] - skip) // D
    return pl.pallas_call(
        _weight_prep_kernel,
        grid=(n_groups,),
        in_specs=[
            pl.BlockSpec((D, K), lambda g: (g, 0)),
            pl.BlockSpec((skip, K), lambda g: ((g + 1) * (D // skip), 0)),
        ],
        out_specs=pl.BlockSpec((D // MXU_N, K, MXU_N), lambda g: (g, 0, 0)),
        out_shape=jax.ShapeDtypeStruct((n_groups * D // MXU_N, K, MXU_N), BF16),
        compiler_params=pltpu.CompilerParams(dimension_semantics=("arbitrary",),
                                             vmem_limit_bytes=VMEM_LIMIT_BYTES),
        name="weight_prep",
    )(w_t, w_t)


def _proj_kernel(x_ref, mod_ref, ng_ref, wm_ref, wra_ref, wa_ref, ba_ref, lng_ref, lnb_ref, ws_ref, bsm_ref,
                 wbg_ref, tri_ref, gl_ref, d_ref, v_ref, gz_ref, sgl_ref, mg_ref, sg_scr):
    D = D_MODEL
    DK = GLA_DK
    hk = GLA_HK
    C = GLA_CHUNK
    tm = x_ref.shape[0]
    half = tm // 2
    shift = mod_ref[0, 0:1, :]
    gain = ng_ref[...] * (1.0 + mod_ref[0, 1:2, :])

    def normed(rows):
        x = x_ref[rows, :]
        r = lax.rsqrt(jnp.mean(x * x, axis=-1, keepdims=True) + EPS)
        return ((x * r) * gain + shift).astype(BF16)

    h_top = normed(slice(0, half))
    h_bot = normed(slice(half, tm))
    h = jnp.concatenate([h_top, h_bot], axis=0)

    def proj(j, lhs=None):
        lhs = h if lhs is None else lhs
        tiles = D // MXU_N
        return jnp.concatenate([_dot(lhs, wm_ref[j * tiles + t]) for t in range(tiles)], axis=1)


    qk = jnp.concatenate([proj(0, h_top), proj(0, h_bot)], axis=0)
    ra = jnp.concatenate([_dot_nt(h_top, wra_ref[...]), _dot_nt(h_bot, wra_ref[...])], axis=0).astype(BF16)
    pre = _dot(ra, wa_ref[...]) + ba_ref[...]
    la = _log2_decay(pre)

    v_ref[...] = proj(1).astype(BF16)

    hi = la.astype(BF16)
    r1 = la - hi.astype(F32)
    mid = r1.astype(BF16)
    lo = (r1 - mid.astype(F32)).astype(BF16)

    u_pre = proj(3)

    tri_f = tri_ref[0]
    tri_b = tri_ref[1]
    cum = []
    for blk in range(tm // GLA_BLK):
        rs = slice(blk * GLA_BLK, (blk + 1) * GLA_BLK)
        b_f = _dot(tri_f, hi[rs, :DK]) + _dot(tri_f, mid[rs, :DK]) + _dot(tri_f, lo[rs, :DK])
        b_b = _dot(tri_b, hi[rs, DK:]) + _dot(tri_b, mid[rs, DK:]) + _dot(tri_b, lo[rs, DK:])
        cum.append((b_f, b_b))

    sgl_ref[...] = _sigmoid_h(proj(6)).astype(BF16)

    q_scale = GLA_HK ** -0.5
    for blk in range(tm // GLA_BLK):
        b_f, b_b = cum[blk]
        for c in range(CHUNKS_PER_BLK):
            cs = slice(c * C, (c + 1) * C)
            gr = slice(blk * GLA_BLK + c * C, blk * GLA_BLK + (c + 1) * C)
            n = blk * CHUNKS_PER_BLK + c
            q = qk[gr, :DK] * q_scale
            k = qk[gr, DK:]
            bf = b_f[cs]
            bb = b_b[cs]
            bf_last = bf[C - 1:C]
            bb_last = bb[0:1]
            ke_f = k * jnp.exp2(bf_last - bf)
            ke_b = k * jnp.exp2(bb_last - bb)
            operands = (q * jnp.exp2(bf), ke_f * jnp.exp2(-bf_last), ke_f,
                        q * jnp.exp2(bb), ke_b * jnp.exp2(-bb_last), ke_b)
            decays = (jnp.exp2(bf_last), jnp.exp2(bb_last))
            for hd in range(GLA_HEADS):
                hs = slice(hd * hk, (hd + 1) * hk)
                for g, val in enumerate(operands):
                    lane0 = (hd * N_GLA_OPERANDS + g) * hk
                    gl_ref[gr, lane0:lane0 + hk] = val[:, hs].astype(BF16)
                for g, val in enumerate(decays):
                    lane0 = (hd * len(decays) + g) * hk
                    d_ref[n:n + 1, lane0:lane0 + hk] = val[:, hs]

    zg = proj(5)

    vs = _gelu_exact_h(proj(4))
    mu = jnp.mean(vs, axis=-1, keepdims=True)
    vc = vs - mu
    var = jnp.mean(vc * vc, axis=-1, keepdims=True)
    vsn = ((vc * lax.rsqrt(var + EPS)) * lng_ref[...] + lnb_ref[...]).astype(BF16)

    gz_ref[...] = _silu_h(proj(2)).astype(BF16)

    n_pos = tm // GMLP_CHUNK
    for g in range(GMLP_GROUPS):
        cs = slice(g * GMLP_GC, (g + 1) * GMLP_GC)
        rhs = jnp.concatenate([vsn[n * GMLP_CHUNK:(n + 1) * GMLP_CHUNK, cs] for n in range(n_pos)], axis=1)
        mixed = _dot(ws_ref[g], rhs)
        for n in range(n_pos):
            sg_scr[n * GMLP_CHUNK:(n + 1) * GMLP_CHUNK, cs] = (
                mixed[:, n * GMLP_GC:(n + 1) * GMLP_GC] + bsm_ref[:, cs])
    merge_gate = _sigmoid_h(proj(7))

    g2 = ((_gelu_exact_h(u_pre) * sg_scr[...]) * _silu_h(zg)).astype(BF16)
    mg_ref[...] = (merge_gate * _dot(g2, wbg_ref[...])).astype(BF16)


def _cumsum_matrices():
    ti, si = _block_iotas()
    same_chunk = (ti >> CHUNK_SHIFT) == (si >> CHUNK_SHIFT)
    return jnp.stack([same_chunk & (si <= ti), same_chunk & (si >= ti)]).astype(BF16)


def _projection(x2, mod3, norm_g, w_main, w_ra, wa, ba, ln_g, ln_b, ws, bsm, w_br_gmlp, seq):
    T, D = x2.shape
    tm = PROJ_TM
    assert T % tm == 0 and seq % tm == 0 and tm % GMLP_CHUNK == 0 and tm % GLA_BLK == 0
    resident = functools.partial(pl.BlockSpec, pipeline_mode=pl.Buffered(1))
    tok = lambda w: pl.BlockSpec((tm, w), lambda i: (i, 0))
    tiles_per_seq = seq // tm
    out_bf = jax.ShapeDtypeStruct((T, D), BF16)
    return pl.pallas_call(
        _proj_kernel,
        grid=(T // tm,),
        in_specs=[
            tok(D),
            pl.BlockSpec((1, 3, D), lambda i: (i // tiles_per_seq, 0, 0)),
            resident((1, D), lambda i: (0, 0)),
            resident(w_main.shape, lambda i: (0, 0, 0)),
            resident(w_ra.shape, lambda i: (0, 0)),
            resident(wa.shape, lambda i: (0, 0)),
            resident(ba.shape, lambda i: (0, 0)),
            resident((1, D), lambda i: (0, 0)),
            resident((1, D), lambda i: (0, 0)),
            resident(ws.shape, lambda i: (0, 0, 0)),
            resident(bsm.shape, lambda i: (0, 0)),
            resident(w_br_gmlp.shape, lambda i: (0, 0)),
            resident((2, GLA_BLK, GLA_BLK), lambda i: (0, 0, 0)),
        ],
        out_specs=[tok(N_GLA_OPERANDS * GLA_DK), pl.BlockSpec((tm // GLA_CHUNK, 2 * GLA_DK), lambda i: (i, 0)),
                   tok(D), tok(D), tok(D), tok(D)],
        out_shape=[jax.ShapeDtypeStruct((T, N_GLA_OPERANDS * GLA_DK), BF16),
                   jax.ShapeDtypeStruct((T // GLA_CHUNK, 2 * GLA_DK), F32),
                   out_bf, out_bf, out_bf, out_bf],
        scratch_shapes=[pltpu.VMEM((tm, D), F32)],
        compiler_params=pltpu.CompilerParams(dimension_semantics=("arbitrary",),
                                             vmem_limit_bytes=VMEM_LIMIT_BYTES),
        name="projection_gmlp",
    )(x2, mod3, norm_g, w_main, w_ra, wa, ba, ln_g, ln_b, ws, bsm, w_br_gmlp, _cumsum_matrices())


def _gla_kernel(gl_ref, v_ref, d_ref, gz_ref, gng_ref, o_ref, of_scr, ob_scr, kvf_scr, kvb_scr):
    S = v_ref.shape[1]
    C = GLA_CHUNK
    hk = GLA_HK
    n_blk = S // GLA_BLK
    ti, si = _block_iotas()
    chunk_lo = (ti >> CHUNK_SHIFT) << CHUNK_SHIFT
    chunk_hi = chunk_lo + (C - 1)
    pos_chunk = lax.broadcasted_iota(jnp.int32, (hk, GLA_BLK), 1) >> CHUNK_SHIFT
    chunk_sel = [jnp.where(pos_chunk == c, 1.0, 0.0).astype(BF16) for c in range(CHUNKS_PER_BLK)]
    d_cols = d_ref[0].T
    hv = GLA_HV

    def direction(hd, forward):
        lane0 = (hd * N_GLA_OPERANDS + (0 if forward else N_GLA_OPERANDS // 2)) * hk
        row0 = (2 * hd + (0 if forward else 1)) * hk
        return (hd, lane0, lane0 + hk, lane0 + 2 * hk, d_cols[row0:row0 + hk],
                of_scr if forward else ob_scr, kvf_scr if forward else kvb_scr)

    def operand(rows, lane0):
        return gl_ref[0, rows, lane0:lane0 + hk]

    def values(hd, rows):
        return v_ref[0, rows, hd * hv:(hd + 1) * hv]

    def scores(j, ctx, forward):
        rows = _aligned_rows(j * GLA_BLK, GLA_BLK)
        a = _dot_nt(operand(rows, ctx[1]), operand(rows, ctx[2]))
        if forward:
            a = jnp.where(si <= ti, jnp.where(si >= chunk_lo, a, 0.0), 0.0)
        else:
            a = jnp.where(si > ti, jnp.where(si <= chunk_hi, a, 0.0), 0.0)
        return a.astype(BF16)

    def chunk_states(j, ctx):
        rows = _aligned_rows(j * GLA_BLK, GLA_BLK)
        ke = operand(rows, ctx[3])
        ke_t = ke.T
        ke_bd = jnp.concatenate([ke_t * chunk_sel[c] for c in range(CHUNKS_PER_BLK)], axis=0)
        ctx[6][j] = _dot(ke_bd, values(ctx[0], rows))

    def weighted_values(j, a, ctx):
        rows = _aligned_rows(j * GLA_BLK, GLA_BLK)
        ctx[5][ctx[0], rows, :] = _dot(a, values(ctx[0], rows))

    def scan(j, state, ctx, forward):
        hd, qt_lane, _, _, decays, o_scr, kv_scr = ctx
        o_inter = [None] * CHUNKS_PER_BLK
        for c in (range(CHUNKS_PER_BLK) if forward else reversed(range(CHUNKS_PER_BLK))):
            rows = _aligned_rows(j * GLA_BLK + c * C, C)
            n = j * CHUNKS_PER_BLK + c
            o_inter[c] = _dot(operand(rows, qt_lane), state.astype(BF16))
            state = state * decays[:, n:n + 1] + kv_scr[j, c * hk:(c + 1) * hk, :]
        o_scr[hd, _aligned_rows(j * GLA_BLK, GLA_BLK), :] += jnp.concatenate(o_inter, axis=0)
        return state

    def finish(hd, j):
        rows = _aligned_rows(j * GLA_BLK, GLA_BLK)
        lanes = slice(hd * hv, (hd + 1) * hv)
        o = of_scr[hd, rows, :] + ob_scr[hd, rows, :]
        on = o * lax.rsqrt(jnp.mean(o * o, axis=-1, keepdims=True) + EPS) * gng_ref[hd]
        o_ref[0, rows, lanes] = (on * gz_ref[0, rows, lanes].astype(F32)).astype(BF16)

    steps = [(hd, i) for hd in range(GLA_HEADS_PER_STEP) for i in range(n_blk)]
    sf = sb = None
    for cur, nxt in zip([None] + steps, steps + [None]):
        if nxt is not None:
            n_fwd, n_bwd = direction(nxt[0], True), direction(nxt[0], False)
            a_f = scores(nxt[1], n_fwd, True)
            a_b = scores(n_blk - 1 - nxt[1], n_bwd, False)
        if cur is not None:
            hd, i = cur
            if i == 0:
                sf = sb = jnp.zeros((hk, hv), F32)
            sf = scan(i, sf, direction(hd, True), True)
            sb = scan(n_blk - 1 - i, sb, direction(hd, False), False)
        if nxt is not None:
            chunk_states(nxt[1], n_fwd)
            chunk_states(n_blk - 1 - nxt[1], n_bwd)
            weighted_values(nxt[1], a_f, n_fwd)
            weighted_values(n_blk - 1 - nxt[1], a_b, n_bwd)
        if cur is not None:
            for j in sorted({i, n_blk - 1 - i}):
                if max(j, n_blk - 1 - j) == i:
                    finish(hd, j)


def _gla(gl3, v3, d3, gz3, gng):
    B, S, D = v3.shape
    H, hk, hv = GLA_HEADS, GLA_HK, GLA_HV
    n_chunks = S // GLA_CHUNK
    assert S % GLA_BLK == 0
    hps = GLA_HEADS_PER_STEP
    assert H % hps == 0
    head_v = pl.BlockSpec((1, S, hps * hv), lambda b, h: (b, 0, h))
    kv_scratch = pltpu.VMEM((S // GLA_BLK, CHUNKS_PER_BLK * hk, hv), F32)
    o_scratch = pltpu.VMEM((hps, S, hv), F32)
    return pl.pallas_call(
        _gla_kernel,
        grid=(B, H // hps),
        in_specs=[
            pl.BlockSpec((1, S, hps * N_GLA_OPERANDS * hk), lambda b, h: (b, 0, h)),
            head_v,
            pl.BlockSpec((1, n_chunks, hps * 2 * hk), lambda b, h: (b, 0, h)),
            head_v,
            pl.BlockSpec((hps, 1, hv), lambda b, h: (h, 0, 0)),
        ],
        out_specs=head_v,
        out_shape=jax.ShapeDtypeStruct((B, S, D), BF16),
        scratch_shapes=[o_scratch, o_scratch, kv_scratch, kv_scratch],
        compiler_params=pltpu.CompilerParams(dimension_semantics=("arbitrary", "arbitrary"),
                                             vmem_limit_bytes=VMEM_LIMIT_BYTES),
        name="gla_bidir",
    )(gl3, v3, d3, gz3, gng)


def _out_kernel(x_ref, g1_ref, sgl_ref, mg_ref, mod_ref, wbr_ref, wout_ref, fg_ref, o_ref):
    gate = mod_ref[0, 2:3, :]
    n_sub = x_ref.shape[0] // OUT_SUB
    subs = [slice(s * OUT_SUB, (s + 1) * OUT_SUB) for s in range(n_sub)]
    y_gla = [None] * n_sub
    y_gla[0] = _dot(g1_ref[subs[0], :], wbr_ref[...])
    for s, rows in enumerate(subs):
        if s + 1 < n_sub:
            y_gla[s + 1] = _dot(g1_ref[subs[s + 1], :], wbr_ref[...])
        merged = (sgl_ref[rows, :].astype(F32) * y_gla[s] + mg_ref[rows, :].astype(F32)).astype(BF16)
        r = x_ref[rows, :] + gate * _dot(merged, wout_ref[...])
        o_ref[rows, :] = (r * lax.rsqrt(jnp.mean(r * r, axis=-1, keepdims=True) + EPS)) * fg_ref[...]


def _merge_out(x2, g1, sgl, mg, mod3, w_br_gla, w_out, final_g, seq):
    T, D = x2.shape
    tm = OUT_TM
    assert T % tm == 0 and seq % tm == 0
    resident = functools.partial(pl.BlockSpec, pipeline_mode=pl.Buffered(1))
    tok = pl.BlockSpec((tm, D), lambda i: (i, 0))
    tiles_per_seq = seq // tm
    return pl.pallas_call(
        _out_kernel,
        grid=(T // tm,),
        in_specs=[
            tok, tok, tok, tok,
            pl.BlockSpec((1, 3, D), lambda i: (i // tiles_per_seq, 0, 0)),
            resident((D, D), lambda i: (0, 0)),
            resident((D, D), lambda i: (0, 0)),
            resident((1, D), lambda i: (0, 0)),
        ],
        out_specs=tok,
        out_shape=jax.ShapeDtypeStruct((T, D), F32),
        compiler_params=pltpu.CompilerParams(dimension_semantics=("arbitrary",),
                                             vmem_limit_bytes=VMEM_LIMIT_BYTES),
        name="merge_out",
    )(x2, g1, sgl, mg, mod3, w_br_gla, w_out, final_g)


def _layer(x, c, norm_g, w_ada, b_ada, w_in, alpha_fw_w, alpha_fw_b, alpha_bw_w, alpha_bw_b,
           gla_norm_g, gmlp_ln_g, gmlp_ln_b, gmlp_ws, gmlp_bs, w_br_gla, w_br_gmlp, w_out, out_g):
    B, S, D = x.shape
    H, hv, R, DK = GLA_HEADS, GLA_HV, GLA_RANK, GLA_DK
    T = B * S

    o_ra = 2 * DK + 2 * D
    assert o_ra == N_LO_GROUPS * D
    w_t = w_in.T
    w_main = _weight_prep(w_t)
    w_ra = jnp.pad(w_t[o_ra:o_ra + 2 * R], ((0, RA_PAD - 2 * R), (0, 0))).astype(BF16)
    zeros = jnp.zeros_like(alpha_fw_w)
    wa = jnp.concatenate([jnp.concatenate([alpha_fw_w, zeros], axis=1),
                          jnp.concatenate([zeros, alpha_bw_w], axis=1)], axis=0)
    wa = jnp.pad(wa, ((0, RA_PAD - 2 * R), (0, 0))).astype(BF16)
    ba = jnp.concatenate([alpha_fw_b, alpha_bw_b]).reshape(1, 2 * DK)
    bsm = jnp.repeat(gmlp_bs.T, GMLP_GC, axis=1)

    mod3 = _adaln_mod(c, w_ada, b_ada).reshape(B, 3, D)
    x2 = x.reshape(T, D)
    gl, d, v, gz, sgl, mg = _projection(
        x2, mod3, norm_g.reshape(1, D), w_main, w_ra, wa, ba, gmlp_ln_g.reshape(1, D), gmlp_ln_b.reshape(1, D),
        gmlp_ws.astype(BF16), bsm, w_br_gmlp.astype(BF16), S)
    g1 = _gla(gl.reshape(B, S, N_GLA_OPERANDS * DK), v.reshape(B, S, D), d.reshape(B, S // GLA_CHUNK, 2 * DK),
              gz.reshape(B, S, D), gla_norm_g.reshape(H, 1, hv))
    out = _merge_out(x2, g1.reshape(T, D), sgl, mg, mod3, w_br_gla.astype(BF16), w_out.astype(BF16),
                     out_g.reshape(1, D), S)
    return out.reshape(B, S, D)


def kernel(x, c, norm_g, w_ada, b_ada, w_in, alpha_fw_w, alpha_fw_b, alpha_bw_w, alpha_bw_b, gla_norm_g,
           gmlp_ln_g, gmlp_ln_b, gmlp_ws, gmlp_bs, w_br_gla, w_br_gmlp, w_out, final_g):
    depth = norm_g.shape[0]
    assert depth == 1, "the final rmsnorm is fused into the layer's output kernel"
    return _layer(x, c, norm_g[0], w_ada[0], b_ada[0], w_in[0], alpha_fw_w[0], alpha_fw_b[0], alpha_bw_w[0],
                  alpha_bw_b[0], gla_norm_g[0], gmlp_ln_g[0], gmlp_ln_b[0], gmlp_ws[0], gmlp_bs[0],
                  w_br_gla[0], w_br_gmlp[0], w_out[0], final_g)
```

```python
import functools

import jax
import jax.numpy as jnp
from jax import lax
from jax.experimental import pallas as pl
from jax.experimental.pallas import tpu as pltpu

F32 = jnp.float32
BF16 = jnp.bfloat16

D_MODEL = 1024
GLA_HEADS = 4
GLA_HK = 128
GLA_HV = 256
GLA_DK = GLA_HEADS * GLA_HK
GLA_RANK = 16
GLA_TAU = 16.0
GLA_CHUNK = 64
LOG_DECAY_FLOOR = -1.25
GMLP_GROUPS = 8
GMLP_GC = 128
GMLP_CHUNK = 128
EPS = 1e-6
SQRT_TWO = 1.4142135623730951
LOG2_E = 1.4426950408889634

RA_PAD = 128
MXU_N = 256
N_LO_GROUPS = 3
N_FULL_SCALE_GROUPS = 2
VMEM_LIMIT_BYTES = 56 * 1024 * 1024

PROJ_TM = 512
OUT_TM = 1024
OUT_SUB = 256
GLA_BLK = 256
CHUNKS_PER_BLK = GLA_BLK // GLA_CHUNK
CHUNK_SHIFT = GLA_CHUNK.bit_length() - 1
N_GLA_OPERANDS = 6
GLA_HEADS_PER_STEP = 2


def _sigmoid_h(hz):
    return 0.5 * jnp.tanh(hz) + 0.5


def _silu_h(hz):
    return hz * jnp.tanh(hz) + hz


def _gelu_exact_h(hz):
    return hz * (1.0 + lax.erf(hz * SQRT_TWO))


def _log2_decay(z):
    soft = jnp.log(1.0 + jnp.exp(-z))
    return jnp.maximum(soft * (-LOG2_E / GLA_TAU), LOG_DECAY_FLOOR * LOG2_E)


def _dot(a, b):
    return jnp.dot(a, b, preferred_element_type=F32)


def _dot_nt(a, b):
    return lax.dot_general(a, b, (((1,), (1,)), ((), ())), preferred_element_type=F32)


def _dot_tn(a, b):
    return lax.dot_general(a, b, (((0,), (0,)), ((), ())), preferred_element_type=F32)


def _aligned_rows(start, size):
    if isinstance(start, int):
        return pl.ds(start, size)
    return pl.ds(pl.multiple_of(start, size), size)


def _block_iotas():
    ti = lax.broadcasted_iota(jnp.int32, (GLA_BLK, GLA_BLK), 0)
    si = lax.broadcasted_iota(jnp.int32, (GLA_BLK, GLA_BLK), 1)
    return ti, si


def _mod_kernel(c_ref, w_ref, b_ref, o_ref):
    a = _silu_h(0.5 * c_ref[...]).astype(BF16)
    o_ref[...] = _dot(a, w_ref[...].astype(BF16)) + b_ref[...]


def _adaln_mod(c, w_ada, b_ada):
    B, D = c.shape
    n_out = w_ada.shape[1]
    return pl.pallas_call(
        _mod_kernel,
        grid=(n_out // D,),
        in_specs=[
            pl.BlockSpec((B, D), lambda j: (0, 0)),
            pl.BlockSpec((D, D), lambda j: (0, j)),
            pl.BlockSpec((1, D), lambda j: (0, j)),
        ],
        out_specs=pl.BlockSpec((B, D), lambda j: (0, j)),
        out_shape=jax.ShapeDtypeStruct((B, n_out), F32),
        compiler_params=pltpu.CompilerParams(dimension_semantics=("arbitrary",)),
        name="adaln_mod",
    )(c, w_ada, b_ada.reshape(1, n_out))


def _store_column_tiles(o_ref, w):
    for t in range(o_ref.shape[0]):
        o_ref[t] = w[:, t * MXU_N:(t + 1) * MXU_N]


def _weight_prep_kernel(a_ref, b_ref, o_ref):
    skip = 2 * GLA_RANK
    scale = jnp.where(pl.program_id(0) >= N_FULL_SCALE_GROUPS, 0.5, 1.0).astype(F32)

    @pl.when(pl.program_id(0) < N_LO_GROUPS)
    def _():
        _store_column_tiles(o_ref, (a_ref[...] * scale).T.astype(BF16))

    @pl.when(pl.program_id(0) >= N_LO_GROUPS)
    def _():
        _store_column_tiles(o_ref, (jnp.concatenate([a_ref[skip:, :], b_ref[...]], axis=0) * scale).T.astype(BF16))


def _weight_prep(w_t):
    K = w_t.shape[1]
    D = D_MODEL
    skip = 2 * GLA_RANK
    n_groups = (w_t.shape[0] - skip) // D
    return pl.pallas_call(
        _weight_prep_kernel,
        grid=(n_groups,),
        in_specs=[
            pl.BlockSpec((D, K), lambda g: (g, 0)),
            pl.BlockSpec((skip, K), lambda g: ((g + 1) * (D // skip), 0)),
        ],
        out_specs=pl.BlockSpec((D // MXU_N, K, MXU_N), lambda g: (g, 0, 0)),
        out_shape=jax.ShapeDtypeStruct((n_groups * D // MXU_N, K, MXU_N), BF16),
        compiler_params=pltpu.CompilerParams(dimension_semantics=("arbitrary",),
                                             vmem_limit_bytes=VMEM_LIMIT_BYTES),
        name="weight_prep",
    )(w_t, w_t)


def _proj_kernel(x_ref, mod_ref, ng_ref, wm_ref, wra_ref, wa_ref, ba_ref, lng_ref, lnb_ref, ws_ref, bsm_ref,
                 wbg_ref, tri_ref, gl_ref, d_ref, v_ref, gz_ref, sgl_ref, mg_ref, sg_scr):
    D = D_MODEL
    DK = GLA_DK
    hk = GLA_HK
    C = GLA_CHUNK
    tm = x_ref.shape[0]
    half = tm // 2
    shift = mod_ref[0, 0:1, :]
    gain = ng_ref[...] * (1.0 + mod_ref[0, 1:2, :])

    def normed(rows):
        x = x_ref[rows, :]
        r = lax.rsqrt(jnp.mean(x * x, axis=-1, keepdims=True) + EPS)
        return ((x * r) * gain + shift).astype(BF16)

    h_top = normed(slice(0, half))
    h_bot = normed(slice(half, tm))
    h = jnp.concatenate([h_top, h_bot], axis=0)

    def proj(j, lhs=None):
        lhs = h if lhs is None else lhs
        tiles = D // MXU_N
        return jnp.concatenate([_dot(lhs, wm_ref[j * tiles + t]) for t in range(tiles)], axis=1)


    qk = jnp.concatenate([proj(0, h_top), proj(0, h_bot)], axis=0)
    ra = jnp.concatenate([_dot_nt(h_top, wra_ref[...]), _dot_nt(h_bot, wra_ref[...])], axis=0).astype(BF16)
    pre = _dot(ra, wa_ref[...]) + ba_ref[...]
    la = _log2_decay(pre)

    v_ref[...] = proj(1).astype(BF16)

    hi = la.astype(BF16)
    r1 = la - hi.astype(F32)
    mid = r1.astype(BF16)
    lo = (r1 - mid.astype(F32)).astype(BF16)

    u_pre = proj(3)

    tri_f = tri_ref[0]
    tri_b = tri_ref[1]
    cum = []
    for blk in range(tm // GLA_BLK):
        rs = slice(blk * GLA_BLK, (blk + 1) * GLA_BLK)
        b_f = _dot(tri_f, hi[rs, :DK]) + _dot(tri_f, mid[rs, :DK]) + _dot(tri_f, lo[rs, :DK])
        b_b = _dot(tri_b, hi[rs, DK:]) + _dot(tri_b, mid[rs, DK:]) + _dot(tri_b, lo[rs, DK:])
        cum.append((b_f, b_b))

    sgl_ref[...] = _sigmoid_h(proj(6)).astype(BF16)

    q_scale = GLA_HK ** -0.5
    for blk in range(tm // GLA_BLK):
        b_f, b_b = cum[blk]
        for c in range(CHUNKS_PER_BLK):
            cs = slice(c * C, (c + 1) * C)
            gr = slice(blk * GLA_BLK + c * C, blk * GLA_BLK + (c + 1) * C)
            n = blk * CHUNKS_PER_BLK + c
            q = qk[gr, :DK] * q_scale
            k = qk[gr, DK:]
            bf = b_f[cs]
            bb = b_b[cs]
            bf_last = bf[C - 1:C]
            bb_last = bb[0:1]
            operands = (q * jnp.exp2(bf), k * jnp.exp2(-bf), k * jnp.exp2(bf_last - bf),
                        q * jnp.exp2(bb), k * jnp.exp2(-bb), k * jnp.exp2(bb_last - bb))
            decays = (jnp.exp2(bf_last), jnp.exp2(bb_last))
            for hd in range(GLA_HEADS):
                hs = slice(hd * hk, (hd + 1) * hk)
                for g, val in enumerate(operands):
                    lane0 = (hd * N_GLA_OPERANDS + g) * hk
                    gl_ref[gr, lane0:lane0 + hk] = val[:, hs].astype(BF16)
                for g, val in enumerate(decays):
                    lane0 = (hd * len(decays) + g) * hk
                    d_ref[n:n + 1, lane0:lane0 + hk] = val[:, hs]

    zg = proj(5)

    vs = _gelu_exact_h(proj(4))
    mu = jnp.mean(vs, axis=-1, keepdims=True)
    vc = vs - mu
    var = jnp.mean(vc * vc, axis=-1, keepdims=True)
    vsn = ((vc * lax.rsqrt(var + EPS)) * lng_ref[...] + lnb_ref[...]).astype(BF16)

    gz_ref[...] = _silu_h(proj(2)).astype(BF16)

    n_pos = tm // GMLP_CHUNK
    for g in range(GMLP_GROUPS):
        cs = slice(g * GMLP_GC, (g + 1) * GMLP_GC)
        rhs = jnp.concatenate([vsn[n * GMLP_CHUNK:(n + 1) * GMLP_CHUNK, cs] for n in range(n_pos)], axis=1)
        mixed = _dot(ws_ref[g], rhs)
        for n in range(n_pos):
            sg_scr[n * GMLP_CHUNK:(n + 1) * GMLP_CHUNK, cs] = (
                mixed[:, n * GMLP_GC:(n + 1) * GMLP_GC] + bsm_ref[:, cs])
    merge_gate = _sigmoid_h(proj(7))

    g2 = ((_gelu_exact_h(u_pre) * sg_scr[...]) * _silu_h(zg)).astype(BF16)
    mg_ref[...] = (merge_gate * _dot(g2, wbg_ref[...])).astype(BF16)


def _cumsum_matrices():
    ti, si = _block_iotas()
    same_chunk = (ti >> CHUNK_SHIFT) == (si >> CHUNK_SHIFT)
    return jnp.stack([same_chunk & (si <= ti), same_chunk & (si >= ti)]).astype(BF16)


def _projection(x2, mod3, norm_g, w_main, w_ra, wa, ba, ln_g, ln_b, ws, bsm, w_br_gmlp, seq):
    T, D = x2.shape
    tm = PROJ_TM
    assert T % tm == 0 and seq % tm == 0 and tm % GMLP_CHUNK == 0 and tm % GLA_BLK == 0
    resident = functools.partial(pl.BlockSpec, pipeline_mode=pl.Buffered(1))
    tok = lambda w: pl.BlockSpec((tm, w), lambda i: (i, 0))
    tiles_per_seq = seq // tm
    out_bf = jax.ShapeDtypeStruct((T, D), BF16)
    return pl.pallas_call(
        _proj_kernel,
        grid=(T // tm,),
        in_specs=[
            tok(D),
            pl.BlockSpec((1, 3, D), lambda i: (i // tiles_per_seq, 0, 0)),
            resident((1, D), lambda i: (0, 0)),
            resident(w_main.shape, lambda i: (0, 0, 0)),
            resident(w_ra.shape, lambda i: (0, 0)),
            resident(wa.shape, lambda i: (0, 0)),
            resident(ba.shape, lambda i: (0, 0)),
            resident((1, D), lambda i: (0, 0)),
            resident((1, D), lambda i: (0, 0)),
            resident(ws.shape, lambda i: (0, 0, 0)),
            resident(bsm.shape, lambda i: (0, 0)),
            resident(w_br_gmlp.shape, lambda i: (0, 0)),
            resident((2, GLA_BLK, GLA_BLK), lambda i: (0, 0, 0)),
        ],
        out_specs=[tok(N_GLA_OPERANDS * GLA_DK), pl.BlockSpec((tm // GLA_CHUNK, 2 * GLA_DK), lambda i: (i, 0)),
                   tok(D), tok(D), tok(D), tok(D)],
        out_shape=[jax.ShapeDtypeStruct((T, N_GLA_OPERANDS * GLA_DK), BF16),
                   jax.ShapeDtypeStruct((T // GLA_CHUNK, 2 * GLA_DK), F32),
                   out_bf, out_bf, out_bf, out_bf],
        scratch_shapes=[pltpu.VMEM((tm, D), F32)],
        compiler_params=pltpu.CompilerParams(dimension_semantics=("arbitrary",),
                                             vmem_limit_bytes=VMEM_LIMIT_BYTES),
        name="projection_gmlp",
    )(x2, mod3, norm_g, w_main, w_ra, wa, ba, ln_g, ln_b, ws, bsm, w_br_gmlp, _cumsum_matrices())


def _gla_kernel(gl_ref, v_ref, d_ref, gz_ref, gng_ref, o_ref, of_scr, ob_scr, kvf_scr, kvb_scr):
    S = v_ref.shape[1]
    C = GLA_CHUNK
    hk = GLA_HK
    n_blk = S // GLA_BLK
    ti, si = _block_iotas()
    chunk_lo = (ti >> CHUNK_SHIFT) << CHUNK_SHIFT
    chunk_hi = chunk_lo + (C - 1)
    pos_chunk = lax.broadcasted_iota(jnp.int32, (hk, GLA_BLK), 1) >> CHUNK_SHIFT
    chunk_sel = [jnp.where(pos_chunk == c, 1.0, 0.0).astype(BF16) for c in range(CHUNKS_PER_BLK)]
    d_cols = d_ref[0].T
    hv = GLA_HV

    def direction(hd, forward):
        lane0 = (hd * N_GLA_OPERANDS + (0 if forward else N_GLA_OPERANDS // 2)) * hk
        row0 = (2 * hd + (0 if forward else 1)) * hk
        return (hd, lane0, lane0 + hk, lane0 + 2 * hk, d_cols[row0:row0 + hk],
                of_scr if forward else ob_scr, kvf_scr if forward else kvb_scr)

    def operand(rows, lane0):
        return gl_ref[0, rows, lane0:lane0 + hk]

    def values(hd, rows):
        return v_ref[0, rows, hd * hv:(hd + 1) * hv]

    def scores(j, ctx, forward):
        rows = _aligned_rows(j * GLA_BLK, GLA_BLK)
        a = _dot_nt(operand(rows, ctx[1]), operand(rows, ctx[2]))
        if forward:
            a = jnp.where(si <= ti, jnp.where(si >= chunk_lo, a, 0.0), 0.0)
        else:
            a = jnp.where(si > ti, jnp.where(si <= chunk_hi, a, 0.0), 0.0)
        return a.astype(BF16)

    def chunk_states(j, ctx):
        rows = _aligned_rows(j * GLA_BLK, GLA_BLK)
        ke = operand(rows, ctx[3])
        ke_t = ke.T
        ke_bd = jnp.concatenate([ke_t * chunk_sel[c] for c in range(CHUNKS_PER_BLK)], axis=0)
        ctx[6][j] = _dot(ke_bd, values(ctx[0], rows))

    def weighted_values(j, a, ctx):
        rows = _aligned_rows(j * GLA_BLK, GLA_BLK)
        ctx[5][ctx[0], rows, :] = _dot(a, values(ctx[0], rows))

    def scan(j, state, ctx, forward):
        hd, qt_lane, _, _, decays, o_scr, kv_scr = ctx
        o_inter = [None] * CHUNKS_PER_BLK
        for c in (range(CHUNKS_PER_BLK) if forward else reversed(range(CHUNKS_PER_BLK))):
            rows = _aligned_rows(j * GLA_BLK + c * C, C)
            n = j * CHUNKS_PER_BLK + c
            o_inter[c] = _dot(operand(rows, qt_lane), state.astype(BF16))
            state = state * decays[:, n:n + 1] + kv_scr[j, c * hk:(c + 1) * hk, :]
        o_scr[hd, _aligned_rows(j * GLA_BLK, GLA_BLK), :] += jnp.concatenate(o_inter, axis=0)
        return state

    def finish(hd, j):
        rows = _aligned_rows(j * GLA_BLK, GLA_BLK)
        lanes = slice(hd * hv, (hd + 1) * hv)
        o = of_scr[hd, rows, :] + ob_scr[hd, rows, :]
        on = o * lax.rsqrt(jnp.mean(o * o, axis=-1, keepdims=True) + EPS) * gng_ref[hd]
        o_ref[0, rows, lanes] = (on * gz_ref[0, rows, lanes].astype(F32)).astype(BF16)

    steps = [(hd, i) for hd in range(GLA_HEADS_PER_STEP) for i in range(n_blk)]
    sf = sb = None
    for cur, nxt in zip([None] + steps, steps + [None]):
        if nxt is not None:
            n_fwd, n_bwd = direction(nxt[0], True), direction(nxt[0], False)
            a_f = scores(nxt[1], n_fwd, True)
            a_b = scores(n_blk - 1 - nxt[1], n_bwd, False)
        if cur is not None:
            hd, i = cur
            if i == 0:
                sf = sb = jnp.zeros((hk, hv), F32)
            sf = scan(i, sf, direction(hd, True), True)
            sb = scan(n_blk - 1 - i, sb, direction(hd, False), False)
        if nxt is not None:
            chunk_states(nxt[1], n_fwd)
            chunk_states(n_blk - 1 - nxt[1], n_bwd)
            weighted_values(nxt[1], a_f, n_fwd)
            weighted_values(n_blk - 1 - nxt[1], a_b, n_bwd)
        if cur is not None:
            for j in sorted({i, n_blk - 1 - i}):
                if max(j, n_blk - 1 - j) == i:
                    finish(hd, j)


def _gla(gl3, v3, d3, gz3, gng):
    B, S, D = v3.shape
    H, hk, hv = GLA_HEADS, GLA_HK, GLA_HV
    n_chunks = S // GLA_CHUNK
    assert S % GLA_BLK == 0
    hps = GLA_HEADS_PER_STEP
    assert H % hps == 0
    head_v = pl.BlockSpec((1, S, hps * hv), lambda b, h: (b, 0, h))
    kv_scratch = pltpu.VMEM((S // GLA_BLK, CHUNKS_PER_BLK * hk, hv), F32)
    o_scratch = pltpu.VMEM((hps, S, hv), F32)
    return pl.pallas_call(
        _gla_kernel,
        grid=(B, H // hps),
        in_specs=[
            pl.BlockSpec((1, S, hps * N_GLA_OPERANDS * hk), lambda b, h: (b, 0, h)),
            head_v,
            pl.BlockSpec((1, n_chunks, hps * 2 * hk), lambda b, h: (b, 0, h)),
            head_v,
            pl.BlockSpec((hps, 1, hv), lambda b, h: (h, 0, 0)),
        ],
        out_specs=head_v,
        out_shape=jax.ShapeDtypeStruct((B, S, D), BF16),
        scratch_shapes=[o_scratch, o_scratch, kv_scratch, kv_scratch],
        compiler_params=pltpu.CompilerParams(dimension_semantics=("arbitrary", "arbitrary"),
                                             vmem_limit_bytes=VMEM_LIMIT_BYTES),
        name="gla_bidir",
    )(gl3, v3, d3, gz3, gng)


def _out_kernel(x_ref, g1_ref, sgl_ref, mg_ref, mod_ref, wbr_ref, wout_ref, fg_ref, o_ref):
    gate = mod_ref[0, 2:3, :]
    n_sub = x_ref.shape[0] // OUT_SUB
    subs = [slice(s * OUT_SUB, (s + 1) * OUT_SUB) for s in range(n_sub)]
    y_gla = [None] * n_sub
    y_gla[0] = _dot(g1_ref[subs[0], :], wbr_ref[...])
    for s, rows in enumerate(subs):
        if s + 1 < n_sub:
            y_gla[s + 1] = _dot(g1_ref[subs[s + 1], :], wbr_ref[...])
        merged = (sgl_ref[rows, :].astype(F32) * y_gla[s] + mg_ref[rows, :].astype(F32)).astype(BF16)
        r = x_ref[rows, :] + gate * _dot(merged, wout_ref[...])
        o_ref[rows, :] = (r * lax.rsqrt(jnp.mean(r * r, axis=-1, keepdims=True) + EPS)) * fg_ref[...]


def _merge_out(x2, g1, sgl, mg, mod3, w_br_gla, w_out, final_g, seq):
    T, D = x2.shape
    tm = OUT_TM
    assert T % tm == 0 and seq % tm == 0
    resident = functools.partial(pl.BlockSpec, pipeline_mode=pl.Buffered(1))
    tok = pl.BlockSpec((tm, D), lambda i: (i, 0))
    tiles_per_seq = seq // tm
    return pl.pallas_call(
        _out_kernel,
        grid=(T // tm,),
        in_specs=[
            tok, tok, tok, tok,
            pl.BlockSpec((1, 3, D), lambda i: (i // tiles_per_seq, 0, 0)),
            resident((D, D), lambda i: (0, 0)),
            resident((D, D), lambda i: (0, 0)),
            resident((1, D), lambda i: (0, 0)),
        ],
        out_specs=tok,
        out_shape=jax.ShapeDtypeStruct((T, D), F32),
        compiler_params=pltpu.CompilerParams(dimension_semantics=("arbitrary",),
                                             vmem_limit_bytes=VMEM_LIMIT_BYTES),
        name="merge_out",
    )(x2, g1, sgl, mg, mod3, w_br_gla, w_out, final_g)


def _layer(x, c, norm_g, w_ada, b_ada, w_in, alpha_fw_w, alpha_fw_b, alpha_bw_w, alpha_bw_b,
           gla_norm_g, gmlp_ln_g, gmlp_ln_b, gmlp_ws, gmlp_bs, w_br_gla, w_br_gmlp, w_out, out_g):
    B, S, D = x.shape
    H, hv, R, DK = GLA_HEADS, GLA_HV, GLA_RANK, GLA_DK
    T = B * S

    o_ra = 2 * DK + 2 * D
    assert o_ra == N_LO_GROUPS * D
    w_t = w_in.T
    w_main = _weight_prep(w_t)
    w_ra = jnp.pad(w_t[o_ra:o_ra + 2 * R], ((0, RA_PAD - 2 * R), (0, 0))).astype(BF16)
    zeros = jnp.zeros_like(alpha_fw_w)
    wa = jnp.concatenate([jnp.concatenate([alpha_fw_w, zeros], axis=1),
                          jnp.concatenate([zeros, alpha_bw_w], axis=1)], axis=0)
    wa = jnp.pad(wa, ((0, RA_PAD - 2 * R), (0, 0))).astype(BF16)
    ba = jnp.concatenate([alpha_fw_b, alpha_bw_b]).reshape(1, 2 * DK)
    bsm = jnp.repeat(gmlp_bs.T, GMLP_GC, axis=1)

    mod3 = _adaln_mod(c, w_ada, b_ada).reshape(B, 3, D)
    x2 = x.reshape(T, D)
    gl, d, v, gz, sgl, mg = _projection(
        x2, mod3, norm_g.reshape(1, D), w_main, w_ra, wa, ba, gmlp_ln_g.reshape(1, D), gmlp_ln_b.reshape(1, D),
        gmlp_ws.astype(BF16), bsm, w_br_gmlp.astype(BF16), S)
    g1 = _gla(gl.reshape(B, S, N_GLA_OPERANDS * DK), v.reshape(B, S, D), d.reshape(B, S // GLA_CHUNK, 2 * DK),
              gz.reshape(B, S, D), gla_norm_g.reshape(H, 1, hv))
    out = _merge_out(x2, g1.reshape(T, D), sgl, mg, mod3, w_br_gla.astype(BF16), w_out.astype(BF16),
                     out_g.reshape(1, D), S)
    return out.reshape(B, S, D)


def kernel(x, c, norm_g, w_ada, b_ada, w_in, alpha_fw_w, alpha_fw_b, alpha_bw_w, alpha_bw_b, gla_norm_g,
           gmlp_ln_g, gmlp_ln_b, gmlp_ws, gmlp_bs, w_br_gla, w_br_gmlp, w_out, final_g):
    depth = norm_g.shape[0]
    assert depth == 1, "the final rmsnorm is fused into the layer's output kernel"
    return _layer(x, c, norm_g[0], w_ada[0], b_ada[0], w_in[0], alpha_fw_w[0], alpha_fw_b[0], alpha_bw_w[0],
                  alpha_bw_b[0], gla_norm_g[0], gmlp_ln_g[0], gmlp_ln_b[0], gmlp_ws[0], gmlp_bs[0],
                  w_br_gla[0], w_br_gmlp[0], w_out[0], final_g)
```
